```python
import math
import jax
import jax.numpy as jnp
from jax import lax
import numpy as np

D_MODEL = 1024
BATCH = 2
SEQ = 8192
DEPTH = 1
DEC_BATCH = 128
DEC_SEQ = 1
PAST_LEN = 16384
PAGE_SIZE = 128

H_A = 16
Q_LORA = 384
KV_LORA = 256
QK_NOPE = 64
QK_ROPE = 32
QK_HEAD = QK_NOPE + QK_ROPE
V_HEAD = 64
ROPE_THETA = 10000.0
Q_BLOCK = 128
D_INNER = 2 * D_MODEL
SSM_HEAD = 64
H_B = D_INNER // SSM_HEAD
N_GROUPS = 4
D_STATE = 128
CONV_W = 4
CONV_DIM = D_INNER + 2 * N_GROUPS * D_STATE
SSD_CHUNK = 128
N_EXPERTS = 32
TOP_K = 4
D_FF = D_MODEL
SWIGLU_LIMIT = 7.0
SWIGLU_ALPHA = 1.702
MOE_BLOCK = 128
PLE_DIM = 256
EPS = 1e-6
D_IN_PROJ = Q_LORA + KV_LORA + QK_ROPE + D_INNER + CONV_DIM + H_B + 2 * D_MODEL
F32 = jnp.float32

kernel_name = 'hybrid_mla_mamba2_moe_ple_step'


def rmsnorm(x, g):
    xf = x.astype(F32)
    y = xf * lax.rsqrt(jnp.mean(xf * xf, axis=-1, keepdims=True) + EPS)
    return (y * g.astype(F32)).astype(x.dtype)


def rope(x, pos):
    half = x.shape[-1] // 2
    inv = ROPE_THETA ** (-jnp.arange(half, dtype=F32) / half)
    ang = pos.astype(F32)[:, None] * inv[None, :]
    cos = jnp.cos(ang)[None, :, None, :]
    sin = jnp.sin(ang)[None, :, None, :]
    xf = x.astype(F32)
    x1, x2 = xf[..., :half], xf[..., half:]
    return jnp.concatenate([x1 * cos - x2 * sin, x1 * sin + x2 * cos], axis=-1).astype(x.dtype)


def split_in_proj(proj):
    sizes = (Q_LORA, KV_LORA, QK_ROPE, D_INNER, CONV_DIM, H_B, 2 * D_MODEL)
    cuts = [int(c) for c in np.cumsum(sizes)[:-1]]
    return jnp.split(proj, cuts, axis=-1)


def mla_queries(q_lat, pos, lp):
    q = rmsnorm(q_lat, lp['g_q_a']) @ lp['w_q_b']
    q = q.reshape(q.shape[:-1] + (H_A, QK_HEAD))
    q = jnp.concatenate([q[..., :QK_NOPE], rope(q[..., QK_NOPE:], pos)], axis=-1)
    return rmsnorm(q, lp['g_q_head'])


def mla_latent(kv_lat, k_rope_raw, pos, lp):
    c = rmsnorm(kv_lat, lp['g_kv_a'])
    kr = rope(k_rope_raw[:, :, None, :], pos)[:, :, 0, :]
    return c, kr


def mla_keys(c, kr, lp):
    k_nope = (c @ lp['w_uk']).reshape(c.shape[:-1] + (H_A, QK_NOPE))
    k_r = jnp.broadcast_to(kr[..., None, :], k_nope.shape[:-1] + (QK_ROPE,))
    return rmsnorm(jnp.concatenate([k_nope, k_r], axis=-1), lp['g_k_head'])


def mla_prompt_attention(q, k, v):
    b, s = q.shape[:2]
    nb = s // Q_BLOCK
    qb = jnp.moveaxis(q.reshape(b, nb, Q_BLOCK, H_A, QK_HEAD), 1, 0)
    kpos = jnp.arange(s)
    scale = QK_HEAD ** -0.5

    def block(args):
        qi, i = args
        sc = jnp.einsum('bqhd,bkhd->bhqk', qi, k, preferred_element_type=F32) * scale
        qpos = i * Q_BLOCK + jnp.arange(Q_BLOCK)
        sc = jnp.where(kpos[None, :] <= qpos[:, None], sc, -jnp.inf)
        p = jax.nn.softmax(sc, axis=-1).astype(v.dtype)
        return jnp.einsum('bhqk,bkhv->bqhv', p, v)

    o = lax.map(block, (qb, jnp.arange(nb)))
    return jnp.moveaxis(o, 0, 1).reshape(b, s, H_A * V_HEAD)


def online_update(carry, s, c_rows):
    m, l, acc = carry
    m_new = jnp.maximum(m, jnp.max(s, axis=-1))
    corr = jnp.exp(m - m_new)
    p = jnp.exp(s - m_new[..., None])
    l = l * corr + jnp.sum(p, axis=-1)
    acc = acc * corr[..., None] + jnp.einsum('bhqk,bkc->bhqc', p, c_rows.astype(F32))
    return m_new, l, acc


def mla_sample_attention(q, c_new, kr_new, cache_c, cache_kr, page_table, lp):
    bd, sd = q.shape[:2]
    scale = QK_HEAD ** -0.5

    def page_step(carry, phys):
        c_pg = cache_c[phys]
        kr_pg = cache_kr[phys]
        k = mla_keys(c_pg, kr_pg, lp)
        sc = jnp.einsum('bqhd,bkhd->bhqk', q, k, preferred_element_type=F32) * scale
        return online_update(carry, sc, c_pg), None

    init = (jnp.full((bd, H_A, sd), -jnp.inf, F32),
            jnp.zeros((bd, H_A, sd), F32),
            jnp.zeros((bd, H_A, sd, KV_LORA), F32))
    carry, _ = lax.scan(page_step, init, page_table.T)
    k_new = mla_keys(c_new, kr_new, lp)
    sc = jnp.einsum('bqhd,bkhd->bhqk', q, k_new, preferred_element_type=F32) * scale
    causal = jnp.arange(sd)[None, :] <= jnp.arange(sd)[:, None]
    sc = jnp.where(causal, sc, -jnp.inf)
    _, l, acc = online_update(carry, sc, c_new)
    ctx = acc / l[..., None]
    w_uv = lp['w_uv'].astype(F32).reshape(KV_LORA, H_A, V_HEAD)
    o = jnp.einsum('bhqc,chv->bqhv', ctx, w_uv)
    return o.reshape(bd, sd, H_A * V_HEAD).astype(q.dtype)


def causal_conv(xpad, w, b):
    y = lax.conv_general_dilated(xpad, w[:, None, :], window_strides=(1,), padding='VALID',
                                 dimension_numbers=('NWC', 'WIO', 'NWC'),
                                 feature_group_count=xpad.shape[-1])
    return jax.nn.silu(y + b)


def split_xbc(xc):
    lead = xc.shape[:-1]
    gn = N_GROUPS * D_STATE
    xs = xc[..., :D_INNER]
    bm = xc[..., D_INNER:D_INNER + gn].reshape(lead + (N_GROUPS, D_STATE))
    cm = xc[..., D_INNER + gn:].reshape(lead + (N_GROUPS, D_STATE))
    return xs, bm, cm


def gated_norm(y, z, g):
    lead = y.shape[:-1]
    yg = (y * jax.nn.silu(z.astype(F32))).reshape(lead + (N_GROUPS, D_INNER // N_GROUPS))
    yg = yg * lax.rsqrt(jnp.mean(yg * yg, axis=-1, keepdims=True) + EPS)
    return yg.reshape(lead + (D_INNER,)) * g.astype(F32)


def segsum(a):
    t = a.shape[-1]
    x = jnp.broadcast_to(a[..., :, None], a.shape + (t,))
    x = jnp.where(jnp.tril(jnp.ones((t, t), bool), -1), x, 0.0)
    cs = jnp.cumsum(x, axis=-2)
    return jnp.where(jnp.tril(jnp.ones((t, t), bool)), cs, -jnp.inf)


def ssd_chunked(x, dt, a_head, bm, cm):
    b, seqlen, nh, hp = x.shape
    n = bm.shape[-1]
    r = nh // N_GROUPS
    nc = seqlen // SSD_CHUNK
    x = x.reshape(b, nc, SSD_CHUNK, N_GROUPS, r, hp)
    dt = dt.reshape(b, nc, SSD_CHUNK, N_GROUPS, r)
    bm = bm.reshape(b, nc, SSD_CHUNK, N_GROUPS, n)
    cm = cm.reshape(b, nc, SSD_CHUNK, N_GROUPS, n)
    xdt = x * dt[..., None]
    a = jnp.moveaxis(dt * a_head.reshape(N_GROUPS, r), 2, -1)
    a_cum = jnp.cumsum(a, axis=-1)
    cb = jnp.einsum('bclgn,bcsgn->bcgls', cm, bm)
    scores = cb[:, :, :, None] * jnp.exp(segsum(a))
    y_diag = jnp.einsum('bcgrls,bcsgrp->bclgrp', scores, xdt)
    decay_to_end = jnp.moveaxis(jnp.exp(a_cum[..., -1:] - a_cum), -1, 2)
    states = jnp.einsum('bclgn,bclgrp->bcgrpn', bm, xdt * decay_to_end[..., None])
    states = jnp.concatenate([jnp.zeros_like(states[:, :1]), states], axis=1)
    chunk_a = jnp.pad(jnp.moveaxis(a_cum[..., -1], 1, -1), ((0, 0), (0, 0), (0, 0), (1, 0)))
    decay_chunk = jnp.exp(segsum(chunk_a))
    carried = jnp.einsum('bgrzc,bcgrpn->bzgrpn', decay_chunk, states)
    decay_from_start = jnp.moveaxis(jnp.exp(a_cum), -1, 2)
    y_off = jnp.einsum('bclgn,bcgrpn->bclgrp', cm, carried[:, :-1]) * decay_from_start[..., None]
    y = (y_diag + y_off).reshape(b, seqlen, nh, hp)
    return y, carried[:, -1].reshape(b, nh, hp, n)


def ssm_prompt(z, xbc, dt_raw, lp):
    b, s, _ = xbc.shape
    xpad = jnp.pad(xbc, ((0, 0), (CONV_W - 1, 0), (0, 0)))
    conv_state = xpad[:, s:]
    xs, bm, cm = split_xbc(causal_conv(xpad, lp['w_conv'], lp['b_conv']))
    dt = jax.nn.softplus(dt_raw.astype(F32) + lp['dt_bias'].astype(F32))
    a_head = -jnp.exp(lp['a_log'].astype(F32))
    xh = xs.reshape(b, s, H_B, SSM_HEAD).astype(F32)
    y, h_final = ssd_chunked(xh, dt, a_head, bm.astype(F32), cm.astype(F32))
    y = y + lp['d_skip'].astype(F32)[:, None] * xh
    out = gated_norm(y.reshape(b, s, D_INNER), z, lp['g_ssm_norm'])
    return out.astype(z.dtype), conv_state, h_final.astype(z.dtype)


def ssm_sample(z, xbc, dt_raw, conv_buf, ssm_state, lp):
    bd, sd, _ = xbc.shape
    xall = jnp.concatenate([conv_buf.astype(xbc.dtype), xbc], axis=1)
    conv_state = xall[:, -(CONV_W - 1):]
    xs, bm, cm = split_xbc(causal_conv(xall, lp['w_conv'], lp['b_conv']))
    dt = jax.nn.softplus(dt_raw.astype(F32) + lp['dt_bias'].astype(F32))
    a_head = -jnp.exp(lp['a_log'].astype(F32))
    xh = xs.reshape(bd, sd, H_B, SSM_HEAD).astype(F32)
    r = H_B // N_GROUPS

    def step(h, inp):
        x_t, dt_t, b_t, c_t = inp
        d_a = jnp.exp(dt_t * a_head).reshape(bd, N_GROUPS, r)
        xg = (x_t * dt_t[..., None]).reshape(bd, N_GROUPS, r, SSM_HEAD)
        h = h * d_a[..., None, None] + jnp.einsum('bgrp,bgn->bgrpn', xg, b_t)
        y_t = jnp.einsum('bgrpn,bgn->bgrp', h, c_t).reshape(bd, H_B, SSM_HEAD)
        return h, y_t

    h0 = ssm_state.astype(F32).reshape(bd, N_GROUPS, r, SSM_HEAD, D_STATE)
    seq = (jnp.moveaxis(xh, 1, 0), jnp.moveaxis(dt, 1, 0),
           jnp.moveaxis(bm.astype(F32), 1, 0), jnp.moveaxis(cm.astype(F32), 1, 0))
    h_t, ys = lax.scan(step, h0, seq)
    y = jnp.moveaxis(ys, 0, 1) + lp['d_skip'].astype(F32)[:, None] * xh
    out = gated_norm(y.reshape(bd, sd, D_INNER), z, lp['g_ssm_norm'])
    return out.astype(z.dtype), conv_state, h_t.reshape(bd, H_B, SSM_HEAD, D_STATE).astype(z.dtype)


def moe_ffn(v, lp):
    lead = v.shape[:-1]
    t = v.reshape(-1, D_MODEL)
    n_tok = t.shape[0]
    logits = (t @ lp['w_router'] + lp['b_router']).astype(F32)
    top_val, top_idx = lax.top_k(logits, TOP_K)
    gate_w = jax.nn.softmax(top_val, axis=-1)
    n_assign = n_tok * TOP_K
    flat_e = top_idx.reshape(-1)
    order = jnp.argsort(flat_e)
    sorted_e = flat_e[order]
    counts = jnp.bincount(flat_e, length=N_EXPERTS)
    padded = (counts + MOE_BLOCK - 1) // MOE_BLOCK * MOE_BLOCK
    starts = jnp.cumsum(counts) - counts
    pstarts = jnp.cumsum(padded) - padded
    dest = pstarts[sorted_e] + jnp.arange(n_assign) - starts[sorted_e]
    n_blocks = -(-n_assign // MOE_BLOCK) + N_EXPERTS
    rows = n_blocks * MOE_BLOCK
    row_tok = jnp.zeros((rows,), jnp.int32).at[dest].set((order // TOP_K).astype(jnp.int32))
    row_w = jnp.zeros((rows,), F32).at[dest].set(gate_w.reshape(-1)[order])
    block_e = jnp.minimum(jnp.searchsorted(jnp.cumsum(padded), jnp.arange(n_blocks) * MOE_BLOCK, side='right'),
                          N_EXPERTS - 1)
    xb = t[row_tok].reshape(n_blocks, MOE_BLOCK, D_MODEL)

    def expert_block(args):
        xe, e = args
        gu = xe @ lp['w_gate_up'][e] + lp['b_gate_up'][e]
        gate = jnp.minimum(gu[..., :D_FF], SWIGLU_LIMIT)
        up = jnp.clip(gu[..., D_FF:], -SWIGLU_LIMIT, SWIGLU_LIMIT)
        act = (up + 1.0) * gate * jax.nn.sigmoid(SWIGLU_ALPHA * gate)
        return act @ lp['w_down'][e] + lp['b_down'][e]

    out = lax.map(expert_block, (xb, block_e)).reshape(rows, D_MODEL)
    y = jnp.zeros((n_tok, D_MODEL), F32).at[row_tok].add(out.astype(F32) * row_w[:, None])
    return y.astype(v.dtype).reshape(lead + (D_MODEL,))


def merge_branches(x, attn, ssm, gates, lp):
    g = jax.nn.sigmoid(gates.astype(F32)).astype(x.dtype)
    mixed = g[..., :D_MODEL] * (attn @ lp['w_a_out']) + g[..., D_MODEL:] * (ssm @ lp['w_b_out'])
    return x + mixed @ lp['w_out']


def channel_and_ple(h, p_i, lp):
    h = h + moe_ffn(rmsnorm(h, lp['g_ffn_norm']), lp)
    gate = jax.nn.sigmoid((rmsnorm(h, lp['g_ple_in']) @ lp['w_ple_gate']).astype(F32))
    ple = rmsnorm(p_i @ lp['w_ple'], lp['g_ple']).astype(F32) * gate
    return h + ple.astype(h.dtype)


def prompt_layer(x, p_i, pos, lp):
    b, s, _ = x.shape
    u = rmsnorm(x, lp['g_mix_norm'])
    q_lat, kv_lat, k_rope_raw, z, xbc, dt_raw, gates = split_in_proj(u @ lp['w_in'])
    q = mla_queries(q_lat, pos, lp)
    c, kr = mla_latent(kv_lat, k_rope_raw, pos, lp)
    k = mla_keys(c, kr, lp)
    v = (c @ lp['w_uv']).reshape(b, s, H_A, V_HEAD)
    attn = mla_prompt_attention(q, k, v)
    ssm, conv_state, ssm_state = ssm_prompt(z, xbc, dt_raw, lp)
    h = channel_and_ple(merge_branches(x, attn, ssm, gates, lp), p_i, lp)
    return h, c, kr, conv_state, ssm_state


def sample_layer(x, p_i, pos, cache_c, cache_kr, page_table, conv_buf, ssm_state, lp):
    u = rmsnorm(x, lp['g_mix_norm'])
    q_lat, kv_lat, k_rope_raw, z, xbc, dt_raw, gates = split_in_proj(u @ lp['w_in'])
    q = mla_queries(q_lat, pos, lp)
    c, kr = mla_latent(kv_lat, k_rope_raw, pos, lp)
    attn = mla_sample_attention(q, c, kr, cache_c, cache_kr, page_table, lp)
    ssm, conv_state, new_ssm = ssm_sample(z, xbc, dt_raw, conv_buf, ssm_state, lp)
    h = channel_and_ple(merge_branches(x, attn, ssm, gates, lp), p_i, lp)
    return h, c, kr, conv_state, new_ssm


def setup_inputs(seed: int = 0) -> dict:
    key = jax.random.key(seed)
    ks = iter(jax.random.split(key, 64))
    L = DEPTH
    n_pages = PAST_LEN // PAGE_SIZE
    n_pool = (DEC_BATCH * n_pages * 5) // 4

    def nrm(shape, scale):
        return jax.random.normal(next(ks), shape, F32) * scale

    def gain(shape):
        return 1.0 + nrm(shape, 0.05)

    dt0 = jnp.exp(jax.random.uniform(next(ks), (L, H_B), F32, math.log(1e-3), math.log(1e-1)))
    page_table = jax.random.permutation(next(ks), n_pool)[:DEC_BATCH * n_pages]
    page_table = page_table.reshape(DEC_BATCH, n_pages).astype(jnp.int32)
    return {
        'x_prompt': nrm((BATCH, SEQ, D_MODEL), 1.0),
        'x_sample': nrm((DEC_BATCH, DEC_SEQ, D_MODEL), 1.0),
        'p_prompt': nrm((L, BATCH, SEQ, PLE_DIM), 1.0),
        'p_sample': nrm((L, DEC_BATCH, DEC_SEQ, PLE_DIM), 1.0),
        'cache_kv_latent': nrm((L, n_pool, PAGE_SIZE, KV_LORA), 1.0),
        'cache_k_rope': nrm((L, n_pool, PAGE_SIZE, QK_ROPE), 1.0),
        'page_table': page_table,
        'state_conv': nrm((L, DEC_BATCH, CONV_W - 1, CONV_DIM), 1.0),
        'state_ssm': nrm((L, DEC_BATCH, H_B, SSM_HEAD, D_STATE), 0.5),
        'g_mix_norm': gain((L, D_MODEL)),
        'w_in': nrm((L, D_MODEL, D_IN_PROJ), D_MODEL ** -0.5),
        'g_q_a': gain((L, Q_LORA)),
        'w_q_b': nrm((L, Q_LORA, H_A * QK_HEAD), Q_LORA ** -0.5),
        'g_kv_a': gain((L, KV_LORA)),
        'w_uk': nrm((L, KV_LORA, H_A * QK_NOPE), KV_LORA ** -0.5),
        'w_uv': nrm((L, KV_LORA, H_A * V_HEAD), KV_LORA ** -0.5),
        'g_q_head': gain((L, QK_HEAD)),
        'g_k_head': gain((L, QK_HEAD)),
        'w_a_out': nrm((L, H_A * V_HEAD, D_MODEL), (H_A * V_HEAD) ** -0.5),
        'w_conv': nrm((L, CONV_W, CONV_DIM), CONV_W ** -0.5),
        'b_conv': nrm((L, CONV_DIM), 0.01),
        'dt_bias': dt0 + jnp.log(-jnp.expm1(-dt0)),
        'a_log': jnp.log(jax.random.uniform(next(ks), (L, H_B), F32, 1.0, 16.0)),
        'd_skip': gain((L, H_B)),
        'g_ssm_norm': gain((L, D_INNER)),
        'w_b_out': nrm((L, D_INNER, D_MODEL), D_INNER ** -0.5),
        'w_out': nrm((L, D_MODEL, D_MODEL), D_MODEL ** -0.5),
        'g_ffn_norm': gain((L, D_MODEL)),
        'w_router': nrm((L, D_MODEL, N_EXPERTS), D_MODEL ** -0.5),
        'b_router': nrm((L, N_EXPERTS), 0.01),
        'w_gate_up': nrm((L, N_EXPERTS, D_MODEL, 2 * D_FF), D_MODEL ** -0.5),
        'b_gate_up': nrm((L, N_EXPERTS, 2 * D_FF), 0.01),
        'w_down': nrm((L, N_EXPERTS, D_FF, D_MODEL), D_FF ** -0.5),
        'b_down': nrm((L, N_EXPERTS, D_MODEL), 0.01),
        'g_ple_in': gain((L, D_MODEL)),
        'w_ple_gate': nrm((L, D_MODEL, D_MODEL), D_MODEL ** -0.5),
        'w_ple': nrm((L, PLE_DIM, D_MODEL), PLE_DIM ** -0.5),
        'g_ple': gain((L, D_MODEL)),
    }


def reference(x_prompt, x_sample, p_prompt, p_sample, cache_kv_latent, cache_k_rope, page_table,
              state_conv, state_ssm, g_mix_norm, w_in, g_q_a, w_q_b, g_kv_a, w_uk, w_uv, g_q_head,
              g_k_head, w_a_out, w_conv, b_conv, dt_bias, a_log, d_skip, g_ssm_norm, w_b_out, w_out,
              g_ffn_norm, w_router, b_router, w_gate_up, b_gate_up, w_down, b_down, g_ple_in,
              w_ple_gate, w_ple, g_ple):
    hp, hs = x_prompt, x_sample
    pos_p = jnp.arange(x_prompt.shape[1])
    pos_s = page_table.shape[1] * PAGE_SIZE + jnp.arange(x_sample.shape[1])
    c_p, kr_p, conv_p, ssm_p = [], [], [], []
    c_s, kr_s, conv_s, ssm_s = [], [], [], []
    for i in range(DEPTH):
        lp = {
            'g_mix_norm': g_mix_norm[i], 'w_in': w_in[i], 'g_q_a': g_q_a[i], 'w_q_b': w_q_b[i],
            'g_kv_a': g_kv_a[i], 'w_uk': w_uk[i], 'w_uv': w_uv[i], 'g_q_head': g_q_head[i],
            'g_k_head': g_k_head[i], 'w_a_out': w_a_out[i], 'w_conv': w_conv[i], 'b_conv': b_conv[i],
            'dt_bias': dt_bias[i], 'a_log': a_log[i], 'd_skip': d_skip[i], 'g_ssm_norm': g_ssm_norm[i],
            'w_b_out': w_b_out[i], 'w_out': w_out[i], 'g_ffn_norm': g_ffn_norm[i],
            'w_router': w_router[i], 'b_router': b_router[i], 'w_gate_up': w_gate_up[i],
            'b_gate_up': b_gate_up[i], 'w_down': w_down[i], 'b_down': b_down[i],
            'g_ple_in': g_ple_in[i], 'w_ple_gate': w_ple_gate[i], 'w_ple': w_ple[i], 'g_ple': g_ple[i],
        }
        hp, c, kr, cv, ss = prompt_layer(hp, p_prompt[i], pos_p, lp)
        c_p.append(c); kr_p.append(kr); conv_p.append(cv); ssm_p.append(ss)
        hs, c, kr, cv, ss = sample_layer(hs, p_sample[i], pos_s, cache_kv_latent[i], cache_k_rope[i],
                                         page_table, state_conv[i], state_ssm[i], lp)
        c_s.append(c); kr_s.append(kr); conv_s.append(cv); ssm_s.append(ss)
    return (hp, hs, jnp.stack(c_p), jnp.stack(kr_p), jnp.stack(conv_p), jnp.stack(ssm_p),
            jnp.stack(c_s), jnp.stack(kr_s), jnp.stack(conv_s), jnp.stack(ssm_s))
```

```python
import functools
import math

import jax
import jax.numpy as jnp
import numpy as np
from jax import lax
from jax.experimental import pallas as pl
from jax.experimental.pallas import tpu as pltpu

F32 = jnp.float32
BF16 = jnp.bfloat16

D_MODEL = 1024
H_A = 16
Q_LORA = 384
KV_LORA = 256
QK_NOPE = 64
QK_ROPE = 32
QK_HEAD = QK_NOPE + QK_ROPE
V_HEAD = 64
ROPE_THETA = 10000.0
D_INNER = 2 * D_MODEL
SSM_HEAD = 64
H_B = D_INNER // SSM_HEAD
N_GROUPS = 4
D_STATE = 128
CONV_W = 4
CONV_DIM = D_INNER + 2 * N_GROUPS * D_STATE
SSD_CHUNK = 128
N_EXPERTS = 32
TOP_K = 4
D_FF = D_MODEL
SWIGLU_LIMIT = 7.0
SWIGLU_ALPHA = 1.702
PLE_DIM = 256
EPS = 1e-6

LANE = 128
HEAD_PAD = LANE
GROUP_W = D_INNER // N_GROUPS
HEADS_PER_GROUP = H_B // N_GROUPS
Q_SCALE = QK_HEAD ** -0.5 * math.log2(math.e)
VMEM_LIMIT = 56 * 1024 * 1024

FRONT_TM = 256
ATTN_TQ = 1024
MERGE_TM = 256
MOE_BM = 256
PLE_TM = 128
DEC_PPS = 16
DEC_SUB = 4


def _dot(a, b):
    return jnp.dot(a, b, preferred_element_type=F32)


def _dot_nt(a, b):
    return lax.dot_general(a, b, (((1,), (1,)), ((), ())), preferred_element_type=F32)


def _dot_tn(a, b):
    return lax.dot_general(a, b, (((0,), (0,)), ((), ())), preferred_element_type=F32)


def _split3(x):
    hi = x.astype(BF16)
    r1 = x - hi.astype(F32)
    mid = r1.astype(BF16)
    lo = (r1 - mid.astype(F32)).astype(BF16)
    return hi, mid, lo


def _dot_f32_lhs(x, e):
    hi, mid, lo = _split3(x)
    return _dot(hi, e) + _dot(mid, e) + _dot(lo, e)


def _dot_f32_rhs(e, x):
    hi, mid, lo = _split3(x)
    return _dot(e, hi) + _dot(e, mid) + _dot(e, lo)


def _rms(x, g):
    return x * lax.rsqrt(jnp.mean(x * x, axis=-1, keepdims=True) + EPS) * g


def _silu(x):
    return x * jax.nn.sigmoid(x)


def _cparams(sem, vmem=VMEM_LIMIT):
    return pltpu.CompilerParams(dimension_semantics=sem, vmem_limit_bytes=vmem)


def _whole():
    return pl.BlockSpec(memory_space=pltpu.VMEM)


def _head_norm(xh, g):
    ss = jnp.sum(xh * xh, axis=-1, keepdims=True)
    return xh * lax.rsqrt(ss * (1.0 / QK_HEAD) + EPS) * g


def _front_kernel(x_ref, cos_ref, sin_ref, gmix_ref, wlat_ref, wz_ref, wxbc_ref, wg_ref,
                  gqa_ref, wqb_ref, gkva_ref, wuk_ref, wuv_ref, gq_ref, gk_ref,
                  q_ref, k_ref, v_ref, c_ref, kr_ref, z_ref, xbc_ref, dt_ref, gates_ref):
    x = x_ref[...]
    ub = _rms(x, gmix_ref[...]).astype(BF16)
    z_ref[...] = _dot(ub, wz_ref[...])
    xbc_ref[...] = _dot(ub, wxbc_ref[...])
    gates_ref[...] = _dot(ub, wg_ref[...])
    lat = _dot(ub, wlat_ref[...])
    q_lat = lat[:, :Q_LORA]
    kv_lat = lat[:, Q_LORA:Q_LORA + KV_LORA]
    o = Q_LORA + KV_LORA
    kr_raw = lat[:, o:o + LANE]
    kr_rot = lat[:, o + LANE:o + 2 * LANE]
    dt_ref[...] = lat[:, o + 2 * LANE:o + 3 * LANE]
    cos = cos_ref[...]
    sin = sin_ref[...]
    c = _rms(kv_lat, gkva_ref[...])
    c_ref[...] = c
    kr = kr_raw * cos + kr_rot * sin
    kr_ref[...] = kr
    qn = _rms(q_lat, gqa_ref[...]).astype(BF16)
    q2 = _dot(qn, wqb_ref[...])
    nq = H_A * HEAD_PAD
    gq = gq_ref[...]
    for h in range(H_A):
        lo, hi = h * HEAD_PAD, (h + 1) * HEAD_PAD
        qh = q2[:, lo:hi] * cos + q2[:, nq + lo:nq + hi] * sin
        q_ref[:, lo:hi] = _head_norm(qh, gq).astype(q_ref.dtype)
    cb = c.astype(BF16)
    kn = _dot(cb, wuk_ref[...])
    gk = gk_ref[...]
    for h in range(H_A):
        lo, hi = h * HEAD_PAD, (h + 1) * HEAD_PAD
        k_ref[:, lo:hi] = _head_norm(kn[:, lo:hi] + kr, gk).astype(k_ref.dtype)
    v_ref[...] = _dot(cb, wuv_ref[...]).astype(v_ref.dtype)


def _front(x, cos, sin, fw, *, tm, q_dtype, pos_blocks):
    t = x.shape[0]
    nt = t // tm
    row = lambda w: pl.BlockSpec((tm, w), lambda i: (i, 0))
    pos_spec = pl.BlockSpec((tm, LANE), lambda i: (i % pos_blocks, 0))
    nq = H_A * HEAD_PAD
    out_shape = (
        jax.ShapeDtypeStruct((t, nq), q_dtype),
        jax.ShapeDtypeStruct((t, nq), BF16),
        jax.ShapeDtypeStruct((t, H_A * V_HEAD), BF16),
        jax.ShapeDtypeStruct((t, KV_LORA), F32),
        jax.ShapeDtypeStruct((t, LANE), F32),
        jax.ShapeDtypeStruct((t, D_INNER), F32),
        jax.ShapeDtypeStruct((t, CONV_DIM), F32),
        jax.ShapeDtypeStruct((t, LANE), F32),
        jax.ShapeDtypeStruct((t, 2 * D_MODEL), F32),
    )
    out_specs = (row(nq), row(nq), row(H_A * V_HEAD), row(KV_LORA), row(LANE), row(D_INNER),
                 row(CONV_DIM), row(LANE), row(2 * D_MODEL))
    weights = (fw['g_mix'], fw['w_lat'], fw['w_z'], fw['w_xbc'], fw['w_g'], fw['g_q_a'], fw['w_qb'],
               fw['g_kv_a'], fw['w_uk_pad'], fw['w_uv'], fw['g_q128'], fw['g_k128'])
    return pl.pallas_call(
        _front_kernel,
        grid=(nt,),
        in_specs=[row(D_MODEL), pos_spec, pos_spec] + [_whole()] * len(weights),
        out_specs=out_specs,
        out_shape=out_shape,
        compiler_params=_cparams(("parallel",)),
        name="front",
    )(x, cos, sin, *weights)


def _attn_kernel(qi_ref, ki_ref, q_ref, k_ref, v_ref, o_ref, m_sc, l_sc, acc_sc):
    step = pl.program_id(2)
    qi = qi_ref[step]
    ki = ki_ref[step]
    tq, tk = q_ref.shape[0], k_ref.shape[0]

    @pl.when(ki == 0)
    def _():
        m_sc[...] = jnp.full(m_sc.shape, -jnp.inf, F32)
        l_sc[...] = jnp.zeros(l_sc.shape, F32)
        acc_sc[...] = jnp.zeros(acc_sc.shape, F32)

    def update(masked):
        v = v_ref[...]
        for h in range(2):
            q = q_ref[:, h * HEAD_PAD:(h + 1) * HEAD_PAD]
            k = k_ref[:, h * HEAD_PAD:(h + 1) * HEAD_PAD]
            s = _dot_nt(q, k)
            if masked:
                row = lax.broadcasted_iota(jnp.int32, (tq, tk), 0)
                col = lax.broadcasted_iota(jnp.int32, (tq, tk), 1)
                s = jnp.where(col <= row, s, -jnp.inf)
            m_prev = m_sc[h]
            m_next = jnp.maximum(m_prev, jnp.max(s, axis=-1, keepdims=True))
            alpha = jnp.exp2(m_prev - m_next)
            p = jnp.exp2(s - m_next[:, :1])
            l_sc[h] = alpha * l_sc[h] + jnp.sum(p, axis=-1, keepdims=True)
            acc_sc[h] = alpha * acc_sc[h] + _dot(p.astype(BF16), v)
            m_sc[h] = m_next

    @pl.when(ki < qi)
    def _():
        update(False)

    @pl.when(ki == qi)
    def _():
        update(True)
        lane = lax.broadcasted_iota(jnp.int32, (tq, LANE), 1)
        o0 = acc_sc[0] / l_sc[0]
        o1 = acc_sc[1] / l_sc[1]
        o_ref[...] = jnp.where(lane < V_HEAD, o0, o1).astype(o_ref.dtype)


def _prompt_attention(q, k, v, b, s):
    tq = min(ATTN_TQ, s)
    nq = s // tq
    pairs = [(i, j) for i in range(nq) for j in range(i + 1)]
    qi_tab = jnp.asarray([p[0] for p in pairs], jnp.int32)
    ki_tab = jnp.asarray([p[1] for p in pairs], jnp.int32)
    grid_spec = pltpu.PrefetchScalarGridSpec(
        num_scalar_prefetch=2,
        grid=(b, H_A // 2, len(pairs)),
        in_specs=[
            pl.BlockSpec((None, tq, 2 * HEAD_PAD), lambda bi, h, t, qt, kt: (bi, qt[t], h)),
            pl.BlockSpec((None, tq, 2 * HEAD_PAD), lambda bi, h, t, qt, kt: (bi, kt[t], h)),
            pl.BlockSpec((None, tq, 2 * V_HEAD), lambda bi, h, t, qt, kt: (bi, kt[t], h)),
        ],
        out_specs=pl.BlockSpec((None, tq, 2 * V_HEAD), lambda bi, h, t, qt, kt: (bi, qt[t], h)),
        scratch_shapes=[pltpu.VMEM((2, tq, LANE), F32), pltpu.VMEM((2, tq, LANE), F32),
                        pltpu.VMEM((2, tq, LANE), F32)],
    )
    return pl.pallas_call(
        _attn_kernel,
        grid_spec=grid_spec,
        out_shape=jax.ShapeDtypeStruct((b, s, H_A * V_HEAD), BF16),
        compiler_params=_cparams(("parallel", "parallel", "arbitrary")),
        name="prompt_attn",
    )(qi_tab, ki_tab, q, k, v)


def _ssd_kernel(xbc_ref, z_ref, dt_ref, wconv_ref, bconv_ref, dtb_ref, ah_ref, dskip_ref, gn_ref,
                tri_ref, exp_ref, expt_ref, y_ref, hfin_ref, xbuf, state):
    ci = pl.program_id(1)
    nc = pl.num_programs(1)
    L = SSD_CHUNK

    @pl.when(ci == 0)
    def _():
        xbuf[0:8, :] = jnp.zeros((8, CONV_DIM), F32)
        state[...] = jnp.zeros(state.shape, F32)

    xbuf[8:8 + L, :] = xbc_ref[...]
    conv = bconv_ref[...] + xbuf[8:8 + L, :] * wconv_ref[3:4, :]
    for w in range(CONV_W - 1):
        sh = CONV_W - 1 - w
        conv = conv + xbuf[8 - sh:8 - sh + L, :] * wconv_ref[w:w + 1, :]
    xbuf[0:8, :] = xbuf[L:L + 8, :]
    xc = _silu(conv)
    xs = xc[:, :D_INNER]

    dt = jax.nn.softplus(dt_ref[...] + dtb_ref[...])
    a = dt * ah_ref[...]
    a_cum = _dot_f32_rhs(tri_ref[...], a)
    a_cum_t = a_cum.T
    a_last = a_cum[L - 1:L, :]
    ex = exp_ref[...]
    dt_x = _dot_f32_lhs(dt, ex)
    dfs_x = _dot_f32_lhs(jnp.exp(a_cum), ex)
    dte_x = _dot_f32_lhs(jnp.exp(a_last - a_cum), ex)
    xdt = xs * dt_x
    xdt_b = xdt.astype(BF16)
    xw_b = (xdt * dte_x).astype(BF16)
    last_col = jnp.broadcast_to(jnp.exp(a_cum_t[:, L - 1:L]), (LANE, LANE))
    carry = _dot_f32_rhs(expt_ref[...], last_col)

    row = lax.broadcasted_iota(jnp.int32, (L, L), 0)
    col = lax.broadcasted_iota(jnp.int32, (L, L), 1)
    causal = col <= row
    lane = lax.broadcasted_iota(jnp.int32, (L, LANE), 1)
    for g in range(N_GROUPS):
        bm = xc[:, D_INNER + g * D_STATE:D_INNER + (g + 1) * D_STATE].astype(BF16)
        cm = xc[:, D_INNER + (N_GROUPS + g) * D_STATE:D_INNER + (N_GROUPS + g + 1) * D_STATE].astype(BF16)
        cb = _dot_nt(cm, bm)
        c0, c1 = g * GROUP_W, (g + 1) * GROUP_W
        st_prev = state[c0:c1, :]
        y_off = _dot_nt(cm, st_prev.astype(BF16)) * dfs_x[:, c0:c1]
        for j in range(HEADS_PER_GROUP // 2):
            h0 = g * HEADS_PER_GROUP + 2 * j
            x2 = xdt_b[:, h0 * SSM_HEAD:(h0 + 2) * SSM_HEAD]
            ys = []
            for hh in (h0, h0 + 1):
                seg = a_cum[:, hh:hh + 1] - a_cum_t[hh:hh + 1, :]
                sc = jnp.where(causal, cb * jnp.exp(seg), 0.0)
                ys.append(_dot(sc.astype(BF16), x2))
            y2 = jnp.where(lane < SSM_HEAD, ys[0], ys[1])
            lo = h0 * SSM_HEAD
            y_ref[:, lo:lo + LANE] = y2 + y_off[:, lo - c0:lo - c0 + LANE]
        st_new = _dot_tn(xw_b[:, c0:c1], bm)
        state[c0:c1, :] = st_prev * carry[c0:c1, :] + st_new

    y = y_ref[...] + dskip_ref[...] * xs
    yg = y * _silu(z_ref[...])
    gn = gn_ref[...]
    for g in range(N_GROUPS):
        c0, c1 = g * GROUP_W, (g + 1) * GROUP_W
        blk = yg[:, c0:c1]
        y_ref[:, c0:c1] = blk * lax.rsqrt(jnp.mean(blk * blk, axis=-1, keepdims=True) + EPS) * gn[:, c0:c1]

    @pl.when(ci == nc - 1)
    def _():
        hfin_ref[...] = state[...]


def _ssd_prompt(xbc, z, dt, sw, b, s):
    nc = s // SSD_CHUNK
    row = lambda w: pl.BlockSpec((SSD_CHUNK, w), lambda bi, ci: (bi * nc + ci, 0))
    consts = (sw['w_conv'], sw['b_conv'], sw['dt_bias128'], sw['a_head128'], sw['d_skip_x'], sw['g_norm'],
              sw['tri'], sw['expand'], sw['expand_t'])
    return pl.pallas_call(
        _ssd_kernel,
        grid=(b, nc),
        in_specs=[row(CONV_DIM), row(D_INNER), row(LANE)] + [_whole()] * len(consts),
        out_specs=(row(D_INNER), pl.BlockSpec((None, D_INNER, D_STATE), lambda bi, ci: (bi, 0, 0))),
        out_shape=(jax.ShapeDtypeStruct((b * s, D_INNER), F32),
                   jax.ShapeDtypeStruct((b, D_INNER, D_STATE), F32)),
        scratch_shapes=[pltpu.VMEM((SSD_CHUNK + 8, CONV_DIM), F32), pltpu.VMEM((D_INNER, D_STATE), F32)],
        compiler_params=_cparams(("parallel", "arbitrary")),
        name="ssd_prompt",
    )(xbc, z, dt, *consts)


def _ssm_step_kernel(xbc_ref, cbuf_ref, z_ref, dt_ref, h_ref, wconv_ref, bconv_ref, dtb_ref, ah_ref,
                     dskip_ref, gn_ref, exp_ref, y_ref, hnew_ref):
    conv = bconv_ref[...] + xbc_ref[0] * wconv_ref[3:4, :]
    for w in range(CONV_W - 1):
        conv = conv + cbuf_ref[0, w:w + 1, :] * wconv_ref[w:w + 1, :]
    xc = _silu(conv)
    xs = xc[:, :D_INNER]
    dt = jax.nn.softplus(dt_ref[0] + dtb_ref[...])
    da = jnp.exp(dt * ah_ref[...])
    ex = exp_ref[...]
    da_x = _dot_f32_lhs(jnp.broadcast_to(da, (8, LANE)), ex)[0:1, :]
    xdt = xs * _dot_f32_lhs(jnp.broadcast_to(dt, (8, LANE)), ex)[0:1, :]

    eye = lax.broadcasted_iota(jnp.int32, (LANE, LANE), 0) == lax.broadcasted_iota(jnp.int32, (LANE, LANE), 1)

    def to_col(rowvec):
        return jnp.sum(jnp.where(eye, jnp.broadcast_to(rowvec, (LANE, LANE)), 0.0), axis=-1, keepdims=True)

    ys = []
    for g in range(N_GROUPS):
        bt = xc[:, D_INNER + g * D_STATE:D_INNER + (g + 1) * D_STATE]
        ct = xc[:, D_INNER + (N_GROUPS + g) * D_STATE:D_INNER + (N_GROUPS + g + 1) * D_STATE]
        ct8 = jnp.broadcast_to(ct, (8, D_STATE)).astype(BF16)
        for j in range(GROUP_W // LANE):
            r0 = g * GROUP_W + j * LANE
            h_old = h_ref[0, r0:r0 + LANE, :]
            h_new = h_old * to_col(da_x[:, r0:r0 + LANE]) + to_col(xdt[:, r0:r0 + LANE]) * bt
            hnew_ref[0, r0:r0 + LANE, :] = h_new
            ys.append(_dot_nt(ct8, h_new.astype(BF16))[0:1, :])
    y = jnp.concatenate(ys, axis=-1) + dskip_ref[...] * xs
    yg = y * _silu(z_ref[0])
    gn = gn_ref[...]
    for g in range(N_GROUPS):
        c0, c1 = g * GROUP_W, (g + 1) * GROUP_W
        blk = yg[:, c0:c1]
        y_ref[0, :, c0:c1] = blk * lax.rsqrt(jnp.mean(blk * blk, axis=-1, keepdims=True) + EPS) * gn[:, c0:c1]


def _ssm_sample(xbc, conv_buf, z, dt, h, sw):
    bd = xbc.shape[0]
    vec = lambda w: pl.BlockSpec((1, 1, w), lambda i: (i, 0, 0))
    consts = (sw['w_conv'], sw['b_conv'], sw['dt_bias128'], sw['a_head128'], sw['d_skip_x'], sw['g_norm'],
              sw['expand'])
    y, h_new = pl.pallas_call(
        _ssm_step_kernel,
        grid=(bd,),
        in_specs=[vec(CONV_DIM), pl.BlockSpec((1, CONV_W - 1, CONV_DIM), lambda i: (i, 0, 0)), vec(D_INNER),
                  vec(LANE), pl.BlockSpec((1, D_INNER, D_STATE), lambda i: (i, 0, 0))] + [_whole()] * len(consts),
        out_specs=(vec(D_INNER), pl.BlockSpec((1, D_INNER, D_STATE), lambda i: (i, 0, 0))),
        out_shape=(jax.ShapeDtypeStruct((bd, 1, D_INNER), F32),
                   jax.ShapeDtypeStruct((bd, D_INNER, D_STATE), F32)),
        compiler_params=_cparams(("parallel",)),
        name="ssm_sample",
    )(xbc.reshape(bd, 1, CONV_DIM), conv_buf, z.reshape(bd, 1, D_INNER), dt.reshape(bd, 1, LANE), h, *consts)
    return y.reshape(bd, D_INNER), h_new


def _decode_kernel(pt_ref, qp_ref, qr_ref, gkp_ref, gkr_ref, wuk_ref, segt_ref, ones_ref, *rest):
    c_refs = rest[:DEC_PPS]
    kr_refs = rest[DEC_PPS:2 * DEC_PPS]
    acc_ref, m_ref, l_ref = rest[2 * DEC_PPS:2 * DEC_PPS + 3]
    m_sc, l_sc, acc_sc = rest[2 * DEC_PPS + 3:]
    j = pl.program_id(1)

    @pl.when(j == 0)
    def _():
        m_sc[...] = jnp.full(m_sc.shape, -jnp.inf, F32)
        l_sc[...] = jnp.zeros(l_sc.shape, F32)
        acc_sc[...] = jnp.zeros(acc_sc.shape, F32)

    qg = qp_ref[0] * gkp_ref[...]
    qr = (qr_ref[0] * gkr_ref[...]).astype(BF16)
    nj = (H_A * QK_NOPE) // LANE
    for sub in range(DEC_PPS // DEC_SUB):
        pages = range(sub * DEC_SUB, (sub + 1) * DEC_SUB)
        cb = jnp.concatenate([c_refs[p][...].astype(BF16) for p in pages], axis=0)
        krf = jnp.concatenate([kr_refs[p][...] for p in pages], axis=0)
        kk = _dot(cb, wuk_ref[...])
        p_sq = None
        p_qs = None
        for jj in range(nj):
            blk = kk[:, jj * LANE:(jj + 1) * LANE]
            sq = blk * blk
            qs = blk * qg[:, jj * LANE:(jj + 1) * LANE]
            p_sq = sq if p_sq is None else p_sq + sq
            p_qs = qs if p_qs is None else p_qs + qs
        pcat = jnp.concatenate([p_sq, p_qs], axis=1).astype(BF16)
        ns = _dot_nt(segt_ref[...], pcat)
        ns = ns + _dot_nt(qr, krf.astype(BF16)) + _dot_nt(ones_ref[...], (krf * krf).astype(BF16))
        nsq = ns[0:H_A, :]
        s = ns[H_A:2 * H_A, :] * lax.rsqrt(nsq * (1.0 / QK_HEAD) + EPS)
        m_prev = m_sc[...]
        m_next = jnp.maximum(m_prev, jnp.max(s, axis=-1, keepdims=True))
        alpha = jnp.exp2(m_prev - m_next)
        p = jnp.exp2(s - m_next[:, :1])
        l_sc[...] = alpha * l_sc[...] + jnp.sum(p, axis=-1, keepdims=True)
        acc_sc[...] = acc_sc[...] * alpha[:, :1] + _dot(p.astype(BF16), cb)
        m_sc[...] = m_next

    @pl.when(j == pl.num_programs(1) - 1)
    def _():
        acc_ref[0] = acc_sc[...]
        m_ref[0] = m_sc[...]
        l_ref[0] = l_sc[...]


def _decode_attention(page_table, q_perm, qr_mat, cache_c, cache_kr, dw):
    bd, n_pages = page_table.shape
    nstep = n_pages // DEC_PPS
    PAGE = cache_c.shape[1]

    def c_spec(i):
        return pl.BlockSpec((None, PAGE, KV_LORA), lambda b, j, pt: (pt[b, j * DEC_PPS + i], 0, 0))

    def kr_spec(i):
        return pl.BlockSpec((None, PAGE, QK_ROPE), lambda b, j, pt: (pt[b, j * DEC_PPS + i], 0, 0))

    const = lambda shp: pl.BlockSpec(shp, lambda b, j, pt: (0,) * len(shp))
    in_specs = ([pl.BlockSpec((1, 1, H_A * QK_NOPE), lambda b, j, pt: (b, 0, 0)),
                 pl.BlockSpec((1, 2 * H_A, QK_ROPE), lambda b, j, pt: (b, 0, 0)),
                 const((1, H_A * QK_NOPE)), const((1, QK_ROPE)), const((KV_LORA, H_A * QK_NOPE)),
                 const((2 * H_A, 2 * LANE)),
                 const((2 * H_A, QK_ROPE))]
                + [c_spec(i) for i in range(DEC_PPS)] + [kr_spec(i) for i in range(DEC_PPS)])
    out_b = lambda w: pl.BlockSpec((1, H_A, w), lambda b, j, pt: (b, 0, 0))
    grid_spec = pltpu.PrefetchScalarGridSpec(
        num_scalar_prefetch=1,
        grid=(bd, nstep),
        in_specs=in_specs,
        out_specs=(out_b(KV_LORA), out_b(LANE), out_b(LANE)),
        scratch_shapes=[pltpu.VMEM((H_A, LANE), F32), pltpu.VMEM((H_A, LANE), F32),
                        pltpu.VMEM((H_A, KV_LORA), F32)],
    )
    return pl.pallas_call(
        _decode_kernel,
        grid_spec=grid_spec,
        out_shape=(jax.ShapeDtypeStruct((bd, H_A, KV_LORA), F32),
                   jax.ShapeDtypeStruct((bd, H_A, LANE), F32),
                   jax.ShapeDtypeStruct((bd, H_A, LANE), F32)),
        compiler_params=_cparams(("parallel", "arbitrary")),
        name="decode_attn",
    )(page_table, q_perm.reshape(bd, 1, -1), qr_mat, dw['gk_perm'], dw['gk_rope'], dw['w_uk_perm'], dw['seg_t'],
      dw['ones_rows'],
      *([cache_c] * DEC_PPS), *([cache_kr] * DEC_PPS))


def _decode_final_kernel(q_ref, k_ref, c_ref, m_ref, l_ref, acc_ref, wuv_ref, o_ref):
    c_new = c_ref[...]
    for pair in range(H_A // 2):
        o_pair = None
        for h in (2 * pair, 2 * pair + 1):
            lo, hi = h * HEAD_PAD, (h + 1) * HEAD_PAD
            s_new = jnp.sum(q_ref[:, lo:hi] * k_ref[:, lo:hi].astype(F32), axis=-1, keepdims=True)
            m_old = m_ref[:, h:h + 1]
            m_new = jnp.maximum(m_old, s_new)
            a = jnp.exp2(m_old - m_new)
            pn = jnp.exp2(s_new - m_new)
            l_new = l_ref[:, h:h + 1] * a + pn
            ctx = (acc_ref[:, h * KV_LORA:(h + 1) * KV_LORA] * a + pn * c_new) / l_new
            part = _dot(ctx.astype(BF16), wuv_ref[h])
            o_pair = part if o_pair is None else o_pair + part
        o_ref[:, pair * LANE:(pair + 1) * LANE] = o_pair.astype(o_ref.dtype)


def _decode_final(q, k_new, c_new, m, l, acc, dw):
    bd = q.shape[0]
    return pl.pallas_call(
        _decode_final_kernel,
        in_specs=[_whole()] * 7,
        out_specs=_whole(),
        out_shape=jax.ShapeDtypeStruct((bd, H_A * V_HEAD), BF16),
        compiler_params=pltpu.CompilerParams(vmem_limit_bytes=VMEM_LIMIT),
        name="decode_final",
    )(q, k_new, c_new, m, l, acc, dw['w_uv_pair'])


def _merge_kernel(x_ref, attn_ref, ssm_ref, gates_ref, wa_ref, wb_ref, wo_ref, gffn_ref, wr_ref, br_ref,
                  h_ref, hn_ref, logit_ref):
    g = jax.nn.sigmoid(gates_ref[...])
    a = _dot(attn_ref[...], wa_ref[...])
    b = _dot(ssm_ref[...].astype(BF16), wb_ref[...])
    mixed = g[:, :D_MODEL] * a + g[:, D_MODEL:] * b
    h = x_ref[...] + _dot(mixed.astype(BF16), wo_ref[...])
    h_ref[...] = h
    hn = _rms(h, gffn_ref[...])
    hn_ref[...] = hn
    hi, mid, _ = _split3(hn)
    w_hi = wr_ref[0]
    w_lo = wr_ref[1]
    logit_ref[...] = _dot(hi, w_hi) + _dot(mid, w_hi) + _dot(hi, w_lo) + br_ref[...]


def _merge(x, attn, ssm, gates, mw, *, tm):
    t = x.shape[0]
    row = lambda w: pl.BlockSpec((tm, w), lambda i: (i, 0))
    consts = (mw['w_a_out'], mw['w_b_out'], mw['w_out'], mw['g_ffn'], mw['w_router2'], mw['b_router128'])
    return pl.pallas_call(
        _merge_kernel,
        grid=(t // tm,),
        in_specs=[row(D_MODEL), row(H_A * V_HEAD), row(D_INNER), row(2 * D_MODEL)] + [_whole()] * len(consts),
        out_specs=(row(D_MODEL), row(D_MODEL), row(LANE)),
        out_shape=(jax.ShapeDtypeStruct((t, D_MODEL), F32), jax.ShapeDtypeStruct((t, D_MODEL), F32),
                   jax.ShapeDtypeStruct((t, LANE), F32)),
        compiler_params=_cparams(("parallel",)),
        name="merge",
    )(x, attn, ssm, gates, *consts)


def _moe_kernel(be_ref, bv_ref, tok_ref, x_hbm, wgu_ref, bgu_ref, wd_ref, bd_ref, out_ref,
                xbuf, wgu_b, wd_b, sem):
    i = pl.program_id(0)
    n = pl.num_programs(0)
    bm = out_ref.shape[0]

    def row_copy(tok, slot, r):
        return pltpu.make_async_copy(x_hbm.at[pl.ds(tok, 1), :], xbuf.at[slot, pl.ds(r, 1), :], sem.at[slot])

    def issue(blk, slot):
        def body(r, carry):
            row_copy(tok_ref[blk * bm + r], slot, r).start()
            return carry
        lax.fori_loop(0, bm, body, 0, unroll=8)

    def wait(slot):
        def body(r, carry):
            row_copy(0, slot, r).wait()
            return carry
        lax.fori_loop(0, bm, body, 0, unroll=8)

    @pl.when((i == 0) & (bv_ref[0] > 0))
    def _():
        issue(0, 0)

    nxt = jnp.minimum(i + 1, n - 1)

    @pl.when((i + 1 < n) & (bv_ref[nxt] > 0))
    def _():
        issue(i + 1, (i + 1) % 2)

    prev = jnp.maximum(i - 1, 0)

    @pl.when((i == 0) | (be_ref[i] != be_ref[prev]))
    def _():
        wgu_b[...] = wgu_ref[0].astype(BF16)
        wd_b[...] = wd_ref[0].astype(BF16)

    @pl.when(bv_ref[i] > 0)
    def _():
        slot = i % 2
        wait(slot)
        x = xbuf[slot].astype(BF16)
        gu = _dot(x, wgu_b[...]) + bgu_ref[0]
        gate = jnp.minimum(gu[:, :D_FF], SWIGLU_LIMIT)
        up = jnp.clip(gu[:, D_FF:], -SWIGLU_LIMIT, SWIGLU_LIMIT)
        act = (up + 1.0) * gate * jax.nn.sigmoid(SWIGLU_ALPHA * gate)
        out_ref[...] = _dot(act.astype(BF16), wd_b[...]) + bd_ref[0]

    @pl.when(bv_ref[i] == 0)
    def _():
        out_ref[...] = jnp.zeros(out_ref.shape, F32)


def _moe_experts(block_e, block_valid, row_tok, x, w_gate_up, b_gate_up, w_down, b_down):
    n_blocks = block_e.shape[0]
    bm = MOE_BM
    grid_spec = pltpu.PrefetchScalarGridSpec(
        num_scalar_prefetch=3,
        grid=(n_blocks,),
        in_specs=[
            pl.BlockSpec(memory_space=pl.ANY),
            pl.BlockSpec((1, D_MODEL, 2 * D_FF), lambda i, be, bv, tk: (be[i], 0, 0)),
            pl.BlockSpec((1, 1, 2 * D_FF), lambda i, be, bv, tk: (be[i], 0, 0)),
            pl.BlockSpec((1, D_FF, D_MODEL), lambda i, be, bv, tk: (be[i], 0, 0)),
            pl.BlockSpec((1, 1, D_MODEL), lambda i, be, bv, tk: (be[i], 0, 0)),
        ],
        out_specs=pl.BlockSpec((bm, D_MODEL), lambda i, be, bv, tk: (i, 0)),
        scratch_shapes=[pltpu.VMEM((2, bm, D_MODEL), F32), pltpu.VMEM((D_MODEL, 2 * D_FF), BF16),
                        pltpu.VMEM((D_FF, D_MODEL), BF16), pltpu.SemaphoreType.DMA((2,))],
    )
    return pl.pallas_call(
        _moe_kernel,
        grid_spec=grid_spec,
        out_shape=jax.ShapeDtypeStruct((n_blocks * bm, D_MODEL), F32),
        compiler_params=_cparams(("arbitrary",)),
        name="moe_experts",
    )(block_e, block_valid, row_tok, x, w_gate_up, b_gate_up.reshape(N_EXPERTS, 1, 2 * D_FF), w_down,
      b_down.reshape(N_EXPERTS, 1, D_MODEL))


def _ple_kernel(pos_ref, y_hbm, h_ref, gw_ref, p_ref, gin_ref, wgate_ref, wple_ref, gple_ref, o_ref,
                gbuf, sem):
    i = pl.program_id(0)
    n = pl.num_programs(0)
    tm = o_ref.shape[0]

    def row_copy(src, slot, k, r):
        return pltpu.make_async_copy(y_hbm.at[pl.ds(src, 1), :], gbuf.at[slot, k, pl.ds(r, 1), :], sem.at[slot])

    def issue(blk, slot):
        def body(r, carry):
            for k in range(TOP_K):
                row_copy(pos_ref[(blk * tm + r) * TOP_K + k], slot, k, r).start()
            return carry
        lax.fori_loop(0, tm, body, 0, unroll=4)

    def wait(slot):
        def body(r, carry):
            for k in range(TOP_K):
                row_copy(0, slot, k, r).wait()
            return carry
        lax.fori_loop(0, tm, body, 0, unroll=4)

    @pl.when(i == 0)
    def _():
        issue(0, 0)

    @pl.when(i + 1 < n)
    def _():
        issue(i + 1, (i + 1) % 2)

    slot = i % 2
    wait(slot)
    gw = gw_ref[...]
    y = gbuf[slot, 0] * gw[:, 0:1]
    for k in range(1, TOP_K):
        y = y + gbuf[slot, k] * gw[:, k:k + 1]
    h2 = h_ref[...] + y
    gate = jax.nn.sigmoid(_dot(_rms(h2, gin_ref[...]).astype(BF16), wgate_ref[...]))
    ple = _rms(_dot(p_ref[...].astype(BF16), wple_ref[...]), gple_ref[...]) * gate
    o_ref[...] = h2 + ple


def _combine_ple(pos, y_sorted, h, gate_w, p, pw):
    t = h.shape[0]
    tm = PLE_TM
    row = lambda w: pl.BlockSpec((tm, w), lambda i, ps: (i, 0))
    whole = pl.BlockSpec(memory_space=pltpu.VMEM)
    grid_spec = pltpu.PrefetchScalarGridSpec(
        num_scalar_prefetch=1,
        grid=(t // tm,),
        in_specs=[pl.BlockSpec(memory_space=pl.ANY), row(D_MODEL), row(LANE), row(PLE_DIM),
                  whole, whole, whole, whole],
        out_specs=row(D_MODEL),
        scratch_shapes=[pltpu.VMEM((2, TOP_K, tm, D_MODEL), F32), pltpu.SemaphoreType.DMA((2,))],
    )
    return pl.pallas_call(
        _ple_kernel,
        grid_spec=grid_spec,
        out_shape=jax.ShapeDtypeStruct((t, D_MODEL), F32),
        compiler_params=_cparams(("arbitrary",)),
        name="combine_ple",
    )(pos, y_sorted, h, gate_w, p, pw['g_ple_in'], pw['w_ple_gate'], pw['w_ple'], pw['g_ple'])


def _pad_lanes(x, n):
    return jnp.pad(x, [(0, 0)] * (x.ndim - 1) + [(0, n - x.shape[-1])])


def _rot_cols(w):
    half = QK_ROPE // 2
    return jnp.concatenate([-w[..., half:], w[..., :half]], axis=-1)


def _prep_weights(g_mix_norm, w_in, g_q_a, w_q_b, g_kv_a, w_uk, w_uv, g_q_head, g_k_head, w_a_out, w_conv,
                  b_conv, dt_bias, a_log, d_skip, g_ssm_norm, w_b_out, w_out, g_ffn_norm, w_router, b_router,
                  g_ple_in, w_ple_gate, w_ple, g_ple):
    sizes = (Q_LORA, KV_LORA, QK_ROPE, D_INNER, CONV_DIM, H_B, 2 * D_MODEL)
    offs = np.concatenate([[0], np.cumsum(sizes)])
    wq, wkv, wkr, wz, wxbc, wdt, wg = [w_in[:, int(offs[i]):int(offs[i + 1])] for i in range(7)]
    zc = lambda n: jnp.zeros((D_MODEL, n), F32)
    kr128 = jnp.concatenate([zc(QK_NOPE), wkr, zc(LANE - QK_HEAD)], axis=1)
    krrot = jnp.concatenate([zc(QK_NOPE), _rot_cols(wkr), zc(LANE - QK_HEAD)], axis=1)
    w_lat = jnp.concatenate([wq, wkv, kr128, krrot, _pad_lanes(wdt, LANE)], axis=1).astype(BF16)

    wqb = w_q_b.reshape(Q_LORA, H_A, QK_HEAD)
    nope, rope_w = wqb[..., :QK_NOPE], wqb[..., QK_NOPE:]
    z_nope = jnp.zeros_like(nope)
    q128 = _pad_lanes(jnp.concatenate([nope, rope_w], axis=-1), LANE).reshape(Q_LORA, H_A * LANE)
    qrot = _pad_lanes(jnp.concatenate([z_nope, _rot_cols(rope_w)], axis=-1), LANE).reshape(Q_LORA, H_A * LANE)
    w_qb = jnp.concatenate([q128, qrot], axis=1).astype(BF16)

    wuk3 = w_uk.reshape(KV_LORA, H_A, QK_NOPE)
    w_uk_pad = _pad_lanes(wuk3, LANE).reshape(KV_LORA, H_A * LANE).astype(BF16)
    sub = LANE // H_A
    w_uk_perm = wuk3.reshape(KV_LORA, H_A, QK_NOPE // sub, sub).transpose(0, 2, 1, 3)
    w_uk_perm = w_uk_perm.reshape(KV_LORA, H_A * QK_NOPE).astype(BF16)
    gk_perm = jnp.broadcast_to(g_k_head[:QK_NOPE].reshape(1, QK_NOPE // sub, 1, sub),
                               (1, QK_NOPE // sub, H_A, sub)).reshape(1, H_A * QK_NOPE)

    front = dict(
        g_mix=g_mix_norm.reshape(1, -1), w_lat=w_lat, w_z=wz.astype(BF16), w_xbc=wxbc.astype(BF16),
        w_g=wg.astype(BF16), g_q_a=g_q_a.reshape(1, -1), w_qb=w_qb, g_kv_a=g_kv_a.reshape(1, -1),
        w_uk_pad=w_uk_pad, w_uv=w_uv.astype(BF16),
        g_q128=_pad_lanes(g_q_head.reshape(1, -1), LANE) * Q_SCALE,
        g_k128=_pad_lanes(g_k_head.reshape(1, -1), LANE),
    )

    lane_head = np.arange(LANE) // sub
    seg = (lane_head[None, :] == np.arange(H_A)[:, None]).astype(np.float32)
    seg_t = np.zeros((2 * H_A, 2 * LANE), np.float32)
    seg_t[:H_A, :LANE] = seg
    seg_t[H_A:, LANE:] = seg
    ones_rows = np.zeros((2 * H_A, QK_ROPE), np.float32)
    ones_rows[:H_A] = 1.0
    wuv3 = w_uv.reshape(KV_LORA, H_A, V_HEAD)
    even = jnp.arange(H_A) % 2 == 0
    w_uv_pair = jnp.where(even[None, :, None],
                          jnp.concatenate([wuv3, jnp.zeros_like(wuv3)], axis=-1),
                          jnp.concatenate([jnp.zeros_like(wuv3), wuv3], axis=-1))
    decode = dict(
        gk_perm=gk_perm, gk_rope=g_k_head[QK_NOPE:].reshape(1, QK_ROPE), w_uk_perm=w_uk_perm, seg_t=jnp.asarray(seg_t, BF16), ones_rows=jnp.asarray(ones_rows, BF16),
        w_uv_pair=w_uv_pair.transpose(1, 0, 2).astype(BF16),
    )

    expand = (np.arange(D_INNER)[None, :] // SSM_HEAD == np.arange(LANE)[:, None]).astype(np.float32)
    tri = (np.arange(SSD_CHUNK)[None, :] <= np.arange(SSD_CHUNK)[:, None]).astype(np.float32)
    ssm = dict(
        w_conv=w_conv, b_conv=b_conv.reshape(1, -1), dt_bias128=_pad_lanes(dt_bias.reshape(1, -1), LANE),
        a_head128=_pad_lanes(-jnp.exp(a_log).reshape(1, -1), LANE),
        d_skip_x=jnp.repeat(d_skip, SSM_HEAD).reshape(1, -1), g_norm=g_ssm_norm.reshape(1, -1),
        tri=jnp.asarray(tri, BF16), expand=jnp.asarray(expand, BF16), expand_t=jnp.asarray(expand.T, BF16),
    )

    wr = _pad_lanes(w_router, LANE)
    wr_hi = wr.astype(BF16)
    wr_lo = (wr - wr_hi.astype(F32)).astype(BF16)
    merge = dict(
        w_a_out=w_a_out.astype(BF16), w_b_out=w_b_out.astype(BF16), w_out=w_out.astype(BF16),
        g_ffn=g_ffn_norm.reshape(1, -1), w_router2=jnp.stack([wr_hi, wr_lo]),
        b_router128=_pad_lanes(b_router.reshape(1, -1), LANE),
    )
    ple = dict(g_ple_in=g_ple_in.reshape(1, -1), w_ple_gate=w_ple_gate.astype(BF16), w_ple=w_ple.astype(BF16),
               g_ple=g_ple.reshape(1, -1))
    return front, decode, ssm, merge, ple


def _rope_tables(pos):
    half = QK_ROPE // 2
    inv = ROPE_THETA ** (-jnp.arange(half, dtype=F32) / half)
    ang = pos.astype(F32)[:, None] * inv[None, :]
    cos, sin = jnp.cos(ang), jnp.sin(ang)
    n = pos.shape[0]
    cos128 = jnp.concatenate([jnp.ones((n, QK_NOPE), F32), cos, cos, jnp.ones((n, LANE - QK_HEAD), F32)], axis=1)
    sin128 = jnp.concatenate([jnp.zeros((n, QK_NOPE), F32), sin, sin, jnp.zeros((n, LANE - QK_HEAD), F32)], axis=1)
    return cos128, sin128


def _route(logits, n_tok):
    top_val, top_idx = lax.top_k(logits[:, :N_EXPERTS], TOP_K)
    gate_w = jax.nn.softmax(top_val, axis=-1)
    n_assign = n_tok * TOP_K
    flat_e = top_idx.reshape(-1)
    onehot = (flat_e[:, None] == jnp.arange(N_EXPERTS)[None, :]).astype(jnp.int32)
    csum = jnp.cumsum(onehot, axis=0)
    rank = jnp.take_along_axis(csum, flat_e[:, None], axis=1)[:, 0] - 1
    counts = csum[-1]
    bm = MOE_BM
    padded = (counts + bm - 1) // bm * bm
    pend = jnp.cumsum(padded)
    pstarts = pend - padded
    dest = (pstarts[flat_e] + rank).astype(jnp.int32)
    n_blocks = -(-n_assign // bm) + N_EXPERTS
    rows = n_blocks * bm
    row_tok = jnp.zeros((rows,), jnp.int32).at[dest].set(jnp.arange(n_assign, dtype=jnp.int32) // TOP_K)
    blk_start = jnp.arange(n_blocks, dtype=jnp.int32) * bm
    block_e = jnp.minimum(jnp.searchsorted(pend, blk_start, side='right'), N_EXPERTS - 1).astype(jnp.int32)
    block_valid = (blk_start < pend[-1]).astype(jnp.int32)
    return gate_w, dest, row_tok, block_e, block_valid


def kernel(x_prompt, x_sample, p_prompt, p_sample, cache_kv_latent, cache_k_rope, page_table, state_conv, state_ssm, g_mix_norm, w_in, g_q_a, w_q_b, g_kv_a, w_uk, w_uv, g_q_head, g_k_head, w_a_out, w_conv, b_conv, dt_bias, a_log, d_skip, g_ssm_norm, w_b_out, w_out, g_ffn_norm, w_router, b_router, w_gate_up, b_gate_up, w_down, b_down, g_ple_in, w_ple_gate, w_ple, g_ple):
    depth = g_mix_norm.shape[0]
    assert depth == 1, "one layer"
    b, s, _ = x_prompt.shape
    bd, sd, _ = x_sample.shape
    assert sd == 1, "one new token per sample sequence"
    n_pages = page_table.shape[1]
    page_size = cache_kv_latent.shape[2]
    assert n_pages % DEC_PPS == 0 and s % SSD_CHUNK == 0
    tp = b * s

    fw, dw, sw, mw, pw = _prep_weights(
        g_mix_norm[0], w_in[0], g_q_a[0], w_q_b[0], g_kv_a[0], w_uk[0], w_uv[0], g_q_head[0], g_k_head[0],
        w_a_out[0], w_conv[0], b_conv[0], dt_bias[0], a_log[0], d_skip[0], g_ssm_norm[0], w_b_out[0], w_out[0],
        g_ffn_norm[0], w_router[0], b_router[0], g_ple_in[0], w_ple_gate[0], w_ple[0], g_ple[0])

    xp = x_prompt.reshape(tp, D_MODEL)
    cos_p, sin_p = _rope_tables(jnp.arange(s))
    tm_p = min(FRONT_TM, s)
    q_p, k_p, v_p, c_p, kr_p, z_p, xbc_p, dt_p, gates_p = _front(
        xp, cos_p, sin_p, fw, tm=tm_p, q_dtype=BF16, pos_blocks=s // tm_p)
    attn_p = _prompt_attention(q_p.reshape(b, s, -1), k_p.reshape(b, s, -1), v_p.reshape(b, s, -1), b, s)
    ssm_p, hfin_p = _ssd_prompt(xbc_p, z_p, dt_p, sw, b, s)
    h_p, hn_p, logit_p = _merge(xp, attn_p.reshape(tp, -1), ssm_p, gates_p, mw, tm=min(MERGE_TM, tp))

    xs = x_sample.reshape(bd, D_MODEL)
    cos_s, sin_s = _rope_tables(jnp.full((bd,), n_pages * page_size, jnp.int32))
    q_s, k_s, _, c_s, kr_s, z_s, xbc_s, dt_s, gates_s = _front(
        xs, cos_s, sin_s, fw, tm=bd, q_dtype=F32, pos_blocks=1)
    q3 = q_s.reshape(bd, H_A, LANE)
    sub = LANE // H_A
    q_perm = q3[:, :, :QK_NOPE].reshape(bd, H_A, QK_NOPE // sub, sub).transpose(0, 2, 1, 3).reshape(bd, -1)
    qr = q3[:, :, QK_NOPE:QK_HEAD]
    qr_mat = jnp.concatenate([jnp.zeros_like(qr), qr], axis=1)
    acc, m_run, l_run = _decode_attention(page_table, q_perm, qr_mat, cache_kv_latent[0], cache_k_rope[0], dw)
    attn_s = _decode_final(q_s, k_s, c_s, m_run[:, :, 0], l_run[:, :, 0], acc.reshape(bd, -1), dw)
    ssm_s, hnew_s = _ssm_sample(xbc_s, state_conv[0], z_s, dt_s,
                                state_ssm[0].reshape(bd, D_INNER, D_STATE), sw)
    h_s, hn_s, logit_s = _merge(xs, attn_s, ssm_s, gates_s, mw, tm=bd)

    h_all = jnp.concatenate([h_p, h_s], axis=0)
    hn_all = jnp.concatenate([hn_p, hn_s], axis=0)
    logits = jnp.concatenate([logit_p, logit_s], axis=0)
    p_all = jnp.concatenate([p_prompt[0].reshape(tp, PLE_DIM), p_sample[0].reshape(bd, PLE_DIM)], axis=0)
    n_tok = tp + bd
    gate_w, dest, row_tok, block_e, block_valid = _route(logits, n_tok)
    y_sorted = _moe_experts(block_e, block_valid, row_tok, hn_all, w_gate_up[0], b_gate_up[0], w_down[0],
                            b_down[0])
    out_all = _combine_ple(dest, y_sorted, h_all, _pad_lanes(gate_w, LANE), p_all, pw)

    y_prompt = out_all[:tp].reshape(b, s, D_MODEL)
    y_sample = out_all[tp:].reshape(bd, sd, D_MODEL)
    new_c_p = c_p.reshape(1, b, s, KV_LORA)
    new_kr_p = kr_p[:, QK_NOPE:QK_HEAD].reshape(1, b, s, QK_ROPE)
    conv_p = xbc_p.reshape(b, s, CONV_DIM)[:, s - (CONV_W - 1):].reshape(1, b, CONV_W - 1, CONV_DIM)
    ssm_state_p = hfin_p.reshape(1, b, H_B, SSM_HEAD, D_STATE)
    new_c_s = c_s.reshape(1, bd, sd, KV_LORA)
    new_kr_s = kr_s[:, QK_NOPE:QK_HEAD].reshape(1, bd, sd, QK_ROPE)
    conv_s = jnp.concatenate([state_conv[0][:, 1:], xbc_s[:, None, :]], axis=1).reshape(1, bd, CONV_W - 1, CONV_DIM)
    ssm_state_s = hnew_s.reshape(1, bd, H_B, SSM_HEAD, D_STATE)
    return (y_prompt, y_sample, new_c_p, new_kr_p, conv_p, ssm_state_p, new_c_s, new_kr_s, conv_s, ssm_state_s)
```

```python
import functools
import math

import jax
import jax.numpy as jnp
import numpy as np
from jax import lax
from jax.experimental import pallas as pl
from jax.experimental.pallas import tpu as pltpu

F32 = jnp.float32
BF16 = jnp.bfloat16

D_MODEL = 1024
H_A = 16
Q_LORA = 384
KV_LORA = 256
QK_NOPE = 64
QK_ROPE = 32
QK_HEAD = QK_NOPE + QK_ROPE
V_HEAD = 64
ROPE_THETA = 10000.0
D_INNER = 2 * D_MODEL
SSM_HEAD = 64
H_B = D_INNER // SSM_HEAD
N_GROUPS = 4
D_STATE = 128
CONV_W = 4
CONV_DIM = D_INNER + 2 * N_GROUPS * D_STATE
SSD_CHUNK = 128
N_EXPERTS = 32
TOP_K = 4
D_FF = D_MODEL
SWIGLU_LIMIT = 7.0
SWIGLU_ALPHA = 1.702
PLE_DIM = 256
EPS = 1e-6

LANE = 128
HEAD_PAD = LANE
GROUP_W = D_INNER // N_GROUPS
HEADS_PER_GROUP = H_B // N_GROUPS
Q_SCALE = QK_HEAD ** -0.5 * math.log2(math.e)
VMEM_LIMIT = 56 * 1024 * 1024

FRONT_TM = 256
ATTN_TQ = 1024
MERGE_TM = 256
MOE_BM = 256
PLE_TM = 128
DEC_PPS = 16
DEC_SUB = 4


def _dot(a, b):
    return jnp.dot(a, b, preferred_element_type=F32)


def _dot_nt(a, b):
    return lax.dot_general(a, b, (((1,), (1,)), ((), ())), preferred_element_type=F32)


def _dot_tn(a, b):
    return lax.dot_general(a, b, (((0,), (0,)), ((), ())), preferred_element_type=F32)


def _split3(x):
    hi = x.astype(BF16)
    r1 = x - hi.astype(F32)
    mid = r1.astype(BF16)
    lo = (r1 - mid.astype(F32)).astype(BF16)
    return hi, mid, lo


def _dot_f32_lhs(x, e):
    hi, mid, lo = _split3(x)
    return _dot(hi, e) + _dot(mid, e) + _dot(lo, e)


def _dot_f32_rhs(e, x):
    hi, mid, lo = _split3(x)
    return _dot(e, hi) + _dot(e, mid) + _dot(e, lo)


def _rms(x, g):
    return x * lax.rsqrt(jnp.mean(x * x, axis=-1, keepdims=True) + EPS) * g


def _silu(x):
    return x * jax.nn.sigmoid(x)


def _cparams(sem, vmem=VMEM_LIMIT):
    return pltpu.CompilerParams(dimension_semantics=sem, vmem_limit_bytes=vmem)


def _whole():
    return pl.BlockSpec(memory_space=pltpu.VMEM)


def _head_norm(xh, g):
    ss = jnp.sum(xh * xh, axis=-1, keepdims=True)
    return xh * lax.rsqrt(ss * (1.0 / QK_HEAD) + EPS) * g


def _front_kernel(x_ref, cos_ref, sin_ref, gmix_ref, wlat_ref, wz_ref, wxbc_ref, wg_ref,
                  gqa_ref, wqb_ref, gkva_ref, wuk_ref, wuv_ref, gq_ref, gk_ref,
                  q_ref, k_ref, v_ref, c_ref, kr_ref, z_ref, xbc_ref, dt_ref, gates_ref):
    x = x_ref[...]
    ub = _rms(x, gmix_ref[...]).astype(BF16)
    z_ref[...] = _dot(ub, wz_ref[...])
    xbc_ref[...] = _dot(ub, wxbc_ref[...])
    gates_ref[...] = _dot(ub, wg_ref[...])
    lat = _dot(ub, wlat_ref[...])
    q_lat = lat[:, :Q_LORA]
    kv_lat = lat[:, Q_LORA:Q_LORA + KV_LORA]
    o = Q_LORA + KV_LORA
    kr_raw = lat[:, o:o + LANE]
    kr_rot = lat[:, o + LANE:o + 2 * LANE]
    dt_ref[...] = lat[:, o + 2 * LANE:o + 3 * LANE]
    cos = cos_ref[...]
    sin = sin_ref[...]
    c = _rms(kv_lat, gkva_ref[...])
    c_ref[...] = c
    kr = kr_raw * cos + kr_rot * sin
    kr_ref[...] = kr
    qn = _rms(q_lat, gqa_ref[...]).astype(BF16)
    q2 = _dot(qn, wqb_ref[...])
    nq = H_A * HEAD_PAD
    gq = gq_ref[...]
    for h in range(H_A):
        lo, hi = h * HEAD_PAD, (h + 1) * HEAD_PAD
        qh = q2[:, lo:hi] * cos + q2[:, nq + lo:nq + hi] * sin
        q_ref[:, lo:hi] = _head_norm(qh, gq).astype(q_ref.dtype)
    cb = c.astype(BF16)
    kn = _dot(cb, wuk_ref[...])
    gk = gk_ref[...]
    for h in range(H_A):
        lo, hi = h * HEAD_PAD, (h + 1) * HEAD_PAD
        k_ref[:, lo:hi] = _head_norm(kn[:, lo:hi] + kr, gk).astype(k_ref.dtype)
    v_ref[...] = _dot(cb, wuv_ref[...]).astype(v_ref.dtype)


def _front(x, cos, sin, fw, *, tm, q_dtype, pos_blocks):
    t = x.shape[0]
    nt = t // tm
    row = lambda w: pl.BlockSpec((tm, w), lambda i: (i, 0))
    pos_spec = pl.BlockSpec((tm, LANE), lambda i: (i % pos_blocks, 0))
    nq = H_A * HEAD_PAD
    out_shape = (
        jax.ShapeDtypeStruct((t, nq), q_dtype),
        jax.ShapeDtypeStruct((t, nq), BF16),
        jax.ShapeDtypeStruct((t, H_A * V_HEAD), BF16),
        jax.ShapeDtypeStruct((t, KV_LORA), F32),
        jax.ShapeDtypeStruct((t, LANE), F32),
        jax.ShapeDtypeStruct((t, D_INNER), F32),
        jax.ShapeDtypeStruct((t, CONV_DIM), F32),
        jax.ShapeDtypeStruct((t, LANE), F32),
        jax.ShapeDtypeStruct((t, 2 * D_MODEL), F32),
    )
    out_specs = (row(nq), row(nq), row(H_A * V_HEAD), row(KV_LORA), row(LANE), row(D_INNER),
                 row(CONV_DIM), row(LANE), row(2 * D_MODEL))
    weights = (fw['g_mix'], fw['w_lat'], fw['w_z'], fw['w_xbc'], fw['w_g'], fw['g_q_a'], fw['w_qb'],
               fw['g_kv_a'], fw['w_uk_pad'], fw['w_uv'], fw['g_q128'], fw['g_k128'])
    return pl.pallas_call(
        _front_kernel,
        grid=(nt,),
        in_specs=[row(D_MODEL), pos_spec, pos_spec] + [_whole()] * len(weights),
        out_specs=out_specs,
        out_shape=out_shape,
        compiler_params=_cparams(("parallel",)),
        name="front",
    )(x, cos, sin, *weights)


def _attn_kernel(qi_ref, ki_ref, q_ref, k_ref, v_ref, o_ref, m_sc, l_sc, acc_sc):
    step = pl.program_id(2)
    qi = qi_ref[step]
    ki = ki_ref[step]
    tq, tk = q_ref.shape[0], k_ref.shape[0]

    @pl.when(ki == 0)
    def _():
        m_sc[...] = jnp.full(m_sc.shape, -jnp.inf, F32)
        l_sc[...] = jnp.zeros(l_sc.shape, F32)
        acc_sc[...] = jnp.zeros(acc_sc.shape, F32)

    def update(masked):
        v = v_ref[...]
        for h in range(2):
            q = q_ref[:, h * HEAD_PAD:(h + 1) * HEAD_PAD]
            k = k_ref[:, h * HEAD_PAD:(h + 1) * HEAD_PAD]
            s = _dot_nt(q, k)
            if masked:
                row = lax.broadcasted_iota(jnp.int32, (tq, tk), 0)
                col = lax.broadcasted_iota(jnp.int32, (tq, tk), 1)
                s = jnp.where(col <= row, s, -jnp.inf)
            m_prev = m_sc[h]
            m_next = jnp.maximum(m_prev, jnp.max(s, axis=-1, keepdims=True))
            alpha = jnp.exp2(m_prev - m_next)
            p = jnp.exp2(s - m_next[:, :1])
            l_sc[h] = alpha * l_sc[h] + jnp.sum(p, axis=-1, keepdims=True)
            acc_sc[h] = alpha * acc_sc[h] + _dot(p.astype(BF16), v)
            m_sc[h] = m_next

    @pl.when(ki < qi)
    def _():
        update(False)

    @pl.when(ki == qi)
    def _():
        update(True)
        lane = lax.broadcasted_iota(jnp.int32, (tq, LANE), 1)
        o0 = acc_sc[0] / l_sc[0]
        o1 = acc_sc[1] / l_sc[1]
        o_ref[...] = jnp.where(lane < V_HEAD, o0, o1).astype(o_ref.dtype)


def _prompt_attention(q, k, v, b, s):
    tq = min(ATTN_TQ, s)
    nq = s // tq
    pairs = [(i, j) for i in range(nq) for j in range(i + 1)]
    qi_tab = jnp.asarray([p[0] for p in pairs], jnp.int32)
    ki_tab = jnp.asarray([p[1] for p in pairs], jnp.int32)
    grid_spec = pltpu.PrefetchScalarGridSpec(
        num_scalar_prefetch=2,
        grid=(b, H_A // 2, len(pairs)),
        in_specs=[
            pl.BlockSpec((None, tq, 2 * HEAD_PAD), lambda bi, h, t, qt, kt: (bi, qt[t], h)),
            pl.BlockSpec((None, tq, 2 * HEAD_PAD), lambda bi, h, t, qt, kt: (bi, kt[t], h)),
            pl.BlockSpec((None, tq, 2 * V_HEAD), lambda bi, h, t, qt, kt: (bi, kt[t], h)),
        ],
        out_specs=pl.BlockSpec((None, tq, 2 * V_HEAD), lambda bi, h, t, qt, kt: (bi, qt[t], h)),
        scratch_shapes=[pltpu.VMEM((2, tq, LANE), F32), pltpu.VMEM((2, tq, LANE), F32),
                        pltpu.VMEM((2, tq, LANE), F32)],
    )
    return pl.pallas_call(
        _attn_kernel,
        grid_spec=grid_spec,
        out_shape=jax.ShapeDtypeStruct((b, s, H_A * V_HEAD), BF16),
        compiler_params=_cparams(("parallel", "parallel", "arbitrary")),
        name="prompt_attn",
    )(qi_tab, ki_tab, q, k, v)


def _ssd_kernel(xbc_ref, z_ref, dt_ref, wconv_ref, bconv_ref, dtb_ref, ah_ref, dskip_ref, gn_ref,
                tri_ref, exp_ref, expt_ref, y_ref, hfin_ref, xbuf, state):
    ci = pl.program_id(1)
    nc = pl.num_programs(1)
    L = SSD_CHUNK

    @pl.when(ci == 0)
    def _():
        xbuf[0:8, :] = jnp.zeros((8, CONV_DIM), F32)
        state[...] = jnp.zeros(state.shape, F32)

    xbuf[8:8 + L, :] = xbc_ref[...]
    conv = bconv_ref[...] + xbuf[8:8 + L, :] * wconv_ref[3:4, :]
    for w in range(CONV_W - 1):
        sh = CONV_W - 1 - w
        conv = conv + xbuf[8 - sh:8 - sh + L, :] * wconv_ref[w:w + 1, :]
    xbuf[0:8, :] = xbuf[L:L + 8, :]
    xc = _silu(conv)
    xs = xc[:, :D_INNER]

    dt = jax.nn.softplus(dt_ref[...] + dtb_ref[...])
    a = dt * ah_ref[...]
    a_cum = _dot_f32_rhs(tri_ref[...], a)
    a_cum_t = a_cum.T
    a_last = a_cum[L - 1:L, :]
    ex = exp_ref[...]
    dt_x = _dot_f32_lhs(dt, ex)
    dfs_x = _dot_f32_lhs(jnp.exp(a_cum), ex)
    dte_x = _dot_f32_lhs(jnp.exp(a_last - a_cum), ex)
    xdt = xs * dt_x
    xdt_b = xdt.astype(BF16)
    xw_b = (xdt * dte_x).astype(BF16)
    last_col = jnp.broadcast_to(jnp.exp(a_cum_t[:, L - 1:L]), (LANE, LANE))
    carry = _dot_f32_rhs(expt_ref[...], last_col)

    row = lax.broadcasted_iota(jnp.int32, (L, L), 0)
    col = lax.broadcasted_iota(jnp.int32, (L, L), 1)
    causal = col <= row
    lane = lax.broadcasted_iota(jnp.int32, (L, LANE), 1)
    for g in range(N_GROUPS):
        bm = xc[:, D_INNER + g * D_STATE:D_INNER + (g + 1) * D_STATE].astype(BF16)
        cm = xc[:, D_INNER + (N_GROUPS + g) * D_STATE:D_INNER + (N_GROUPS + g + 1) * D_STATE].astype(BF16)
        cb = _dot_nt(cm, bm)
        c0, c1 = g * GROUP_W, (g + 1) * GROUP_W
        st_prev = state[c0:c1, :]
        y_off = _dot_nt(cm, st_prev.astype(BF16)) * dfs_x[:, c0:c1]
        for j in range(HEADS_PER_GROUP // 2):
            h0 = g * HEADS_PER_GROUP + 2 * j
            x2 = xdt_b[:, h0 * SSM_HEAD:(h0 + 2) * SSM_HEAD]
            ys = []
            for hh in (h0, h0 + 1):
                seg = a_cum[:, hh:hh + 1] - a_cum_t[hh:hh + 1, :]
                sc = jnp.where(causal, cb * jnp.exp(seg), 0.0)
                ys.append(_dot(sc.astype(BF16), x2))
            y2 = jnp.where(lane < SSM_HEAD, ys[0], ys[1])
            lo = h0 * SSM_HEAD
            y_ref[:, lo:lo + LANE] = y2 + y_off[:, lo - c0:lo - c0 + LANE]
        st_new = _dot_tn(xw_b[:, c0:c1], bm)
        state[c0:c1, :] = st_prev * carry[c0:c1, :] + st_new

    y = y_ref[...] + dskip_ref[...] * xs
    yg = y * _silu(z_ref[...])
    gn = gn_ref[...]
    for g in range(N_GROUPS):
        c0, c1 = g * GROUP_W, (g + 1) * GROUP_W
        blk = yg[:, c0:c1]
        y_ref[:, c0:c1] = blk * lax.rsqrt(jnp.mean(blk * blk, axis=-1, keepdims=True) + EPS) * gn[:, c0:c1]

    @pl.when(ci == nc - 1)
    def _():
        hfin_ref[...] = state[...]


def _ssd_prompt(xbc, z, dt, sw, b, s):
    nc = s // SSD_CHUNK
    row = lambda w: pl.BlockSpec((SSD_CHUNK, w), lambda bi, ci: (bi * nc + ci, 0))
    consts = (sw['w_conv'], sw['b_conv'], sw['dt_bias128'], sw['a_head128'], sw['d_skip_x'], sw['g_norm'],
              sw['tri'], sw['expand'], sw['expand_t'])
    return pl.pallas_call(
        _ssd_kernel,
        grid=(b, nc),
        in_specs=[row(CONV_DIM), row(D_INNER), row(LANE)] + [_whole()] * len(consts),
        out_specs=(row(D_INNER), pl.BlockSpec((None, D_INNER, D_STATE), lambda bi, ci: (bi, 0, 0))),
        out_shape=(jax.ShapeDtypeStruct((b * s, D_INNER), F32),
                   jax.ShapeDtypeStruct((b, D_INNER, D_STATE), F32)),
        scratch_shapes=[pltpu.VMEM((SSD_CHUNK + 8, CONV_DIM), F32), pltpu.VMEM((D_INNER, D_STATE), F32)],
        compiler_params=_cparams(("parallel", "arbitrary")),
        name="ssd_prompt",
    )(xbc, z, dt, *consts)


def _ssm_step_kernel(xbc_ref, cbuf_ref, z_ref, dt_ref, h_ref, wconv_ref, bconv_ref, dtb_ref, ah_ref,
                     dskip_ref, gn_ref, exp_ref, y_ref, hnew_ref):
    conv = bconv_ref[...] + xbc_ref[0] * wconv_ref[3:4, :]
    for w in range(CONV_W - 1):
        conv = conv + cbuf_ref[0, w:w + 1, :] * wconv_ref[w:w + 1, :]
    xc = _silu(conv)
    xs = xc[:, :D_INNER]
    dt = jax.nn.softplus(dt_ref[0] + dtb_ref[...])
    da = jnp.exp(dt * ah_ref[...])
    ex = exp_ref[...]
    da_x = _dot_f32_lhs(jnp.broadcast_to(da, (8, LANE)), ex)[0:1, :]
    xdt = xs * _dot_f32_lhs(jnp.broadcast_to(dt, (8, LANE)), ex)[0:1, :]

    eye = lax.broadcasted_iota(jnp.int32, (LANE, LANE), 0) == lax.broadcasted_iota(jnp.int32, (LANE, LANE), 1)

    def to_col(rowvec):
        return jnp.sum(jnp.where(eye, jnp.broadcast_to(rowvec, (LANE, LANE)), 0.0), axis=-1, keepdims=True)

    ys = []
    for g in range(N_GROUPS):
        bt = xc[:, D_INNER + g * D_STATE:D_INNER + (g + 1) * D_STATE]
        ct = xc[:, D_INNER + (N_GROUPS + g) * D_STATE:D_INNER + (N_GROUPS + g + 1) * D_STATE]
        ct8 = jnp.broadcast_to(ct, (8, D_STATE)).astype(BF16)
        for j in range(GROUP_W // LANE):
            r0 = g * GROUP_W + j * LANE
            h_old = h_ref[0, r0:r0 + LANE, :]
            h_new = h_old * to_col(da_x[:, r0:r0 + LANE]) + to_col(xdt[:, r0:r0 + LANE]) * bt
            hnew_ref[0, r0:r0 + LANE, :] = h_new
            ys.append(_dot_nt(ct8, h_new.astype(BF16))[0:1, :])
    y = jnp.concatenate(ys, axis=-1) + dskip_ref[...] * xs
    yg = y * _silu(z_ref[0])
    gn = gn_ref[...]
    for g in range(N_GROUPS):
        c0, c1 = g * GROUP_W, (g + 1) * GROUP_W
        blk = yg[:, c0:c1]
        y_ref[0, :, c0:c1] = blk * lax.rsqrt(jnp.mean(blk * blk, axis=-1, keepdims=True) + EPS) * gn[:, c0:c1]


def _ssm_sample(xbc, conv_buf, z, dt, h, sw):
    bd = xbc.shape[0]
    vec = lambda w: pl.BlockSpec((1, 1, w), lambda i: (i, 0, 0))
    consts = (sw['w_conv'], sw['b_conv'], sw['dt_bias128'], sw['a_head128'], sw['d_skip_x'], sw['g_norm'],
              sw['expand'])
    y, h_new = pl.pallas_call(
        _ssm_step_kernel,
        grid=(bd,),
        in_specs=[vec(CONV_DIM), pl.BlockSpec((1, CONV_W - 1, CONV_DIM), lambda i: (i, 0, 0)), vec(D_INNER),
                  vec(LANE), pl.BlockSpec((1, D_INNER, D_STATE), lambda i: (i, 0, 0))] + [_whole()] * len(consts),
        out_specs=(vec(D_INNER), pl.BlockSpec((1, D_INNER, D_STATE), lambda i: (i, 0, 0))),
        out_shape=(jax.ShapeDtypeStruct((bd, 1, D_INNER), F32),
                   jax.ShapeDtypeStruct((bd, D_INNER, D_STATE), F32)),
        compiler_params=_cparams(("parallel",)),
        name="ssm_sample",
    )(xbc.reshape(bd, 1, CONV_DIM), conv_buf, z.reshape(bd, 1, D_INNER), dt.reshape(bd, 1, LANE), h, *consts)
    return y.reshape(bd, D_INNER), h_new


def _decode_kernel(pt_ref, qp_ref, qr_ref, gkp_ref, gkr_ref, wuk_ref, segt_ref, ones_ref, cache_c, cache_kr,
                   acc_ref, m_ref, l_ref, cpage, krpage, cb_sc, pcat_sc, krb_sc, kr2b_sc, sem_c, sem_kr):
    b = pl.program_id(0)
    nb = pl.num_programs(0)
    n_chunks = pt_ref.shape[1] // DEC_PPS
    page = cpage.shape[2]
    sub_rows = DEC_SUB * page
    n_sub = DEC_PPS // DEC_SUB
    nj = (H_A * QK_NOPE) // LANE

    def page_copies(pid, slot, i):
        return (pltpu.make_async_copy(cache_c.at[pid], cpage.at[slot, i], sem_c.at[slot]),
                pltpu.make_async_copy(cache_kr.at[pid], krpage.at[slot, i], sem_kr.at[slot]))

    def issue(bb, c, slot):
        for i in range(DEC_PPS):
            for cp in page_copies(pt_ref[bb, c * DEC_PPS + i], slot, i):
                cp.start()

    def wait(slot):
        for i in range(DEC_PPS):
            for cp in page_copies(0, slot, i):
                cp.wait()

    @pl.when(b == 0)
    def _():
        cb_sc[1] = jnp.zeros(cb_sc.shape[1:], BF16)
        pcat_sc[1] = jnp.zeros(pcat_sc.shape[1:], BF16)
        krb_sc[1] = jnp.zeros(krb_sc.shape[1:], BF16)
        kr2b_sc[1] = jnp.zeros(kr2b_sc.shape[1:], BF16)
        issue(0, 0, 0)

    qg = qp_ref[0] * gkp_ref[...]
    qr = (qr_ref[0] * gkr_ref[...]).astype(BF16)

    def keys_partial(slot, sub):
        cb = jnp.concatenate([cpage[slot, sub * DEC_SUB + t].astype(BF16) for t in range(DEC_SUB)], axis=0)
        r0 = sub * sub_rows
        cb_sc[slot, r0:r0 + sub_rows, :] = cb
        kk = _dot(cb, wuk_ref[...])
        p_sq = None
        p_qs = None
        for jj in range(nj):
            blk = kk[:, jj * LANE:(jj + 1) * LANE]
            sq = blk * blk
            qs = blk * qg[:, jj * LANE:(jj + 1) * LANE]
            p_sq = sq if p_sq is None else p_sq + sq
            p_qs = qs if p_qs is None else p_qs + qs
        pcat_sc[slot, r0:r0 + sub_rows, :] = jnp.concatenate([p_sq, p_qs], axis=1).astype(BF16)

    def rope_keys(slot):
        krt = jnp.concatenate([krpage[slot, i] for i in range(DEC_PPS)], axis=1)
        krb_sc[slot] = krt.astype(BF16)
        kr2b_sc[slot] = (krt * krt).astype(BF16)

    def scores(slot):
        ns = _dot_nt(segt_ref[...], pcat_sc[slot])
        ns = ns + _dot(qr, krb_sc[slot]) + _dot(ones_ref[...], kr2b_sc[slot])
        return ns[H_A:2 * H_A, :] * lax.rsqrt(ns[0:H_A, :] * (1.0 / QK_HEAD) + EPS)

    def softmax_step(s, m_prev, l_prev):
        m_next = jnp.maximum(m_prev, jnp.max(s, axis=-1, keepdims=True))
        alpha = jnp.exp2(m_prev - m_next)
        p = jnp.exp2(s - m_next)
        return p, alpha, m_next, alpha * l_prev + jnp.sum(p, axis=-1, keepdims=True)

    def chunk_step(c, slot, carry):
        m_prev, l_prev, acc_prev = carry
        wait(slot)
        last = c + 1 == n_chunks
        nxt_b = jnp.where(last, b + 1, b)
        nxt_c = jnp.where(last, 0, c + 1)

        @pl.when(nxt_b < nb)
        def _():
            issue(nxt_b, nxt_c, 1 - slot)

        valid = c > 0
        prev = 1 - slot
        keys_partial(slot, 0)
        s = scores(prev)
        keys_partial(slot, 1)
        p, alpha, m_next, l_next = softmax_step(s, m_prev, l_prev)
        keys_partial(slot, 2)
        acc_next = acc_prev * alpha + _dot(p.astype(BF16), cb_sc[prev])
        for sub in range(3, n_sub):
            keys_partial(slot, sub)
        rope_keys(slot)
        return (jnp.where(valid, m_next, m_prev), jnp.where(valid, l_next, l_prev),
                jnp.where(valid, acc_next, acc_prev))

    def pair_step(i, carry):
        carry = chunk_step(2 * i, 0, carry)
        return chunk_step(2 * i + 1, 1, carry)

    init = (jnp.full((H_A, 1), -jnp.inf, F32), jnp.zeros((H_A, 1), F32), jnp.zeros((H_A, KV_LORA), F32))
    m_run, l_run, acc = lax.fori_loop(0, n_chunks // 2, pair_step, init)
    p, alpha, m_run, l_run = softmax_step(scores(1), m_run, l_run)
    acc_ref[0] = acc * alpha + _dot(p.astype(BF16), cb_sc[1])
    m_ref[0] = jnp.broadcast_to(m_run, (H_A, LANE))
    l_ref[0] = jnp.broadcast_to(l_run, (H_A, LANE))


def _decode_attention(page_table, q_perm, qr_mat, cache_c, cache_krt, dw):
    bd, n_pages = page_table.shape
    page = cache_c.shape[1]
    rows = DEC_PPS * page
    assert n_pages % (2 * DEC_PPS) == 0 and DEC_PPS // DEC_SUB >= 3
    const = lambda shp: pl.BlockSpec(shp, lambda b, pt: (0,) * len(shp))
    in_specs = [pl.BlockSpec((1, 1, H_A * QK_NOPE), lambda b, pt: (b, 0, 0)),
                pl.BlockSpec((1, 2 * H_A, QK_ROPE), lambda b, pt: (b, 0, 0)),
                const((1, H_A * QK_NOPE)), const((1, QK_ROPE)), const((KV_LORA, H_A * QK_NOPE)),
                const((2 * H_A, 2 * LANE)), const((2 * H_A, QK_ROPE)),
                pl.BlockSpec(memory_space=pl.ANY), pl.BlockSpec(memory_space=pl.ANY)]
    out_b = lambda w: pl.BlockSpec((1, H_A, w), lambda b, pt: (b, 0, 0))
    grid_spec = pltpu.PrefetchScalarGridSpec(
        num_scalar_prefetch=1,
        grid=(bd,),
        in_specs=in_specs,
        out_specs=(out_b(KV_LORA), out_b(LANE), out_b(LANE)),
        scratch_shapes=[pltpu.VMEM((2, DEC_PPS, page, KV_LORA), F32),
                        pltpu.VMEM((2, DEC_PPS, QK_ROPE, page), F32),
                        pltpu.VMEM((2, rows, KV_LORA), BF16),
                        pltpu.VMEM((2, rows, 2 * LANE), BF16),
                        pltpu.VMEM((2, QK_ROPE, rows), BF16),
                        pltpu.VMEM((2, QK_ROPE, rows), BF16),
                        pltpu.SemaphoreType.DMA((2,)), pltpu.SemaphoreType.DMA((2,))],
    )
    return pl.pallas_call(
        _decode_kernel,
        grid_spec=grid_spec,
        out_shape=(jax.ShapeDtypeStruct((bd, H_A, KV_LORA), F32),
                   jax.ShapeDtypeStruct((bd, H_A, LANE), F32),
                   jax.ShapeDtypeStruct((bd, H_A, LANE), F32)),
        compiler_params=_cparams(("arbitrary",)),
        name="decode_attn",
    )(page_table, q_perm.reshape(bd, 1, -1), qr_mat, dw['gk_perm'], dw['gk_rope'], dw['w_uk_perm'], dw['seg_t'],
      dw['ones_rows'], cache_c, cache_krt)


def _decode_final_kernel(q_ref, k_ref, c_ref, m_ref, l_ref, acc_ref, wuv_ref, o_ref):
    c_new = c_ref[...]
    for pair in range(H_A // 2):
        o_pair = None
        for h in (2 * pair, 2 * pair + 1):
            lo, hi = h * HEAD_PAD, (h + 1) * HEAD_PAD
            s_new = jnp.sum(q_ref[:, lo:hi] * k_ref[:, lo:hi].astype(F32), axis=-1, keepdims=True)
            m_old = m_ref[:, h:h + 1]
            m_new = jnp.maximum(m_old, s_new)
            a = jnp.exp2(m_old - m_new)
            pn = jnp.exp2(s_new - m_new)
            l_new = l_ref[:, h:h + 1] * a + pn
            ctx = (acc_ref[:, h * KV_LORA:(h + 1) * KV_LORA] * a + pn * c_new) / l_new
            part = _dot(ctx.astype(BF16), wuv_ref[h])
            o_pair = part if o_pair is None else o_pair + part
        o_ref[:, pair * LANE:(pair + 1) * LANE] = o_pair.astype(o_ref.dtype)


def _decode_final(q, k_new, c_new, m, l, acc, dw):
    bd = q.shape[0]
    return pl.pallas_call(
        _decode_final_kernel,
        in_specs=[_whole()] * 7,
        out_specs=_whole(),
        out_shape=jax.ShapeDtypeStruct((bd, H_A * V_HEAD), BF16),
        compiler_params=pltpu.CompilerParams(vmem_limit_bytes=VMEM_LIMIT),
        name="decode_final",
    )(q, k_new, c_new, m, l, acc, dw['w_uv_pair'])


def _merge_kernel(x_ref, attn_ref, ssm_ref, gates_ref, wa_ref, wb_ref, wo_ref, gffn_ref, wr_ref, br_ref,
                  h_ref, hn_ref, logit_ref):
    g = jax.nn.sigmoid(gates_ref[...])
    a = _dot(attn_ref[...], wa_ref[...])
    b = _dot(ssm_ref[...].astype(BF16), wb_ref[...])
    mixed = g[:, :D_MODEL] * a + g[:, D_MODEL:] * b
    h = x_ref[...] + _dot(mixed.astype(BF16), wo_ref[...])
    h_ref[...] = h
    hn = _rms(h, gffn_ref[...])
    hn_ref[...] = hn
    hi, mid, _ = _split3(hn)
    w_hi = wr_ref[0]
    w_lo = wr_ref[1]
    logit_ref[...] = _dot(hi, w_hi) + _dot(mid, w_hi) + _dot(hi, w_lo) + br_ref[...]


def _merge(x, attn, ssm, gates, mw, *, tm):
    t = x.shape[0]
    row = lambda w: pl.BlockSpec((tm, w), lambda i: (i, 0))
    consts = (mw['w_a_out'], mw['w_b_out'], mw['w_out'], mw['g_ffn'], mw['w_router2'], mw['b_router128'])
    return pl.pallas_call(
        _merge_kernel,
        grid=(t // tm,),
        in_specs=[row(D_MODEL), row(H_A * V_HEAD), row(D_INNER), row(2 * D_MODEL)] + [_whole()] * len(consts),
        out_specs=(row(D_MODEL), row(D_MODEL), row(LANE)),
        out_shape=(jax.ShapeDtypeStruct((t, D_MODEL), F32), jax.ShapeDtypeStruct((t, D_MODEL), F32),
                   jax.ShapeDtypeStruct((t, LANE), F32)),
        compiler_params=_cparams(("parallel",)),
        name="merge",
    )(x, attn, ssm, gates, *consts)


def _moe_kernel(be_ref, bv_ref, seg_ref, tok_ref, x_hbm, wgu_ref, bgu_ref, wd_ref, bd_ref, out_ref,
                xbuf, wgu_b, wd_b, sem):
    i = pl.program_id(0)
    n = pl.num_programs(0)
    bm = out_ref.shape[0]

    def row_copy(tok, slot, r):
        return pltpu.make_async_copy(x_hbm.at[pl.ds(tok, 1), :], xbuf.at[slot, pl.ds(r, 1), :], sem.at[slot])

    def issue_loop(blk, slot):
        base = seg_ref[blk]

        def body(r, carry):
            row_copy(tok_ref[base + r], slot, r).start()
            return carry
        lax.fori_loop(0, bm, body, 0, unroll=8)

    def issue_inline(blk, slot):
        base = seg_ref[blk]
        for r in range(bm):
            row_copy(tok_ref[base + r], slot, r).start()

    def wait(slot):
        def body(r, carry):
            row_copy(0, slot, r).wait()
            return carry
        lax.fori_loop(0, bm, body, 0, unroll=8)

    @pl.when((i == 0) & (bv_ref[0] > 0))
    def _():
        issue_loop(0, 0)

    prev = jnp.maximum(i - 1, 0)

    @pl.when((i == 0) | (be_ref[i] != be_ref[prev]))
    def _():
        wgu_b[...] = wgu_ref[0].astype(BF16)
        wd_b[...] = wd_ref[0].astype(BF16)

    def expert_block(prefetch_next):
        slot = i % 2
        wait(slot)
        x = xbuf[slot].astype(BF16)
        if prefetch_next:
            issue_inline(i + 1, 1 - slot)
        gu = _dot(x, wgu_b[...]) + bgu_ref[0]
        gate = jnp.minimum(gu[:, :D_FF], SWIGLU_LIMIT)
        up = jnp.clip(gu[:, D_FF:], -SWIGLU_LIMIT, SWIGLU_LIMIT)
        act = (up + 1.0) * gate * jax.nn.sigmoid(SWIGLU_ALPHA * gate)
        out_ref[...] = _dot(act.astype(BF16), wd_b[...]) + bd_ref[0]

    nxt = jnp.minimum(i + 1, n - 1)
    valid = bv_ref[i] > 0
    nxt_valid = (i + 1 < n) & (bv_ref[nxt] > 0)

    @pl.when(valid & nxt_valid)
    def _():
        expert_block(True)

    @pl.when(valid & jnp.logical_not(nxt_valid))
    def _():
        expert_block(False)

    @pl.when(jnp.logical_not(valid))
    def _():
        out_ref[...] = jnp.zeros(out_ref.shape, F32)


def _moe_experts(block_e, block_valid, seg_start, tok_sorted, x, w_gate_up, b_gate_up, w_down, b_down):
    n_blocks = block_e.shape[0]
    bm = MOE_BM
    grid_spec = pltpu.PrefetchScalarGridSpec(
        num_scalar_prefetch=4,
        grid=(n_blocks,),
        in_specs=[
            pl.BlockSpec(memory_space=pl.ANY),
            pl.BlockSpec((1, D_MODEL, 2 * D_FF), lambda i, be, bv, sg, tk: (be[i], 0, 0)),
            pl.BlockSpec((1, 1, 2 * D_FF), lambda i, be, bv, sg, tk: (be[i], 0, 0)),
            pl.BlockSpec((1, D_FF, D_MODEL), lambda i, be, bv, sg, tk: (be[i], 0, 0)),
            pl.BlockSpec((1, 1, D_MODEL), lambda i, be, bv, sg, tk: (be[i], 0, 0)),
        ],
        out_specs=pl.BlockSpec((bm, D_MODEL), lambda i, be, bv, sg, tk: (i, 0)),
        scratch_shapes=[pltpu.VMEM((2, bm, D_MODEL), F32), pltpu.VMEM((D_MODEL, 2 * D_FF), BF16),
                        pltpu.VMEM((D_FF, D_MODEL), BF16), pltpu.SemaphoreType.DMA((2,))],
    )
    return pl.pallas_call(
        _moe_kernel,
        grid_spec=grid_spec,
        out_shape=jax.ShapeDtypeStruct((n_blocks * bm, D_MODEL), F32),
        compiler_params=_cparams(("arbitrary",)),
        name="moe_experts",
    )(block_e, block_valid, seg_start, tok_sorted, x, w_gate_up, b_gate_up.reshape(N_EXPERTS, 1, 2 * D_FF), w_down,
      b_down.reshape(N_EXPERTS, 1, D_MODEL))


def _ple_kernel(pos_ref, y_hbm, h_ref, gw_ref, p_ref, gin_ref, wgate_ref, wple_ref, gple_ref, o_ref,
                gbuf, sem):
    i = pl.program_id(0)
    n = pl.num_programs(0)
    tm = o_ref.shape[0]

    def row_copy(src, slot, k, r):
        return pltpu.make_async_copy(y_hbm.at[pl.ds(src, 1), :], gbuf.at[slot, k, pl.ds(r, 1), :], sem.at[slot])

    def issue(blk, slot):
        def body(r, carry):
            for k in range(TOP_K):
                row_copy(pos_ref[(blk * tm + r) * TOP_K + k], slot, k, r).start()
            return carry
        lax.fori_loop(0, tm, body, 0, unroll=4)

    def wait(slot):
        def body(r, carry):
            for k in range(TOP_K):
                row_copy(0, slot, k, r).wait()
            return carry
        lax.fori_loop(0, tm, body, 0, unroll=4)

    @pl.when(i == 0)
    def _():
        issue(0, 0)

    @pl.when(i + 1 < n)
    def _():
        issue(i + 1, (i + 1) % 2)

    slot = i % 2
    wait(slot)
    gw = gw_ref[...]
    y = gbuf[slot, 0] * gw[:, 0:1]
    for k in range(1, TOP_K):
        y = y + gbuf[slot, k] * gw[:, k:k + 1]
    h2 = h_ref[...] + y
    gate = jax.nn.sigmoid(_dot(_rms(h2, gin_ref[...]).astype(BF16), wgate_ref[...]))
    ple = _rms(_dot(p_ref[...].astype(BF16), wple_ref[...]), gple_ref[...]) * gate
    o_ref[...] = h2 + ple


def _combine_ple(pos, y_sorted, h, gate_w, p, pw):
    t = h.shape[0]
    tm = PLE_TM
    row = lambda w: pl.BlockSpec((tm, w), lambda i, ps: (i, 0))
    whole = pl.BlockSpec(memory_space=pltpu.VMEM)
    grid_spec = pltpu.PrefetchScalarGridSpec(
        num_scalar_prefetch=1,
        grid=(t // tm,),
        in_specs=[pl.BlockSpec(memory_space=pl.ANY), row(D_MODEL), row(LANE), row(PLE_DIM),
                  whole, whole, whole, whole],
        out_specs=row(D_MODEL),
        scratch_shapes=[pltpu.VMEM((2, TOP_K, tm, D_MODEL), F32), pltpu.SemaphoreType.DMA((2,))],
    )
    return pl.pallas_call(
        _ple_kernel,
        grid_spec=grid_spec,
        out_shape=jax.ShapeDtypeStruct((t, D_MODEL), F32),
        compiler_params=_cparams(("arbitrary",)),
        name="combine_ple",
    )(pos, y_sorted, h, gate_w, p, pw['g_ple_in'], pw['w_ple_gate'], pw['w_ple'], pw['g_ple'])


def _pad_lanes(x, n):
    return jnp.pad(x, [(0, 0)] * (x.ndim - 1) + [(0, n - x.shape[-1])])


def _rot_cols(w):
    half = QK_ROPE // 2
    return jnp.concatenate([-w[..., half:], w[..., :half]], axis=-1)


def _prep_weights(g_mix_norm, w_in, g_q_a, w_q_b, g_kv_a, w_uk, w_uv, g_q_head, g_k_head, w_a_out, w_conv,
                  b_conv, dt_bias, a_log, d_skip, g_ssm_norm, w_b_out, w_out, g_ffn_norm, w_router, b_router,
                  g_ple_in, w_ple_gate, w_ple, g_ple):
    sizes = (Q_LORA, KV_LORA, QK_ROPE, D_INNER, CONV_DIM, H_B, 2 * D_MODEL)
    offs = np.concatenate([[0], np.cumsum(sizes)])
    wq, wkv, wkr, wz, wxbc, wdt, wg = [w_in[:, int(offs[i]):int(offs[i + 1])] for i in range(7)]
    zc = lambda n: jnp.zeros((D_MODEL, n), F32)
    kr128 = jnp.concatenate([zc(QK_NOPE), wkr, zc(LANE - QK_HEAD)], axis=1)
    krrot = jnp.concatenate([zc(QK_NOPE), _rot_cols(wkr), zc(LANE - QK_HEAD)], axis=1)
    w_lat = jnp.concatenate([wq, wkv, kr128, krrot, _pad_lanes(wdt, LANE)], axis=1).astype(BF16)

    wqb = w_q_b.reshape(Q_LORA, H_A, QK_HEAD)
    nope, rope_w = wqb[..., :QK_NOPE], wqb[..., QK_NOPE:]
    z_nope = jnp.zeros_like(nope)
    q128 = _pad_lanes(jnp.concatenate([nope, rope_w], axis=-1), LANE).reshape(Q_LORA, H_A * LANE)
    qrot = _pad_lanes(jnp.concatenate([z_nope, _rot_cols(rope_w)], axis=-1), LANE).reshape(Q_LORA, H_A * LANE)
    w_qb = jnp.concatenate([q128, qrot], axis=1).astype(BF16)

    wuk3 = w_uk.reshape(KV_LORA, H_A, QK_NOPE)
    w_uk_pad = _pad_lanes(wuk3, LANE).reshape(KV_LORA, H_A * LANE).astype(BF16)
    sub = LANE // H_A
    w_uk_perm = wuk3.reshape(KV_LORA, H_A, QK_NOPE // sub, sub).transpose(0, 2, 1, 3)
    w_uk_perm = w_uk_perm.reshape(KV_LORA, H_A * QK_NOPE).astype(BF16)
    gk_perm = jnp.broadcast_to(g_k_head[:QK_NOPE].reshape(1, QK_NOPE // sub, 1, sub),
                               (1, QK_NOPE // sub, H_A, sub)).reshape(1, H_A * QK_NOPE)

    front = dict(
        g_mix=g_mix_norm.reshape(1, -1), w_lat=w_lat, w_z=wz.astype(BF16), w_xbc=wxbc.astype(BF16),
        w_g=wg.astype(BF16), g_q_a=g_q_a.reshape(1, -1), w_qb=w_qb, g_kv_a=g_kv_a.reshape(1, -1),
        w_uk_pad=w_uk_pad, w_uv=w_uv.astype(BF16),
        g_q128=_pad_lanes(g_q_head.reshape(1, -1), LANE) * Q_SCALE,
        g_k128=_pad_lanes(g_k_head.reshape(1, -1), LANE),
    )

    lane_head = np.arange(LANE) // sub
    seg = (lane_head[None, :] == np.arange(H_A)[:, None]).astype(np.float32)
    seg_t = np.zeros((2 * H_A, 2 * LANE), np.float32)
    seg_t[:H_A, :LANE] = seg
    seg_t[H_A:, LANE:] = seg
    ones_rows = np.zeros((2 * H_A, QK_ROPE), np.float32)
    ones_rows[:H_A] = 1.0
    wuv3 = w_uv.reshape(KV_LORA, H_A, V_HEAD)
    even = jnp.arange(H_A) % 2 == 0
    w_uv_pair = jnp.where(even[None, :, None],
                          jnp.concatenate([wuv3, jnp.zeros_like(wuv3)], axis=-1),
                          jnp.concatenate([jnp.zeros_like(wuv3), wuv3], axis=-1))
    decode = dict(
        gk_perm=gk_perm, gk_rope=g_k_head[QK_NOPE:].reshape(1, QK_ROPE), w_uk_perm=w_uk_perm, seg_t=jnp.asarray(seg_t, BF16), ones_rows=jnp.asarray(ones_rows, BF16),
        w_uv_pair=w_uv_pair.transpose(1, 0, 2).astype(BF16),
    )

    expand = (np.arange(D_INNER)[None, :] // SSM_HEAD == np.arange(LANE)[:, None]).astype(np.float32)
    tri = (np.arange(SSD_CHUNK)[None, :] <= np.arange(SSD_CHUNK)[:, None]).astype(np.float32)
    ssm = dict(
        w_conv=w_conv, b_conv=b_conv.reshape(1, -1), dt_bias128=_pad_lanes(dt_bias.reshape(1, -1), LANE),
        a_head128=_pad_lanes(-jnp.exp(a_log).reshape(1, -1), LANE),
        d_skip_x=jnp.repeat(d_skip, SSM_HEAD).reshape(1, -1), g_norm=g_ssm_norm.reshape(1, -1),
        tri=jnp.asarray(tri, BF16), expand=jnp.asarray(expand, BF16), expand_t=jnp.asarray(expand.T, BF16),
    )

    wr = _pad_lanes(w_router, LANE)
    wr_hi = wr.astype(BF16)
    wr_lo = (wr - wr_hi.astype(F32)).astype(BF16)
    merge = dict(
        w_a_out=w_a_out.astype(BF16), w_b_out=w_b_out.astype(BF16), w_out=w_out.astype(BF16),
        g_ffn=g_ffn_norm.reshape(1, -1), w_router2=jnp.stack([wr_hi, wr_lo]),
        b_router128=_pad_lanes(b_router.reshape(1, -1), LANE),
    )
    ple = dict(g_ple_in=g_ple_in.reshape(1, -1), w_ple_gate=w_ple_gate.astype(BF16), w_ple=w_ple.astype(BF16),
               g_ple=g_ple.reshape(1, -1))
    return front, decode, ssm, merge, ple


def _rope_tables(pos):
    half = QK_ROPE // 2
    inv = ROPE_THETA ** (-jnp.arange(half, dtype=F32) / half)
    ang = pos.astype(F32)[:, None] * inv[None, :]
    cos, sin = jnp.cos(ang), jnp.sin(ang)
    n = pos.shape[0]
    cos128 = jnp.concatenate([jnp.ones((n, QK_NOPE), F32), cos, cos, jnp.ones((n, LANE - QK_HEAD), F32)], axis=1)
    sin128 = jnp.concatenate([jnp.zeros((n, QK_NOPE), F32), sin, sin, jnp.zeros((n, LANE - QK_HEAD), F32)], axis=1)
    return cos128, sin128


def _route(logits, n_tok):
    top_val, top_idx = lax.top_k(logits[:, :N_EXPERTS], TOP_K)
    gate_w = jax.nn.softmax(top_val, axis=-1)
    n_assign = n_tok * TOP_K
    flat_e = top_idx.reshape(-1).astype(jnp.int32)
    onehot = (flat_e[:, None] == jnp.arange(N_EXPERTS, dtype=jnp.int32)[None, :]).astype(jnp.int32)
    csum = jnp.cumsum(onehot, axis=0)
    counts = csum[-1]
    bm = MOE_BM
    padded = (counts + bm - 1) // bm * bm
    pend = jnp.cumsum(padded)
    pstarts = pend - padded
    starts = jnp.cumsum(counts) - counts
    dest = jnp.sum(onehot * (csum - 1 + pstarts[None, :]), axis=1).astype(jnp.int32)
    idx_bits = max(1, (n_assign - 1).bit_length())
    key = jnp.sort(flat_e * (1 << idx_bits) + jnp.arange(n_assign, dtype=jnp.int32))
    tok_sorted = jnp.pad((key & ((1 << idx_bits) - 1)) // TOP_K, (0, bm))
    n_blocks = -(-n_assign // bm) + N_EXPERTS
    blk_start = jnp.arange(n_blocks, dtype=jnp.int32) * bm
    block_e = jnp.minimum(jnp.sum((pend[None, :] <= blk_start[:, None]).astype(jnp.int32), axis=1), N_EXPERTS - 1)
    block_valid = (blk_start < pend[-1]).astype(jnp.int32)
    seg_start = jnp.clip(starts[block_e] + blk_start - pstarts[block_e], 0, n_assign).astype(jnp.int32)
    return gate_w, dest, tok_sorted.astype(jnp.int32), block_e.astype(jnp.int32), block_valid, seg_start


def kernel(x_prompt, x_sample, p_prompt, p_sample, cache_kv_latent, cache_k_rope, page_table, state_conv, state_ssm, g_mix_norm, w_in, g_q_a, w_q_b, g_kv_a, w_uk, w_uv, g_q_head, g_k_head, w_a_out, w_conv, b_conv, dt_bias, a_log, d_skip, g_ssm_norm, w_b_out, w_out, g_ffn_norm, w_router, b_router, w_gate_up, b_gate_up, w_down, b_down, g_ple_in, w_ple_gate, w_ple, g_ple):
    depth = g_mix_norm.shape[0]
    assert depth == 1, "one layer"
    b, s, _ = x_prompt.shape
    bd, sd, _ = x_sample.shape
    assert sd == 1, "one new token per sample sequence"
    n_pages = page_table.shape[1]
    page_size = cache_kv_latent.shape[2]
    assert n_pages % DEC_PPS == 0 and s % SSD_CHUNK == 0
    tp = b * s

    fw, dw, sw, mw, pw = _prep_weights(
        g_mix_norm[0], w_in[0], g_q_a[0], w_q_b[0], g_kv_a[0], w_uk[0], w_uv[0], g_q_head[0], g_k_head[0],
        w_a_out[0], w_conv[0], b_conv[0], dt_bias[0], a_log[0], d_skip[0], g_ssm_norm[0], w_b_out[0], w_out[0],
        g_ffn_norm[0], w_router[0], b_router[0], g_ple_in[0], w_ple_gate[0], w_ple[0], g_ple[0])

    xp = x_prompt.reshape(tp, D_MODEL)
    cos_p, sin_p = _rope_tables(jnp.arange(s))
    tm_p = min(FRONT_TM, s)
    q_p, k_p, v_p, c_p, kr_p, z_p, xbc_p, dt_p, gates_p = _front(
        xp, cos_p, sin_p, fw, tm=tm_p, q_dtype=BF16, pos_blocks=s // tm_p)
    attn_p = _prompt_attention(q_p.reshape(b, s, -1), k_p.reshape(b, s, -1), v_p.reshape(b, s, -1), b, s)
    ssm_p, hfin_p = _ssd_prompt(xbc_p, z_p, dt_p, sw, b, s)
    h_p, hn_p, logit_p = _merge(xp, attn_p.reshape(tp, -1), ssm_p, gates_p, mw, tm=min(MERGE_TM, tp))

    xs = x_sample.reshape(bd, D_MODEL)
    cos_s, sin_s = _rope_tables(jnp.full((bd,), n_pages * page_size, jnp.int32))
    q_s, k_s, _, c_s, kr_s, z_s, xbc_s, dt_s, gates_s = _front(
        xs, cos_s, sin_s, fw, tm=bd, q_dtype=F32, pos_blocks=1)
    q3 = q_s.reshape(bd, H_A, LANE)
    sub = LANE // H_A
    q_perm = q3[:, :, :QK_NOPE].reshape(bd, H_A, QK_NOPE // sub, sub).transpose(0, 2, 1, 3).reshape(bd, -1)
    qr = q3[:, :, QK_NOPE:QK_HEAD]
    qr_mat = jnp.concatenate([jnp.zeros_like(qr), qr], axis=1)
    acc, m_run, l_run = _decode_attention(page_table, q_perm, qr_mat, cache_kv_latent[0],
                                          jnp.swapaxes(cache_k_rope[0], 1, 2), dw)
    attn_s = _decode_final(q_s, k_s, c_s, m_run[:, :, 0], l_run[:, :, 0], acc.reshape(bd, -1), dw)
    ssm_s, hnew_s = _ssm_sample(xbc_s, state_conv[0], z_s, dt_s,
                                state_ssm[0].reshape(bd, D_INNER, D_STATE), sw)
    h_s, hn_s, logit_s = _merge(xs, attn_s, ssm_s, gates_s, mw, tm=bd)

    h_all = jnp.concatenate([h_p, h_s], axis=0)
    hn_all = jnp.concatenate([hn_p, hn_s], axis=0)
    logits = jnp.concatenate([logit_p, logit_s], axis=0)
    p_all = jnp.concatenate([p_prompt[0].reshape(tp, PLE_DIM), p_sample[0].reshape(bd, PLE_DIM)], axis=0)
    n_tok = tp + bd
    gate_w, dest, tok_sorted, block_e, block_valid, seg_start = _route(logits, n_tok)
    y_sorted = _moe_experts(block_e, block_valid, seg_start, tok_sorted, hn_all, w_gate_up[0], b_gate_up[0],
                            w_down[0], b_down[0])
    out_all = _combine_ple(dest, y_sorted, h_all, _pad_lanes(gate_w, LANE), p_all, pw)

    y_prompt = out_all[:tp].reshape(b, s, D_MODEL)
    y_sample = out_all[tp:].reshape(bd, sd, D_MODEL)
    new_c_p = c_p.reshape(1, b, s, KV_LORA)
    new_kr_p = kr_p[:, QK_NOPE:QK_HEAD].reshape(1, b, s, QK_ROPE)
    conv_p = xbc_p.reshape(b, s, CONV_DIM)[:, s - (CONV_W - 1):].reshape(1, b, CONV_W - 1, CONV_DIM)
    ssm_state_p = hfin_p.reshape(1, b, H_B, SSM_HEAD, D_STATE)
    new_c_s = c_s.reshape(1, bd, sd, KV_LORA)
    new_kr_s = kr_s[:, QK_NOPE:QK_HEAD].reshape(1, bd, sd, QK_ROPE)
    conv_s = jnp.concatenate([state_conv[0][:, 1:], xbc_s[:, None, :]], axis=1).reshape(1, bd, CONV_W - 1, CONV_DIM)
    ssm_state_s = hnew_s.reshape(1, bd, H_B, SSM_HEAD, D_STATE)
    return (y_prompt, y_sample, new_c_p, new_kr_p, conv_p, ssm_state_p, new_c_s, new_kr_s, conv_s, ssm_state_s)
```

```python
import functools
import math

import jax
import jax.numpy as jnp
import numpy as np
from jax import lax
from jax.experimental import pallas as pl
from jax.experimental.pallas import tpu as pltpu

F32 = jnp.float32
BF16 = jnp.bfloat16

D_MODEL = 1024
H_A = 16
Q_LORA = 384
KV_LORA = 256
QK_NOPE = 64
QK_ROPE = 32
QK_HEAD = QK_NOPE + QK_ROPE
V_HEAD = 64
ROPE_THETA = 10000.0
D_INNER = 2 * D_MODEL
SSM_HEAD = 64
H_B = D_INNER // SSM_HEAD
N_GROUPS = 4
D_STATE = 128
CONV_W = 4
CONV_DIM = D_INNER + 2 * N_GROUPS * D_STATE
SSD_CHUNK = 128
N_EXPERTS = 32
TOP_K = 4
D_FF = D_MODEL
SWIGLU_LIMIT = 7.0
SWIGLU_ALPHA = 1.702
PLE_DIM = 256
EPS = 1e-6

LANE = 128
HEAD_PAD = LANE
GROUP_W = D_INNER // N_GROUPS
HEADS_PER_GROUP = H_B // N_GROUPS
Q_SCALE = QK_HEAD ** -0.5 * math.log2(math.e)
VMEM_LIMIT = 56 * 1024 * 1024

FRONT_TM = 256
ATTN_TQ = 1024
ATTN_MAX_BOUND = 40.0
MERGE_TM = 256
MOE_BM = 256
PLE_TM = 128
DEC_PPS = 16
DEC_SUB = 4


def _dot(a, b):
    return jnp.dot(a, b, preferred_element_type=F32)


def _dot_nt(a, b):
    return lax.dot_general(a, b, (((1,), (1,)), ((), ())), preferred_element_type=F32)


def _dot_tn(a, b):
    return lax.dot_general(a, b, (((0,), (0,)), ((), ())), preferred_element_type=F32)


def _split3(x):
    hi = x.astype(BF16)
    r1 = x - hi.astype(F32)
    mid = r1.astype(BF16)
    lo = (r1 - mid.astype(F32)).astype(BF16)
    return hi, mid, lo


def _dot_f32_lhs(x, e, terms=3):
    parts = _split3(x)[:terms]
    out = _dot(parts[0], e)
    for part in parts[1:]:
        out = out + _dot(part, e)
    return out


def _dot_f32_rhs(e, x):
    hi, mid, lo = _split3(x)
    return _dot(e, hi) + _dot(e, mid) + _dot(e, lo)


def _rms(x, g):
    return x * lax.rsqrt(jnp.mean(x * x, axis=-1, keepdims=True) + EPS) * g


def _silu(x):
    return x * jax.nn.sigmoid(x)


def _cparams(sem, vmem=VMEM_LIMIT):
    return pltpu.CompilerParams(dimension_semantics=sem, vmem_limit_bytes=vmem)


def _whole():
    return pl.BlockSpec(memory_space=pltpu.VMEM)


def _head_norm(xh, g):
    ss = jnp.sum(xh * xh, axis=-1, keepdims=True)
    return xh * lax.rsqrt(ss * (1.0 / QK_HEAD) + EPS) * g


def _front_kernel(x_ref, cos_ref, sin_ref, gmix_ref, wlat_ref, wz_ref, wxbc_ref, wg_ref,
                  gqa_ref, wqb_ref, gkva_ref, wuk_ref, wuv_ref, gq_ref, gk_ref, qpad_ref, kpad_ref, vpad_ref,
                  q_ref, k_ref, v_ref, c_ref, kr_ref, z_ref, xbc_ref, dt_ref, gates_ref):
    x = x_ref[...]
    ub = _rms(x, gmix_ref[...]).astype(BF16)
    z_ref[...] = _dot(ub, wz_ref[...])
    xbc_ref[...] = _dot(ub, wxbc_ref[...])
    gates_ref[...] = _dot(ub, wg_ref[...])
    lat = _dot(ub, wlat_ref[...])
    q_lat = lat[:, :Q_LORA]
    kv_lat = lat[:, Q_LORA:Q_LORA + KV_LORA]
    o = Q_LORA + KV_LORA
    kr_raw = lat[:, o:o + LANE]
    kr_rot = lat[:, o + LANE:o + 2 * LANE]
    dt_ref[...] = lat[:, o + 2 * LANE:o + 3 * LANE]
    cos = cos_ref[...]
    sin = sin_ref[...]
    c = _rms(kv_lat, gkva_ref[...])
    c_ref[...] = c
    kr = kr_raw * cos + kr_rot * sin
    kr_ref[...] = kr
    qn = _rms(q_lat, gqa_ref[...]).astype(BF16)
    q2 = _dot(qn, wqb_ref[...])
    nq = H_A * HEAD_PAD
    gq = gq_ref[...]
    qpad = qpad_ref[...]
    kpad = kpad_ref[...]
    for h in range(H_A):
        lo, hi = h * HEAD_PAD, (h + 1) * HEAD_PAD
        qh = q2[:, lo:hi] * cos + q2[:, nq + lo:nq + hi] * sin
        q_ref[:, lo:hi] = (_head_norm(qh, gq) + qpad).astype(q_ref.dtype)
    cb = c.astype(BF16)
    kn = _dot(cb, wuk_ref[...])
    gk = gk_ref[...]
    for h in range(H_A):
        lo, hi = h * HEAD_PAD, (h + 1) * HEAD_PAD
        k_ref[:, lo:hi] = (_head_norm(kn[:, lo:hi] + kr, gk) + kpad).astype(k_ref.dtype)
    v_ref[...] = (_dot(cb, wuv_ref[...]) + vpad_ref[...]).astype(v_ref.dtype)


def _front(x, cos, sin, fw, qpad, kpad, *, tm, q_dtype, pos_blocks):
    t = x.shape[0]
    nt = t // tm
    row = lambda w: pl.BlockSpec((tm, w), lambda i: (i, 0))
    pos_spec = pl.BlockSpec((tm, LANE), lambda i: (i % pos_blocks, 0))
    nq = H_A * HEAD_PAD
    out_shape = (
        jax.ShapeDtypeStruct((t, nq), q_dtype),
        jax.ShapeDtypeStruct((t, nq), BF16),
        jax.ShapeDtypeStruct((t, nq), BF16),
        jax.ShapeDtypeStruct((t, KV_LORA), F32),
        jax.ShapeDtypeStruct((t, LANE), F32),
        jax.ShapeDtypeStruct((t, D_INNER), F32),
        jax.ShapeDtypeStruct((t, CONV_DIM), F32),
        jax.ShapeDtypeStruct((t, LANE), F32),
        jax.ShapeDtypeStruct((t, 2 * D_MODEL), F32),
    )
    out_specs = (row(nq), row(nq), row(nq), row(KV_LORA), row(LANE), row(D_INNER),
                 row(CONV_DIM), row(LANE), row(2 * D_MODEL))
    weights = (fw['g_mix'], fw['w_lat'], fw['w_z'], fw['w_xbc'], fw['w_g'], fw['g_q_a'], fw['w_qb'],
               fw['g_kv_a'], fw['w_uk_pad'], fw['w_uv_ext'], fw['g_q128'], fw['g_k128'], qpad, kpad, fw['v_ones'])
    return pl.pallas_call(
        _front_kernel,
        grid=(nt,),
        in_specs=[row(D_MODEL), pos_spec, pos_spec] + [_whole()] * len(weights),
        out_specs=out_specs,
        out_shape=out_shape,
        compiler_params=_cparams(("parallel",)),
        name="front",
    )(x, cos, sin, *weights)


def _attn_kernel(qi_ref, ki_ref, q_ref, k_ref, v_ref, o_ref, acc_sc, *m_scratch, bounded):
    step = pl.program_id(2)
    qi = qi_ref[step]
    ki = ki_ref[step]
    tq = q_ref.shape[0]
    half = tq // 2

    @pl.when(ki == 0)
    def _():
        acc_sc[...] = jnp.zeros(acc_sc.shape, F32)
        if not bounded:
            m_scratch[0][...] = jnp.full(m_scratch[0].shape, -jnp.inf, F32)

    def update(q0, qn, k0, kn, masked):
        for h in range(2):
            cols = slice(h * HEAD_PAD, (h + 1) * HEAD_PAD)
            s = _dot_nt(q_ref[q0:q0 + qn, cols], k_ref[k0:k0 + kn, cols])
            v = v_ref[k0:k0 + kn, cols]
            if masked:
                keep = (lax.broadcasted_iota(jnp.int32, (qn, kn), 1) <= lax.broadcasted_iota(jnp.int32, (qn, kn), 0))
            if bounded:
                p = jnp.exp2(s)
                if masked:
                    p = jnp.where(keep, p, 0.0)
                acc_sc[h, q0:q0 + qn, :] += _dot(p.astype(BF16), v)
            else:
                m_sc = m_scratch[0]
                if masked:
                    s = jnp.where(keep, s, -jnp.inf)
                m_prev = m_sc[h, q0:q0 + qn, :]
                m_next = jnp.maximum(m_prev, jnp.max(s, axis=-1, keepdims=True))
                alpha = jnp.exp2(m_prev - m_next)
                p = jnp.exp2(s - m_next[:, :1])
                acc_sc[h, q0:q0 + qn, :] = alpha * acc_sc[h, q0:q0 + qn, :] + _dot(p.astype(BF16), v)
                m_sc[h, q0:q0 + qn, :] = m_next

    @pl.when(ki < qi)
    def _():
        update(0, tq, 0, tq, False)

    @pl.when(ki == qi)
    def _():
        update(0, half, 0, half, True)
        update(half, half, 0, half, False)
        update(half, half, half, half, True)
        lane = lax.broadcasted_iota(jnp.int32, (tq, LANE), 1)
        a0 = acc_sc[0]
        a1 = acc_sc[1]
        o0 = a0 / a0[:, V_HEAD:V_HEAD + 1]
        o1 = a1 / a1[:, 0:1]
        o_ref[...] = jnp.where(lane < V_HEAD, o0, o1).astype(o_ref.dtype)


def _prompt_attention(q, k, v, b, s, *, bounded):
    tq = min(ATTN_TQ, s)
    nq = s // tq
    pairs = [(i, j) for i in range(nq) for j in range(i + 1)]
    qi_tab = jnp.asarray([p[0] for p in pairs], jnp.int32)
    ki_tab = jnp.asarray([p[1] for p in pairs], jnp.int32)
    blk = lambda sel: pl.BlockSpec((None, tq, 2 * HEAD_PAD), lambda bi, h, t, qt, kt: (bi, sel(qt, kt)[t], h))
    scratch = [pltpu.VMEM((2, tq, LANE), F32)]
    if not bounded:
        scratch.append(pltpu.VMEM((2, tq, LANE), F32))
    grid_spec = pltpu.PrefetchScalarGridSpec(
        num_scalar_prefetch=2,
        grid=(b, H_A // 2, len(pairs)),
        in_specs=[blk(lambda qt, kt: qt), blk(lambda qt, kt: kt), blk(lambda qt, kt: kt)],
        out_specs=pl.BlockSpec((None, tq, 2 * V_HEAD), lambda bi, h, t, qt, kt: (bi, qt[t], h)),
        scratch_shapes=scratch,
    )
    return pl.pallas_call(
        functools.partial(_attn_kernel, bounded=bounded),
        grid_spec=grid_spec,
        out_shape=jax.ShapeDtypeStruct((b, s, H_A * V_HEAD), BF16),
        compiler_params=_cparams(("parallel", "parallel", "arbitrary")),
        name="prompt_attn_bounded" if bounded else "prompt_attn",
    )(qi_tab, ki_tab, q, k, v)


def _ssd_kernel(xbc_ref, z_ref, dt_ref, wconv_ref, bconv_ref, dtb_ref, ah_ref, dskip_ref, gn_ref,
                tri_ref, exp_ref, y_ref, hfin_ref, xbuf, state):
    ci = pl.program_id(1)
    nc = pl.num_programs(1)
    L = SSD_CHUNK

    @pl.when(ci == 0)
    def _():
        xbuf[0:8, :] = jnp.zeros((8, CONV_DIM), F32)
        state[...] = jnp.zeros(state.shape, F32)

    xbuf[8:8 + L, :] = xbc_ref[...]
    conv = bconv_ref[...] + xbuf[8:8 + L, :] * wconv_ref[3:4, :]
    for w in range(CONV_W - 1):
        sh = CONV_W - 1 - w
        conv = conv + xbuf[8 - sh:8 - sh + L, :] * wconv_ref[w:w + 1, :]
    xbuf[0:8, :] = xbuf[L:L + 8, :]
    xc = _silu(conv)
    xs = xc[:, :D_INNER]

    dt = jax.nn.softplus(dt_ref[...] + dtb_ref[...])
    a = dt * ah_ref[...]
    a_cum = _dot_f32_rhs(tri_ref[...], a)
    a_cum_t = a_cum.T
    a_last = a_cum[L - 1:L, :]
    ex = exp_ref[...]
    dt_x = _dot_f32_lhs(dt, ex, terms=2)
    dfs_x = _dot_f32_lhs(jnp.exp(a_cum), ex, terms=2)
    dte_x = _dot_f32_lhs(jnp.exp(a_last - a_cum), ex, terms=2)
    xdt = xs * dt_x
    xdt_b = xdt.astype(BF16)
    xw_b = (xdt * dte_x).astype(BF16)
    chunk_decay = jnp.exp(a_cum_t[:, L - 1:L])

    row = lax.broadcasted_iota(jnp.int32, (L, L), 0)
    col = lax.broadcasted_iota(jnp.int32, (L, L), 1)
    causal = col <= row
    lane = lax.broadcasted_iota(jnp.int32, (L, LANE), 1)
    for g in range(N_GROUPS):
        bm = xc[:, D_INNER + g * D_STATE:D_INNER + (g + 1) * D_STATE].astype(BF16)
        cm = xc[:, D_INNER + (N_GROUPS + g) * D_STATE:D_INNER + (N_GROUPS + g + 1) * D_STATE].astype(BF16)
        cb = _dot_nt(cm, bm)
        c0, c1 = g * GROUP_W, (g + 1) * GROUP_W
        st_prev = state[c0:c1, :]
        y_off = _dot_nt(cm, st_prev.astype(BF16)) * dfs_x[:, c0:c1]
        for j in range(HEADS_PER_GROUP // 2):
            h0 = g * HEADS_PER_GROUP + 2 * j
            x2 = xdt_b[:, h0 * SSM_HEAD:(h0 + 2) * SSM_HEAD]
            ys = []
            for hh in (h0, h0 + 1):
                seg = a_cum[:, hh:hh + 1] - a_cum_t[hh:hh + 1, :]
                sc = jnp.where(causal, cb * jnp.exp(seg), 0.0)
                ys.append(_dot(sc.astype(BF16), x2))
            y2 = jnp.where(lane < SSM_HEAD, ys[0], ys[1])
            lo = h0 * SSM_HEAD
            y_ref[:, lo:lo + LANE] = y2 + y_off[:, lo - c0:lo - c0 + LANE]
        st_new = _dot_tn(xw_b[:, c0:c1], bm)
        carry = jnp.concatenate(
            [jnp.broadcast_to(chunk_decay[hh:hh + 1, :], (SSM_HEAD, D_STATE))
             for hh in range(g * HEADS_PER_GROUP, (g + 1) * HEADS_PER_GROUP)], axis=0)
        state[c0:c1, :] = st_prev * carry + st_new

    y = y_ref[...] + dskip_ref[...] * xs
    yg = y * _silu(z_ref[...])
    gn = gn_ref[...]
    for g in range(N_GROUPS):
        c0, c1 = g * GROUP_W, (g + 1) * GROUP_W
        blk = yg[:, c0:c1]
        y_ref[:, c0:c1] = blk * lax.rsqrt(jnp.mean(blk * blk, axis=-1, keepdims=True) + EPS) * gn[:, c0:c1]

    @pl.when(ci == nc - 1)
    def _():
        hfin_ref[...] = state[...]


def _ssd_prompt(xbc, z, dt, sw, b, s):
    nc = s // SSD_CHUNK
    row = lambda w: pl.BlockSpec((SSD_CHUNK, w), lambda bi, ci: (bi * nc + ci, 0))
    consts = (sw['w_conv'], sw['b_conv'], sw['dt_bias128'], sw['a_head128'], sw['d_skip_x'], sw['g_norm'],
              sw['tri'], sw['expand'])
    return pl.pallas_call(
        _ssd_kernel,
        grid=(b, nc),
        in_specs=[row(CONV_DIM), row(D_INNER), row(LANE)] + [_whole()] * len(consts),
        out_specs=(row(D_INNER), pl.BlockSpec((None, D_INNER, D_STATE), lambda bi, ci: (bi, 0, 0))),
        out_shape=(jax.ShapeDtypeStruct((b * s, D_INNER), F32),
                   jax.ShapeDtypeStruct((b, D_INNER, D_STATE), F32)),
        scratch_shapes=[pltpu.VMEM((SSD_CHUNK + 8, CONV_DIM), F32), pltpu.VMEM((D_INNER, D_STATE), F32)],
        compiler_params=_cparams(("parallel", "arbitrary")),
        name="ssd_prompt",
    )(xbc, z, dt, *consts)


def _ssm_step_kernel(xbc_ref, cbuf_ref, z_ref, dt_ref, h_ref, wconv_ref, bconv_ref, dtb_ref, ah_ref,
                     dskip_ref, gn_ref, exp_ref, y_ref, hnew_ref):
    conv = bconv_ref[...] + xbc_ref[0] * wconv_ref[3:4, :]
    for w in range(CONV_W - 1):
        conv = conv + cbuf_ref[0, w:w + 1, :] * wconv_ref[w:w + 1, :]
    xc = _silu(conv)
    xs = xc[:, :D_INNER]
    dt = jax.nn.softplus(dt_ref[0] + dtb_ref[...])
    da = jnp.exp(dt * ah_ref[...])
    ex = exp_ref[...]
    da_x = _dot_f32_lhs(jnp.broadcast_to(da, (8, LANE)), ex)[0:1, :]
    xdt = xs * _dot_f32_lhs(jnp.broadcast_to(dt, (8, LANE)), ex)[0:1, :]

    eye = lax.broadcasted_iota(jnp.int32, (LANE, LANE), 0) == lax.broadcasted_iota(jnp.int32, (LANE, LANE), 1)

    def to_col(rowvec):
        return jnp.sum(jnp.where(eye, jnp.broadcast_to(rowvec, (LANE, LANE)), 0.0), axis=-1, keepdims=True)

    ys = []
    for g in range(N_GROUPS):
        bt = xc[:, D_INNER + g * D_STATE:D_INNER + (g + 1) * D_STATE]
        ct = xc[:, D_INNER + (N_GROUPS + g) * D_STATE:D_INNER + (N_GROUPS + g + 1) * D_STATE]
        ct8 = jnp.broadcast_to(ct, (8, D_STATE)).astype(BF16)
        for j in range(GROUP_W // LANE):
            r0 = g * GROUP_W + j * LANE
            h_old = h_ref[0, r0:r0 + LANE, :]
            h_new = h_old * to_col(da_x[:, r0:r0 + LANE]) + to_col(xdt[:, r0:r0 + LANE]) * bt
            hnew_ref[0, r0:r0 + LANE, :] = h_new
            ys.append(_dot_nt(ct8, h_new.astype(BF16))[0:1, :])
    y = jnp.concatenate(ys, axis=-1) + dskip_ref[...] * xs
    yg = y * _silu(z_ref[0])
    gn = gn_ref[...]
    for g in range(N_GROUPS):
        c0, c1 = g * GROUP_W, (g + 1) * GROUP_W
        blk = yg[:, c0:c1]
        y_ref[0, :, c0:c1] = blk * lax.rsqrt(jnp.mean(blk * blk, axis=-1, keepdims=True) + EPS) * gn[:, c0:c1]


def _ssm_sample(xbc, conv_buf, z, dt, h, sw):
    bd = xbc.shape[0]
    vec = lambda w: pl.BlockSpec((1, 1, w), lambda i: (i, 0, 0))
    consts = (sw['w_conv'], sw['b_conv'], sw['dt_bias128'], sw['a_head128'], sw['d_skip_x'], sw['g_norm'],
              sw['expand'])
    y, h_new = pl.pallas_call(
        _ssm_step_kernel,
        grid=(bd,),
        in_specs=[vec(CONV_DIM), pl.BlockSpec((1, CONV_W - 1, CONV_DIM), lambda i: (i, 0, 0)), vec(D_INNER),
                  vec(LANE), pl.BlockSpec((1, D_INNER, D_STATE), lambda i: (i, 0, 0))] + [_whole()] * len(consts),
        out_specs=(vec(D_INNER), pl.BlockSpec((1, D_INNER, D_STATE), lambda i: (i, 0, 0))),
        out_shape=(jax.ShapeDtypeStruct((bd, 1, D_INNER), F32),
                   jax.ShapeDtypeStruct((bd, D_INNER, D_STATE), F32)),
        compiler_params=_cparams(("parallel",)),
        name="ssm_sample",
    )(xbc.reshape(bd, 1, CONV_DIM), conv_buf, z.reshape(bd, 1, D_INNER), dt.reshape(bd, 1, LANE), h, *consts)
    return y.reshape(bd, D_INNER), h_new


def _decode_kernel(pt_ref, qp_ref, qr_ref, gkp_ref, gkr_ref, wuk_ref, segt_ref, ones_ref, cache_c, cache_kr,
                   acc_ref, m_ref, l_ref, cpage, krpage, cb_sc, pcat_sc, krb_sc, kr2b_sc, sem_c, sem_kr):
    b = pl.program_id(0)
    nb = pl.num_programs(0)
    n_chunks = pt_ref.shape[1] // DEC_PPS
    page = cpage.shape[2]
    sub_rows = DEC_SUB * page
    n_sub = DEC_PPS // DEC_SUB
    nj = (H_A * QK_NOPE) // LANE

    def page_copies(pid, slot, i):
        return (pltpu.make_async_copy(cache_c.at[pid], cpage.at[slot, i], sem_c.at[slot]),
                pltpu.make_async_copy(cache_kr.at[pid], krpage.at[slot, i], sem_kr.at[slot]))

    def issue(bb, c, slot):
        for i in range(DEC_PPS):
            for cp in page_copies(pt_ref[bb, c * DEC_PPS + i], slot, i):
                cp.start()

    def wait(slot):
        for i in range(DEC_PPS):
            for cp in page_copies(0, slot, i):
                cp.wait()

    @pl.when(b == 0)
    def _():
        cb_sc[1] = jnp.zeros(cb_sc.shape[1:], BF16)
        pcat_sc[1] = jnp.zeros(pcat_sc.shape[1:], BF16)
        krb_sc[1] = jnp.zeros(krb_sc.shape[1:], BF16)
        kr2b_sc[1] = jnp.zeros(kr2b_sc.shape[1:], BF16)
        issue(0, 0, 0)

    qg = qp_ref[0] * gkp_ref[...]
    qr = (qr_ref[0] * gkr_ref[...]).astype(BF16)
    col_head = (lax.broadcasted_iota(jnp.int32, (H_A, H_A * QK_NOPE), 1) % LANE) // (LANE // H_A)
    q_rows = jnp.where(col_head == lax.broadcasted_iota(jnp.int32, (H_A, H_A * QK_NOPE), 0), qg, 0.0)
    q_hi = q_rows.astype(BF16)
    q_lo = (q_rows - q_hi.astype(F32)).astype(BF16)
    q_abs = _dot_nt(q_hi, wuk_ref[...]) + _dot_nt(q_lo, wuk_ref[...])
    qa_hi = q_abs.astype(BF16)
    qa_lo = (q_abs - qa_hi.astype(F32)).astype(BF16)
    qa2 = jnp.concatenate([qa_hi, qa_lo], axis=0)

    def keys_partial(slot, sub):
        cb = jnp.concatenate([cpage[slot, sub * DEC_SUB + t].astype(BF16) for t in range(DEC_SUB)], axis=0)
        r0 = sub * sub_rows
        cb_sc[slot, r0:r0 + sub_rows, :] = cb
        kk = _dot(cb, wuk_ref[...])
        p_sq = None
        for jj in range(nj):
            blk = kk[:, jj * LANE:(jj + 1) * LANE]
            sq = blk * blk
            p_sq = sq if p_sq is None else p_sq + sq
        pcat_sc[slot, r0:r0 + sub_rows, :] = p_sq.astype(BF16)

    def rope_keys(slot):
        krt = jnp.concatenate([krpage[slot, i] for i in range(DEC_PPS)], axis=1)
        krb_sc[slot] = krt.astype(BF16)
        kr2b_sc[slot] = (krt * krt).astype(BF16)

    def scores(slot):
        s2 = _dot_nt(qa2, cb_sc[slot])
        ns = _dot_nt(segt_ref[...], pcat_sc[slot])
        ns = ns + _dot(qr, krb_sc[slot]) + _dot(ones_ref[...], kr2b_sc[slot])
        s = s2[0:H_A, :] + s2[H_A:2 * H_A, :] + ns[H_A:2 * H_A, :]
        return s * lax.rsqrt(ns[0:H_A, :] * (1.0 / QK_HEAD) + EPS)

    def softmax_step(s, m_prev, l_prev):
        m_next = jnp.maximum(m_prev, jnp.max(s, axis=-1, keepdims=True))
        alpha = jnp.exp2(m_prev - m_next)
        p = jnp.exp2(s - m_next)
        return p, alpha, m_next, alpha * l_prev + jnp.sum(p, axis=-1, keepdims=True)

    def chunk_step(c, slot, carry):
        m_prev, l_prev, acc_prev = carry
        wait(slot)
        last = c + 1 == n_chunks
        nxt_b = jnp.where(last, b + 1, b)
        nxt_c = jnp.where(last, 0, c + 1)

        @pl.when(nxt_b < nb)
        def _():
            issue(nxt_b, nxt_c, 1 - slot)

        valid = c > 0
        prev = 1 - slot
        keys_partial(slot, 0)
        s = scores(prev)
        keys_partial(slot, 1)
        p, alpha, m_next, l_next = softmax_step(s, m_prev, l_prev)
        keys_partial(slot, 2)
        acc_next = acc_prev * alpha + _dot(p.astype(BF16), cb_sc[prev])
        for sub in range(3, n_sub):
            keys_partial(slot, sub)
        rope_keys(slot)
        return (jnp.where(valid, m_next, m_prev), jnp.where(valid, l_next, l_prev),
                jnp.where(valid, acc_next, acc_prev))

    def pair_step(i, carry):
        carry = chunk_step(2 * i, 0, carry)
        return chunk_step(2 * i + 1, 1, carry)

    init = (jnp.full((H_A, 1), -jnp.inf, F32), jnp.zeros((H_A, 1), F32), jnp.zeros((H_A, KV_LORA), F32))
    m_run, l_run, acc = lax.fori_loop(0, n_chunks // 2, pair_step, init)
    p, alpha, m_run, l_run = softmax_step(scores(1), m_run, l_run)
    acc_ref[0] = acc * alpha + _dot(p.astype(BF16), cb_sc[1])
    m_ref[0] = jnp.broadcast_to(m_run, (H_A, LANE))
    l_ref[0] = jnp.broadcast_to(l_run, (H_A, LANE))


def _decode_attention(page_table, q_perm, qr_mat, cache_c, cache_krt, dw):
    bd, n_pages = page_table.shape
    page = cache_c.shape[1]
    rows = DEC_PPS * page
    assert n_pages % (2 * DEC_PPS) == 0 and DEC_PPS // DEC_SUB >= 3
    const = lambda shp: pl.BlockSpec(shp, lambda b, pt: (0,) * len(shp))
    in_specs = [pl.BlockSpec((1, 1, H_A * QK_NOPE), lambda b, pt: (b, 0, 0)),
                pl.BlockSpec((1, 2 * H_A, QK_ROPE), lambda b, pt: (b, 0, 0)),
                const((1, H_A * QK_NOPE)), const((1, QK_ROPE)), const((KV_LORA, H_A * QK_NOPE)),
                const((2 * H_A, LANE)), const((2 * H_A, QK_ROPE)),
                pl.BlockSpec(memory_space=pl.ANY), pl.BlockSpec(memory_space=pl.ANY)]
    out_b = lambda w: pl.BlockSpec((1, H_A, w), lambda b, pt: (b, 0, 0))
    grid_spec = pltpu.PrefetchScalarGridSpec(
        num_scalar_prefetch=1,
        grid=(bd,),
        in_specs=in_specs,
        out_specs=(out_b(KV_LORA), out_b(LANE), out_b(LANE)),
        scratch_shapes=[pltpu.VMEM((2, DEC_PPS, page, KV_LORA), F32),
                        pltpu.VMEM((2, DEC_PPS, QK_ROPE, page), F32),
                        pltpu.VMEM((2, rows, KV_LORA), BF16),
                        pltpu.VMEM((2, rows, LANE), BF16),
                        pltpu.VMEM((2, QK_ROPE, rows), BF16),
                        pltpu.VMEM((2, QK_ROPE, rows), BF16),
                        pltpu.SemaphoreType.DMA((2,)), pltpu.SemaphoreType.DMA((2,))],
    )
    return pl.pallas_call(
        _decode_kernel,
        grid_spec=grid_spec,
        out_shape=(jax.ShapeDtypeStruct((bd, H_A, KV_LORA), F32),
                   jax.ShapeDtypeStruct((bd, H_A, LANE), F32),
                   jax.ShapeDtypeStruct((bd, H_A, LANE), F32)),
        compiler_params=_cparams(("arbitrary",)),
        name="decode_attn",
    )(page_table, q_perm.reshape(bd, 1, -1), qr_mat, dw['gk_perm'], dw['gk_rope'], dw['w_uk_perm'], dw['seg_t'],
      dw['ones_rows'], cache_c, cache_krt)


def _decode_final_kernel(q_ref, k_ref, c_ref, m_ref, l_ref, acc_ref, wuv_ref, o_ref):
    c_new = c_ref[...]
    for pair in range(H_A // 2):
        o_pair = None
        for h in (2 * pair, 2 * pair + 1):
            lo, hi = h * HEAD_PAD, (h + 1) * HEAD_PAD
            s_new = jnp.sum(q_ref[:, lo:hi] * k_ref[:, lo:hi].astype(F32), axis=-1, keepdims=True)
            m_old = m_ref[:, h:h + 1]
            m_new = jnp.maximum(m_old, s_new)
            a = jnp.exp2(m_old - m_new)
            pn = jnp.exp2(s_new - m_new)
            l_new = l_ref[:, h:h + 1] * a + pn
            ctx = (acc_ref[:, h * KV_LORA:(h + 1) * KV_LORA] * a + pn * c_new) / l_new
            part = _dot(ctx.astype(BF16), wuv_ref[h])
            o_pair = part if o_pair is None else o_pair + part
        o_ref[:, pair * LANE:(pair + 1) * LANE] = o_pair.astype(o_ref.dtype)


def _decode_final(q, k_new, c_new, m, l, acc, dw):
    bd = q.shape[0]
    return pl.pallas_call(
        _decode_final_kernel,
        in_specs=[_whole()] * 7,
        out_specs=_whole(),
        out_shape=jax.ShapeDtypeStruct((bd, H_A * V_HEAD), BF16),
        compiler_params=pltpu.CompilerParams(vmem_limit_bytes=VMEM_LIMIT),
        name="decode_final",
    )(q, k_new, c_new, m, l, acc, dw['w_uv_pair'])


def _merge_kernel(x_ref, attn_ref, ssm_ref, gates_ref, wa_ref, wb_ref, wo_ref, gffn_ref, wr_ref, br_ref,
                  h_ref, hn_ref, logit_ref):
    g = jax.nn.sigmoid(gates_ref[...])
    a = _dot(attn_ref[...], wa_ref[...])
    b = _dot(ssm_ref[...].astype(BF16), wb_ref[...])
    mixed = g[:, :D_MODEL] * a + g[:, D_MODEL:] * b
    h = x_ref[...] + _dot(mixed.astype(BF16), wo_ref[...])
    h_ref[...] = h
    hn = _rms(h, gffn_ref[...])
    hn_ref[...] = hn
    hi, mid, _ = _split3(hn)
    w_hi = wr_ref[0]
    w_lo = wr_ref[1]
    logit_ref[...] = _dot(hi, w_hi) + _dot(mid, w_hi) + _dot(hi, w_lo) + br_ref[...]


def _merge(x, attn, ssm, gates, mw, *, tm):
    t = x.shape[0]
    row = lambda w: pl.BlockSpec((tm, w), lambda i: (i, 0))
    consts = (mw['w_a_out'], mw['w_b_out'], mw['w_out'], mw['g_ffn'], mw['w_router2'], mw['b_router128'])
    return pl.pallas_call(
        _merge_kernel,
        grid=(t // tm,),
        in_specs=[row(D_MODEL), row(H_A * V_HEAD), row(D_INNER), row(2 * D_MODEL)] + [_whole()] * len(consts),
        out_specs=(row(D_MODEL), row(D_MODEL), row(LANE)),
        out_shape=(jax.ShapeDtypeStruct((t, D_MODEL), F32), jax.ShapeDtypeStruct((t, D_MODEL), F32),
                   jax.ShapeDtypeStruct((t, LANE), F32)),
        compiler_params=_cparams(("parallel",)),
        name="merge",
    )(x, attn, ssm, gates, *consts)


def _moe_kernel(be_ref, bv_ref, seg_ref, tok_ref, x_hbm, wgu_ref, bgu_ref, wd_ref, bd_ref, out_ref,
                xbuf, wgu_b, wd_b, sem):
    i = pl.program_id(0)
    n = pl.num_programs(0)
    bm = out_ref.shape[0]

    def row_copy(tok, slot, r):
        return pltpu.make_async_copy(x_hbm.at[pl.ds(tok, 1), :], xbuf.at[slot, pl.ds(r, 1), :], sem.at[slot])

    def issue_loop(blk, slot):
        base = seg_ref[blk]

        def body(r, carry):
            row_copy(tok_ref[base + r], slot, r).start()
            return carry
        lax.fori_loop(0, bm, body, 0, unroll=8)

    def issue_inline(blk, slot):
        base = seg_ref[blk]
        for r in range(bm):
            row_copy(tok_ref[base + r], slot, r).start()

    def wait(slot):
        def body(r, carry):
            row_copy(0, slot, r).wait()
            return carry
        lax.fori_loop(0, bm, body, 0, unroll=8)

    @pl.when((i == 0) & (bv_ref[0] > 0))
    def _():
        issue_loop(0, 0)

    prev = jnp.maximum(i - 1, 0)

    @pl.when((i == 0) | (be_ref[i] != be_ref[prev]))
    def _():
        wgu_b[...] = wgu_ref[0].astype(BF16)
        wd_b[...] = wd_ref[0].astype(BF16)

    def expert_block(prefetch_next):
        slot = i % 2
        wait(slot)
        x = xbuf[slot].astype(BF16)
        if prefetch_next:
            issue_inline(i + 1, 1 - slot)
        gu = _dot(x, wgu_b[...]) + bgu_ref[0]
        gate = jnp.minimum(gu[:, :D_FF], SWIGLU_LIMIT)
        up = jnp.clip(gu[:, D_FF:], -SWIGLU_LIMIT, SWIGLU_LIMIT)
        act = (up + 1.0) * gate * jax.nn.sigmoid(SWIGLU_ALPHA * gate)
        out_ref[...] = _dot(act.astype(BF16), wd_b[...]) + bd_ref[0]

    nxt = jnp.minimum(i + 1, n - 1)
    valid = bv_ref[i] > 0
    nxt_valid = (i + 1 < n) & (bv_ref[nxt] > 0)

    @pl.when(valid & nxt_valid)
    def _():
        expert_block(True)

    @pl.when(valid & jnp.logical_not(nxt_valid))
    def _():
        expert_block(False)

    @pl.when(jnp.logical_not(valid))
    def _():
        out_ref[...] = jnp.zeros(out_ref.shape, F32)


def _moe_experts(block_e, block_valid, seg_start, tok_sorted, x, w_gate_up, b_gate_up, w_down, b_down):
    n_blocks = block_e.shape[0]
    bm = MOE_BM
    grid_spec = pltpu.PrefetchScalarGridSpec(
        num_scalar_prefetch=4,
        grid=(n_blocks,),
        in_specs=[
            pl.BlockSpec(memory_space=pl.ANY),
            pl.BlockSpec((1, D_MODEL, 2 * D_FF), lambda i, be, bv, sg, tk: (be[i], 0, 0)),
            pl.BlockSpec((1, 1, 2 * D_FF), lambda i, be, bv, sg, tk: (be[i], 0, 0)),
            pl.BlockSpec((1, D_FF, D_MODEL), lambda i, be, bv, sg, tk: (be[i], 0, 0)),
            pl.BlockSpec((1, 1, D_MODEL), lambda i, be, bv, sg, tk: (be[i], 0, 0)),
        ],
        out_specs=pl.BlockSpec((bm, D_MODEL), lambda i, be, bv, sg, tk: (i, 0)),
        scratch_shapes=[pltpu.VMEM((2, bm, D_MODEL), F32), pltpu.VMEM((D_MODEL, 2 * D_FF), BF16),
                        pltpu.VMEM((D_FF, D_MODEL), BF16), pltpu.SemaphoreType.DMA((2,))],
    )
    return pl.pallas_call(
        _moe_kernel,
        grid_spec=grid_spec,
        out_shape=jax.ShapeDtypeStruct((n_blocks * bm, D_MODEL), F32),
        compiler_params=_cparams(("arbitrary",)),
        name="moe_experts",
    )(block_e, block_valid, seg_start, tok_sorted, x, w_gate_up, b_gate_up.reshape(N_EXPERTS, 1, 2 * D_FF), w_down,
      b_down.reshape(N_EXPERTS, 1, D_MODEL))


def _ple_kernel(pos_ref, y_hbm, h_ref, gw_ref, p_ref, gin_ref, wgate_ref, wple_ref, gple_ref, o_ref,
                gbuf, sem):
    i = pl.program_id(0)
    n = pl.num_programs(0)
    tm = o_ref.shape[0]

    def row_copy(src, slot, k, r):
        return pltpu.make_async_copy(y_hbm.at[pl.ds(src, 1), :], gbuf.at[slot, k, pl.ds(r, 1), :], sem.at[slot])

    def issue_loop(blk, slot):
        def body(r, carry):
            for k in range(TOP_K):
                row_copy(pos_ref[(blk * tm + r) * TOP_K + k], slot, k, r).start()
            return carry
        lax.fori_loop(0, tm, body, 0, unroll=4)

    def issue_inline(blk, slot):
        base = blk * (tm * TOP_K)
        for r in range(tm):
            for k in range(TOP_K):
                row_copy(pos_ref[base + r * TOP_K + k], slot, k, r).start()

    def wait(slot):
        def body(r, carry):
            for k in range(TOP_K):
                row_copy(0, slot, k, r).wait()
            return carry
        lax.fori_loop(0, tm, body, 0, unroll=4)

    @pl.when(i == 0)
    def _():
        issue_loop(0, 0)

    def tile(prefetch_next):
        slot = i % 2
        wait(slot)
        gw = gw_ref[...]
        y = gbuf[slot, 0] * gw[:, 0:1]
        for k in range(1, TOP_K):
            y = y + gbuf[slot, k] * gw[:, k:k + 1]
        if prefetch_next:
            issue_inline(i + 1, 1 - slot)
        h2 = h_ref[...] + y
        gate = jax.nn.sigmoid(_dot(_rms(h2, gin_ref[...]).astype(BF16), wgate_ref[...]))
        ple = _rms(_dot(p_ref[...].astype(BF16), wple_ref[...]), gple_ref[...]) * gate
        o_ref[...] = h2 + ple

    @pl.when(i + 1 < n)
    def _():
        tile(True)

    @pl.when(i + 1 == n)
    def _():
        tile(False)


def _combine_ple(pos, y_sorted, h, gate_w, p, pw):
    t = h.shape[0]
    tm = PLE_TM
    row = lambda w: pl.BlockSpec((tm, w), lambda i, ps: (i, 0))
    whole = pl.BlockSpec(memory_space=pltpu.VMEM)
    grid_spec = pltpu.PrefetchScalarGridSpec(
        num_scalar_prefetch=1,
        grid=(t // tm,),
        in_specs=[pl.BlockSpec(memory_space=pl.ANY), row(D_MODEL), row(LANE), row(PLE_DIM),
                  whole, whole, whole, whole],
        out_specs=row(D_MODEL),
        scratch_shapes=[pltpu.VMEM((2, TOP_K, tm, D_MODEL), F32), pltpu.SemaphoreType.DMA((2,))],
    )
    return pl.pallas_call(
        _ple_kernel,
        grid_spec=grid_spec,
        out_shape=jax.ShapeDtypeStruct((t, D_MODEL), F32),
        compiler_params=_cparams(("arbitrary",)),
        name="combine_ple",
    )(pos, y_sorted, h, gate_w, p, pw['g_ple_in'], pw['w_ple_gate'], pw['w_ple'], pw['g_ple'])


def _pad_lanes(x, n):
    return jnp.pad(x, [(0, 0)] * (x.ndim - 1) + [(0, n - x.shape[-1])])


def _rot_cols(w):
    half = QK_ROPE // 2
    return jnp.concatenate([-w[..., half:], w[..., :half]], axis=-1)


def _prep_weights(g_mix_norm, w_in, g_q_a, w_q_b, g_kv_a, w_uk, w_uv, g_q_head, g_k_head, w_a_out, w_conv,
                  b_conv, dt_bias, a_log, d_skip, g_ssm_norm, w_b_out, w_out, g_ffn_norm, w_router, b_router,
                  g_ple_in, w_ple_gate, w_ple, g_ple):
    sizes = (Q_LORA, KV_LORA, QK_ROPE, D_INNER, CONV_DIM, H_B, 2 * D_MODEL)
    offs = np.concatenate([[0], np.cumsum(sizes)])
    wq, wkv, wkr, wz, wxbc, wdt, wg = [w_in[:, int(offs[i]):int(offs[i + 1])] for i in range(7)]
    zc = lambda n: jnp.zeros((D_MODEL, n), F32)
    kr128 = jnp.concatenate([zc(QK_NOPE), wkr, zc(LANE - QK_HEAD)], axis=1)
    krrot = jnp.concatenate([zc(QK_NOPE), _rot_cols(wkr), zc(LANE - QK_HEAD)], axis=1)
    w_lat = jnp.concatenate([wq, wkv, kr128, krrot, _pad_lanes(wdt, LANE)], axis=1).astype(BF16)

    wqb = w_q_b.reshape(Q_LORA, H_A, QK_HEAD)
    nope, rope_w = wqb[..., :QK_NOPE], wqb[..., QK_NOPE:]
    z_nope = jnp.zeros_like(nope)
    q128 = _pad_lanes(jnp.concatenate([nope, rope_w], axis=-1), LANE).reshape(Q_LORA, H_A * LANE)
    qrot = _pad_lanes(jnp.concatenate([z_nope, _rot_cols(rope_w)], axis=-1), LANE).reshape(Q_LORA, H_A * LANE)
    w_qb = jnp.concatenate([q128, qrot], axis=1).astype(BF16)

    wuk3 = w_uk.reshape(KV_LORA, H_A, QK_NOPE)
    w_uk_pad = _pad_lanes(wuk3, LANE).reshape(KV_LORA, H_A * LANE).astype(BF16)
    sub = LANE // H_A
    w_uk_perm = wuk3.reshape(KV_LORA, H_A, QK_NOPE // sub, sub).transpose(0, 2, 1, 3)
    w_uk_perm = w_uk_perm.reshape(KV_LORA, H_A * QK_NOPE).astype(BF16)
    gk_perm = jnp.broadcast_to(g_k_head[:QK_NOPE].reshape(1, QK_NOPE // sub, 1, sub),
                               (1, QK_NOPE // sub, H_A, sub)).reshape(1, H_A * QK_NOPE)

    wuv3 = w_uv.reshape(KV_LORA, H_A, V_HEAD)
    even = (jnp.arange(H_A) % 2 == 0)[None, :, None]
    zv = jnp.zeros_like(wuv3)
    w_uv_ext = jnp.where(even, jnp.concatenate([wuv3, zv], axis=-1), jnp.concatenate([zv, wuv3], axis=-1))
    ones_lane = np.zeros((H_A, LANE), np.float32)
    ones_lane[0::2, V_HEAD] = 1.0
    ones_lane[1::2, 0] = 1.0
    front = dict(
        g_mix=g_mix_norm.reshape(1, -1), w_lat=w_lat, w_z=wz.astype(BF16), w_xbc=wxbc.astype(BF16),
        w_g=wg.astype(BF16), g_q_a=g_q_a.reshape(1, -1), w_qb=w_qb, g_kv_a=g_kv_a.reshape(1, -1),
        w_uk_pad=w_uk_pad, w_uv_ext=w_uv_ext.reshape(KV_LORA, H_A * LANE).astype(BF16),
        v_ones=jnp.asarray(ones_lane.reshape(1, H_A * LANE)),
        g_q128=_pad_lanes(g_q_head.reshape(1, -1), LANE) * Q_SCALE,
        g_k128=_pad_lanes(g_k_head.reshape(1, -1), LANE),
    )

    lane_head = np.arange(LANE) // sub
    seg = (lane_head[None, :] == np.arange(H_A)[:, None]).astype(np.float32)
    seg_t = np.zeros((2 * H_A, LANE), np.float32)
    seg_t[:H_A] = seg
    ones_rows = np.zeros((2 * H_A, QK_ROPE), np.float32)
    ones_rows[:H_A] = 1.0
    w_uv_pair = w_uv_ext
    decode = dict(
        gk_perm=gk_perm, gk_rope=g_k_head[QK_NOPE:].reshape(1, QK_ROPE), w_uk_perm=w_uk_perm, seg_t=jnp.asarray(seg_t, BF16), ones_rows=jnp.asarray(ones_rows, BF16),
        w_uv_pair=w_uv_pair.transpose(1, 0, 2).astype(BF16),
    )

    expand = (np.arange(D_INNER)[None, :] // SSM_HEAD == np.arange(LANE)[:, None]).astype(np.float32)
    tri = (np.arange(SSD_CHUNK)[None, :] <= np.arange(SSD_CHUNK)[:, None]).astype(np.float32)
    ssm = dict(
        w_conv=w_conv, b_conv=b_conv.reshape(1, -1), dt_bias128=_pad_lanes(dt_bias.reshape(1, -1), LANE),
        a_head128=_pad_lanes(-jnp.exp(a_log).reshape(1, -1), LANE),
        d_skip_x=jnp.repeat(d_skip, SSM_HEAD).reshape(1, -1), g_norm=g_ssm_norm.reshape(1, -1),
        tri=jnp.asarray(tri, BF16), expand=jnp.asarray(expand, BF16),
    )

    wr = _pad_lanes(w_router, LANE)
    wr_hi = wr.astype(BF16)
    wr_lo = (wr - wr_hi.astype(F32)).astype(BF16)
    merge = dict(
        w_a_out=w_a_out.astype(BF16), w_b_out=w_b_out.astype(BF16), w_out=w_out.astype(BF16),
        g_ffn=g_ffn_norm.reshape(1, -1), w_router2=jnp.stack([wr_hi, wr_lo]),
        b_router128=_pad_lanes(b_router.reshape(1, -1), LANE),
    )
    ple = dict(g_ple_in=g_ple_in.reshape(1, -1), w_ple_gate=w_ple_gate.astype(BF16), w_ple=w_ple.astype(BF16),
               g_ple=g_ple.reshape(1, -1))
    return front, decode, ssm, merge, ple


def _rope_tables(pos):
    half = QK_ROPE // 2
    inv = ROPE_THETA ** (-jnp.arange(half, dtype=F32) / half)
    ang = pos.astype(F32)[:, None] * inv[None, :]
    cos, sin = jnp.cos(ang), jnp.sin(ang)
    n = pos.shape[0]
    cos128 = jnp.concatenate([jnp.ones((n, QK_NOPE), F32), cos, cos, jnp.ones((n, LANE - QK_HEAD), F32)], axis=1)
    sin128 = jnp.concatenate([jnp.zeros((n, QK_NOPE), F32), sin, sin, jnp.zeros((n, LANE - QK_HEAD), F32)], axis=1)
    return cos128, sin128


def _route(logits, n_tok):
    top_val, top_idx = lax.top_k(logits[:, :N_EXPERTS], TOP_K)
    gate_w = jax.nn.softmax(top_val, axis=-1)
    n_assign = n_tok * TOP_K
    flat_e = top_idx.reshape(-1).astype(jnp.int32)
    onehot = (flat_e[:, None] == jnp.arange(N_EXPERTS, dtype=jnp.int32)[None, :]).astype(jnp.int32)
    csum = jnp.cumsum(onehot, axis=0)
    counts = csum[-1]
    bm = MOE_BM
    padded = (counts + bm - 1) // bm * bm
    pend = jnp.cumsum(padded)
    pstarts = pend - padded
    starts = jnp.cumsum(counts) - counts
    dest = jnp.sum(onehot * (csum - 1 + pstarts[None, :]), axis=1).astype(jnp.int32)
    idx_bits = max(1, (n_assign - 1).bit_length())
    key = jnp.sort(flat_e * (1 << idx_bits) + jnp.arange(n_assign, dtype=jnp.int32))
    tok_sorted = jnp.pad((key & ((1 << idx_bits) - 1)) // TOP_K, (0, bm))
    n_blocks = -(-n_assign // bm) + N_EXPERTS
    blk_start = jnp.arange(n_blocks, dtype=jnp.int32) * bm
    block_e = jnp.minimum(jnp.sum((pend[None, :] <= blk_start[:, None]).astype(jnp.int32), axis=1), N_EXPERTS - 1)
    block_valid = (blk_start < pend[-1]).astype(jnp.int32)
    seg_start = jnp.clip(starts[block_e] + blk_start - pstarts[block_e], 0, n_assign).astype(jnp.int32)
    return gate_w, dest, tok_sorted.astype(jnp.int32), block_e.astype(jnp.int32), block_valid, seg_start


def kernel(x_prompt, x_sample, p_prompt, p_sample, cache_kv_latent, cache_k_rope, page_table, state_conv, state_ssm, g_mix_norm, w_in, g_q_a, w_q_b, g_kv_a, w_uk, w_uv, g_q_head, g_k_head, w_a_out, w_conv, b_conv, dt_bias, a_log, d_skip, g_ssm_norm, w_b_out, w_out, g_ffn_norm, w_router, b_router, w_gate_up, b_gate_up, w_down, b_down, g_ple_in, w_ple_gate, w_ple, g_ple):
    depth = g_mix_norm.shape[0]
    assert depth == 1, "one layer"
    b, s, _ = x_prompt.shape
    bd, sd, _ = x_sample.shape
    assert sd == 1, "one new token per sample sequence"
    n_pages = page_table.shape[1]
    page_size = cache_kv_latent.shape[2]
    assert n_pages % DEC_PPS == 0 and s % SSD_CHUNK == 0
    tp = b * s

    fw, dw, sw, mw, pw = _prep_weights(
        g_mix_norm[0], w_in[0], g_q_a[0], w_q_b[0], g_kv_a[0], w_uk[0], w_uv[0], g_q_head[0], g_k_head[0],
        w_a_out[0], w_conv[0], b_conv[0], dt_bias[0], a_log[0], d_skip[0], g_ssm_norm[0], w_b_out[0], w_out[0],
        g_ffn_norm[0], w_router[0], b_router[0], g_ple_in[0], w_ple_gate[0], w_ple[0], g_ple[0])

    xp = x_prompt.reshape(tp, D_MODEL)
    cos_p, sin_p = _rope_tables(jnp.arange(s))
    tm_p = min(FRONT_TM, s)
    bound = QK_HEAD * jnp.max(jnp.abs(fw['g_q128'])) * jnp.max(jnp.abs(fw['g_k128']))
    bound = (1.02 * bound + 1.0).astype(BF16).astype(F32)
    bias_lane = (jnp.arange(LANE) == QK_HEAD).astype(F32).reshape(1, LANE)
    q_p, k_p, v_p, c_p, kr_p, z_p, xbc_p, dt_p, gates_p = _front(
        xp, cos_p, sin_p, fw, -bound * bias_lane, bias_lane, tm=tm_p, q_dtype=BF16, pos_blocks=s // tm_p)
    qkv = (q_p.reshape(b, s, -1), k_p.reshape(b, s, -1), v_p.reshape(b, s, -1))
    attn_p = lax.cond(bound <= ATTN_MAX_BOUND,
                      lambda q, k, v: _prompt_attention(q, k, v, b, s, bounded=True),
                      lambda q, k, v: _prompt_attention(q, k, v, b, s, bounded=False), *qkv)
    ssm_p, hfin_p = _ssd_prompt(xbc_p, z_p, dt_p, sw, b, s)
    h_p, hn_p, logit_p = _merge(xp, attn_p.reshape(tp, -1), ssm_p, gates_p, mw, tm=min(MERGE_TM, tp))

    xs = x_sample.reshape(bd, D_MODEL)
    cos_s, sin_s = _rope_tables(jnp.full((bd,), n_pages * page_size, jnp.int32))
    no_pad = jnp.zeros((1, LANE), F32)
    q_s, k_s, _, c_s, kr_s, z_s, xbc_s, dt_s, gates_s = _front(
        xs, cos_s, sin_s, fw, no_pad, no_pad, tm=bd, q_dtype=F32, pos_blocks=1)
    q3 = q_s.reshape(bd, H_A, LANE)
    sub = LANE // H_A
    q_perm = q3[:, :, :QK_NOPE].reshape(bd, H_A, QK_NOPE // sub, sub).transpose(0, 2, 1, 3).reshape(bd, -1)
    qr = q3[:, :, QK_NOPE:QK_HEAD]
    qr_mat = jnp.concatenate([jnp.zeros_like(qr), qr], axis=1)
    acc, m_run, l_run = _decode_attention(page_table, q_perm, qr_mat, cache_kv_latent[0],
                                          jnp.swapaxes(cache_k_rope[0], 1, 2), dw)
    attn_s = _decode_final(q_s, k_s, c_s, m_run[:, :, 0], l_run[:, :, 0], acc.reshape(bd, -1), dw)
    ssm_s, hnew_s = _ssm_sample(xbc_s, state_conv[0], z_s, dt_s,
                                state_ssm[0].reshape(bd, D_INNER, D_STATE), sw)
    h_s, hn_s, logit_s = _merge(xs, attn_s, ssm_s, gates_s, mw, tm=bd)

    h_all = jnp.concatenate([h_p, h_s], axis=0)
    hn_all = jnp.concatenate([hn_p, hn_s], axis=0)
    logits = jnp.concatenate([logit_p, logit_s], axis=0)
    p_all = jnp.concatenate([p_prompt[0].reshape(tp, PLE_DIM), p_sample[0].reshape(bd, PLE_DIM)], axis=0)
    n_tok = tp + bd
    gate_w, dest, tok_sorted, block_e, block_valid, seg_start = _route(logits, n_tok)
    y_sorted = _moe_experts(block_e, block_valid, seg_start, tok_sorted, hn_all, w_gate_up[0], b_gate_up[0],
                            w_down[0], b_down[0])
    out_all = _combine_ple(dest, y_sorted, h_all, _pad_lanes(gate_w, LANE), p_all, pw)

    y_prompt = out_all[:tp].reshape(b, s, D_MODEL)
    y_sample = out_all[tp:].reshape(bd, sd, D_MODEL)
    new_c_p = c_p.reshape(1, b, s, KV_LORA)
    new_kr_p = kr_p[:, QK_NOPE:QK_HEAD].reshape(1, b, s, QK_ROPE)
    conv_p = xbc_p.reshape(b, s, CONV_DIM)[:, s - (CONV_W - 1):].reshape(1, b, CONV_W - 1, CONV_DIM)
    ssm_state_p = hfin_p.reshape(1, b, H_B, SSM_HEAD, D_STATE)
    new_c_s = c_s.reshape(1, bd, sd, KV_LORA)
    new_kr_s = kr_s[:, QK_NOPE:QK_HEAD].reshape(1, bd, sd, QK_ROPE)
    conv_s = jnp.concatenate([state_conv[0][:, 1:], xbc_s[:, None, :]], axis=1).reshape(1, bd, CONV_W - 1, CONV_DIM)
    ssm_state_s = hnew_s.reshape(1, bd, H_B, SSM_HEAD, D_STATE)
    return (y_prompt, y_sample, new_c_p, new_kr_p, conv_p, ssm_state_p, new_c_s, new_kr_s, conv_s, ssm_state_s)
```

```python
import functools
import math

import jax
import jax.numpy as jnp
import numpy as np
from jax import lax
from jax.experimental import pallas as pl
from jax.experimental.pallas import tpu as pltpu

F32 = jnp.float32
BF16 = jnp.bfloat16

D_MODEL = 1024
H_A = 16
Q_LORA = 384
KV_LORA = 256
QK_NOPE = 64
QK_ROPE = 32
QK_HEAD = QK_NOPE + QK_ROPE
V_HEAD = 64
ROPE_THETA = 10000.0
D_INNER = 2 * D_MODEL
SSM_HEAD = 64
H_B = D_INNER // SSM_HEAD
N_GROUPS = 4
D_STATE = 128
CONV_W = 4
CONV_DIM = D_INNER + 2 * N_GROUPS * D_STATE
SSD_CHUNK = 128
N_EXPERTS = 32
TOP_K = 4
D_FF = D_MODEL
SWIGLU_LIMIT = 7.0
SWIGLU_ALPHA = 1.702
PLE_DIM = 256
EPS = 1e-6

LANE = 128
HEAD_PAD = LANE
GROUP_W = D_INNER // N_GROUPS
HEADS_PER_GROUP = H_B // N_GROUPS
Q_SCALE = QK_HEAD ** -0.5 * math.log2(math.e)
VMEM_LIMIT = 56 * 1024 * 1024

FRONT_TM = 256
ATTN_TQ = 1024
ATTN_MAX_BOUND = 40.0
MERGE_TM = 256
MOE_BM = 256
PLE_TM = 128
DEC_PPS = 16
DEC_SUB = 4


def _dot(a, b):
    return jnp.dot(a, b, preferred_element_type=F32)


def _dot_nt(a, b):
    return lax.dot_general(a, b, (((1,), (1,)), ((), ())), preferred_element_type=F32)


def _dot_tn(a, b):
    return lax.dot_general(a, b, (((0,), (0,)), ((), ())), preferred_element_type=F32)


def _split3(x):
    hi = x.astype(BF16)
    r1 = x - hi.astype(F32)
    mid = r1.astype(BF16)
    lo = (r1 - mid.astype(F32)).astype(BF16)
    return hi, mid, lo


def _dot_f32_lhs(x, e, terms=3):
    parts = _split3(x)[:terms]
    out = _dot(parts[0], e)
    for part in parts[1:]:
        out = out + _dot(part, e)
    return out


def _dot_f32_rhs(e, x):
    hi, mid, lo = _split3(x)
    return _dot(e, hi) + _dot(e, mid) + _dot(e, lo)


def _rms(x, g):
    return x * lax.rsqrt(jnp.mean(x * x, axis=-1, keepdims=True) + EPS) * g


def _silu(x):
    return x * jax.nn.sigmoid(x)


def _cparams(sem, vmem=VMEM_LIMIT):
    return pltpu.CompilerParams(dimension_semantics=sem, vmem_limit_bytes=vmem)


def _whole():
    return pl.BlockSpec(memory_space=pltpu.VMEM)


def _head_norm(xh, g):
    ss = jnp.sum(xh * xh, axis=-1, keepdims=True)
    return xh * lax.rsqrt(ss * (1.0 / QK_HEAD) + EPS) * g


def _front_kernel(x_ref, cos_ref, sin_ref, gmix_ref, wlat_ref, wz_ref, wxbc_ref, wg_ref,
                  gqa_ref, wqb_ref, gkva_ref, wuk_ref, wuv_ref, gq_ref, gk_ref, qpad_ref, kpad_ref, vpad_ref,
                  q_ref, k_ref, v_ref, c_ref, kr_ref, z_ref, xbc_ref, dt_ref, gates_ref):
    x = x_ref[...]
    ub = _rms(x, gmix_ref[...]).astype(BF16)
    z_ref[...] = _dot(ub, wz_ref[...])
    xbc_ref[...] = _dot(ub, wxbc_ref[...])
    gates_ref[...] = _dot(ub, wg_ref[...])
    lat = _dot(ub, wlat_ref[...])
    q_lat = lat[:, :Q_LORA]
    kv_lat = lat[:, Q_LORA:Q_LORA + KV_LORA]
    o = Q_LORA + KV_LORA
    kr_raw = lat[:, o:o + LANE]
    kr_rot = lat[:, o + LANE:o + 2 * LANE]
    dt_ref[...] = lat[:, o + 2 * LANE:o + 3 * LANE]
    cos = cos_ref[...]
    sin = sin_ref[...]
    c = _rms(kv_lat, gkva_ref[...])
    c_ref[...] = c
    kr = kr_raw * cos + kr_rot * sin
    kr_ref[...] = kr
    qn = _rms(q_lat, gqa_ref[...]).astype(BF16)
    q2 = _dot(qn, wqb_ref[...])
    nq = H_A * HEAD_PAD
    gq = gq_ref[...]
    qpad = qpad_ref[...]
    kpad = kpad_ref[...]
    for h in range(H_A):
        lo, hi = h * HEAD_PAD, (h + 1) * HEAD_PAD
        qh = q2[:, lo:hi] * cos + q2[:, nq + lo:nq + hi] * sin
        q_ref[:, lo:hi] = (_head_norm(qh, gq) + qpad).astype(q_ref.dtype)
    cb = c.astype(BF16)
    kn = _dot(cb, wuk_ref[...])
    gk = gk_ref[...]
    for h in range(H_A):
        lo, hi = h * HEAD_PAD, (h + 1) * HEAD_PAD
        k_ref[:, lo:hi] = (_head_norm(kn[:, lo:hi] + kr, gk) + kpad).astype(k_ref.dtype)
    v_ref[...] = (_dot(cb, wuv_ref[...]) + vpad_ref[...]).astype(v_ref.dtype)


def _front(x, cos, sin, fw, qpad, kpad, *, tm, q_dtype, pos_blocks):
    t = x.shape[0]
    nt = t // tm
    row = lambda w: pl.BlockSpec((tm, w), lambda i: (i, 0))
    pos_spec = pl.BlockSpec((tm, LANE), lambda i: (i % pos_blocks, 0))
    nq = H_A * HEAD_PAD
    out_shape = (
        jax.ShapeDtypeStruct((t, nq), q_dtype),
        jax.ShapeDtypeStruct((t, nq), BF16),
        jax.ShapeDtypeStruct((t, nq), BF16),
        jax.ShapeDtypeStruct((t, KV_LORA), F32),
        jax.ShapeDtypeStruct((t, LANE), F32),
        jax.ShapeDtypeStruct((t, D_INNER), F32),
        jax.ShapeDtypeStruct((t, CONV_DIM), F32),
        jax.ShapeDtypeStruct((t, LANE), F32),
        jax.ShapeDtypeStruct((t, 2 * D_MODEL), F32),
    )
    out_specs = (row(nq), row(nq), row(nq), row(KV_LORA), row(LANE), row(D_INNER),
                 row(CONV_DIM), row(LANE), row(2 * D_MODEL))
    weights = (fw['g_mix'], fw['w_lat'], fw['w_z'], fw['w_xbc'], fw['w_g'], fw['g_q_a'], fw['w_qb'],
               fw['g_kv_a'], fw['w_uk_pad'], fw['w_uv_ext'], fw['g_q128'], fw['g_k128'], qpad, kpad, fw['v_ones'])
    return pl.pallas_call(
        _front_kernel,
        grid=(nt,),
        in_specs=[row(D_MODEL), pos_spec, pos_spec] + [_whole()] * len(weights),
        out_specs=out_specs,
        out_shape=out_shape,
        compiler_params=_cparams(("parallel",)),
        name="front",
    )(x, cos, sin, *weights)


def _attn_kernel(qi_ref, ki_ref, q_ref, k_ref, v_ref, o_ref, acc_sc, *m_scratch, bounded):
    step = pl.program_id(2)
    qi = qi_ref[step]
    ki = ki_ref[step]
    tq = q_ref.shape[0]
    half = tq // 2

    @pl.when(ki == 0)
    def _():
        acc_sc[...] = jnp.zeros(acc_sc.shape, F32)
        if not bounded:
            m_scratch[0][...] = jnp.full(m_scratch[0].shape, -jnp.inf, F32)

    def update(q0, qn, k0, kn, masked):
        for h in range(2):
            cols = slice(h * HEAD_PAD, (h + 1) * HEAD_PAD)
            s = _dot_nt(q_ref[q0:q0 + qn, cols], k_ref[k0:k0 + kn, cols])
            v = v_ref[k0:k0 + kn, cols]
            if masked:
                keep = (lax.broadcasted_iota(jnp.int32, (qn, kn), 1) <= lax.broadcasted_iota(jnp.int32, (qn, kn), 0))
            if bounded:
                p = jnp.exp2(s)
                if masked:
                    p = jnp.where(keep, p, 0.0)
                acc_sc[h, q0:q0 + qn, :] += _dot(p.astype(BF16), v)
            else:
                m_sc = m_scratch[0]
                if masked:
                    s = jnp.where(keep, s, -jnp.inf)
                m_prev = m_sc[h, q0:q0 + qn, :]
                m_next = jnp.maximum(m_prev, jnp.max(s, axis=-1, keepdims=True))
                alpha = jnp.exp2(m_prev - m_next)
                p = jnp.exp2(s - m_next[:, :1])
                acc_sc[h, q0:q0 + qn, :] = alpha * acc_sc[h, q0:q0 + qn, :] + _dot(p.astype(BF16), v)
                m_sc[h, q0:q0 + qn, :] = m_next

    @pl.when(ki < qi)
    def _():
        update(0, tq, 0, tq, False)

    @pl.when(ki == qi)
    def _():
        update(0, half, 0, half, True)
        update(half, half, 0, half, False)
        update(half, half, half, half, True)
        lane = lax.broadcasted_iota(jnp.int32, (tq, LANE), 1)
        a0 = acc_sc[0]
        a1 = acc_sc[1]
        o0 = a0 / a0[:, V_HEAD:V_HEAD + 1]
        o1 = a1 / a1[:, 0:1]
        o_ref[...] = jnp.where(lane < V_HEAD, o0, o1).astype(o_ref.dtype)


def _prompt_attention(q, k, v, b, s, *, bounded):
    tq = min(ATTN_TQ, s)
    nq = s // tq
    pairs = [(i, j) for i in range(nq) for j in range(i + 1)]
    qi_tab = jnp.asarray([p[0] for p in pairs], jnp.int32)
    ki_tab = jnp.asarray([p[1] for p in pairs], jnp.int32)
    blk = lambda sel: pl.BlockSpec((None, tq, 2 * HEAD_PAD), lambda bi, h, t, qt, kt: (bi, sel(qt, kt)[t], h))
    scratch = [pltpu.VMEM((2, tq, LANE), F32)]
    if not bounded:
        scratch.append(pltpu.VMEM((2, tq, LANE), F32))
    grid_spec = pltpu.PrefetchScalarGridSpec(
        num_scalar_prefetch=2,
        grid=(b, H_A // 2, len(pairs)),
        in_specs=[blk(lambda qt, kt: qt), blk(lambda qt, kt: kt), blk(lambda qt, kt: kt)],
        out_specs=pl.BlockSpec((None, tq, 2 * V_HEAD), lambda bi, h, t, qt, kt: (bi, qt[t], h)),
        scratch_shapes=scratch,
    )
    return pl.pallas_call(
        functools.partial(_attn_kernel, bounded=bounded),
        grid_spec=grid_spec,
        out_shape=jax.ShapeDtypeStruct((b, s, H_A * V_HEAD), BF16),
        compiler_params=_cparams(("parallel", "parallel", "arbitrary")),
        name="prompt_attn_bounded" if bounded else "prompt_attn",
    )(qi_tab, ki_tab, q, k, v)


def _ssd_kernel(xbc_ref, z_ref, dt_ref, wconv_ref, bconv_ref, dtb_ref, ah_ref, dskip_ref, gn_ref,
                tri_ref, exp_ref, y_ref, hfin_ref, xbuf, state):
    ci = pl.program_id(1)
    nc = pl.num_programs(1)
    L = SSD_CHUNK

    @pl.when(ci == 0)
    def _():
        xbuf[0:8, :] = jnp.zeros((8, CONV_DIM), F32)
        state[...] = jnp.zeros(state.shape, F32)

    xbuf[8:8 + L, :] = xbc_ref[...]
    conv = bconv_ref[...] + xbuf[8:8 + L, :] * wconv_ref[3:4, :]
    for w in range(CONV_W - 1):
        sh = CONV_W - 1 - w
        conv = conv + xbuf[8 - sh:8 - sh + L, :] * wconv_ref[w:w + 1, :]
    xbuf[0:8, :] = xbuf[L:L + 8, :]
    xc = _silu(conv)
    xs = xc[:, :D_INNER]

    dt = jax.nn.softplus(dt_ref[...] + dtb_ref[...])
    a = dt * ah_ref[...]
    a_cum = _dot_f32_rhs(tri_ref[...], a)
    a_cum_t = a_cum.T
    a_last = a_cum[L - 1:L, :]
    ex = exp_ref[...]
    dt_x = _dot_f32_lhs(dt, ex, terms=2)
    dfs_x = _dot_f32_lhs(jnp.exp(a_cum), ex, terms=2)
    dte_x = _dot_f32_lhs(jnp.exp(a_last - a_cum), ex, terms=2)
    xdt = xs * dt_x
    xdt_b = xdt.astype(BF16)
    xw_b = (xdt * dte_x).astype(BF16)
    chunk_decay = jnp.exp(a_cum_t[:, L - 1:L])

    row = lax.broadcasted_iota(jnp.int32, (L, L), 0)
    col = lax.broadcasted_iota(jnp.int32, (L, L), 1)
    causal = col <= row
    lane = lax.broadcasted_iota(jnp.int32, (L, LANE), 1)
    for g in range(N_GROUPS):
        bm = xc[:, D_INNER + g * D_STATE:D_INNER + (g + 1) * D_STATE].astype(BF16)
        cm = xc[:, D_INNER + (N_GROUPS + g) * D_STATE:D_INNER + (N_GROUPS + g + 1) * D_STATE].astype(BF16)
        cb = _dot_nt(cm, bm)
        c0, c1 = g * GROUP_W, (g + 1) * GROUP_W
        st_prev = state[c0:c1, :]
        y_off = _dot_nt(cm, st_prev.astype(BF16)) * dfs_x[:, c0:c1]
        for j in range(HEADS_PER_GROUP // 2):
            h0 = g * HEADS_PER_GROUP + 2 * j
            x2 = xdt_b[:, h0 * SSM_HEAD:(h0 + 2) * SSM_HEAD]
            ys = []
            for hh in (h0, h0 + 1):
                seg = a_cum[:, hh:hh + 1] - a_cum_t[hh:hh + 1, :]
                sc = jnp.where(causal, cb * jnp.exp(seg), 0.0)
                ys.append(_dot(sc.astype(BF16), x2))
            y2 = jnp.where(lane < SSM_HEAD, ys[0], ys[1])
            lo = h0 * SSM_HEAD
            y_ref[:, lo:lo + LANE] = y2 + y_off[:, lo - c0:lo - c0 + LANE]
        st_new = _dot_tn(xw_b[:, c0:c1], bm)
        carry = jnp.concatenate(
            [jnp.broadcast_to(chunk_decay[hh:hh + 1, :], (SSM_HEAD, D_STATE))
             for hh in range(g * HEADS_PER_GROUP, (g + 1) * HEADS_PER_GROUP)], axis=0)
        state[c0:c1, :] = st_prev * carry + st_new

    y = y_ref[...] + dskip_ref[...] * xs
    yg = y * _silu(z_ref[...])
    gn = gn_ref[...]
    for g in range(N_GROUPS):
        c0, c1 = g * GROUP_W, (g + 1) * GROUP_W
        blk = yg[:, c0:c1]
        y_ref[:, c0:c1] = blk * lax.rsqrt(jnp.mean(blk * blk, axis=-1, keepdims=True) + EPS) * gn[:, c0:c1]

    @pl.when(ci == nc - 1)
    def _():
        hfin_ref[...] = state[...]


def _ssd_prompt(xbc, z, dt, sw, b, s):
    nc = s // SSD_CHUNK
    row = lambda w: pl.BlockSpec((SSD_CHUNK, w), lambda bi, ci: (bi * nc + ci, 0))
    consts = (sw['w_conv'], sw['b_conv'], sw['dt_bias128'], sw['a_head128'], sw['d_skip_x'], sw['g_norm'],
              sw['tri'], sw['expand'])
    return pl.pallas_call(
        _ssd_kernel,
        grid=(b, nc),
        in_specs=[row(CONV_DIM), row(D_INNER), row(LANE)] + [_whole()] * len(consts),
        out_specs=(row(D_INNER), pl.BlockSpec((None, D_INNER, D_STATE), lambda bi, ci: (bi, 0, 0))),
        out_shape=(jax.ShapeDtypeStruct((b * s, D_INNER), F32),
                   jax.ShapeDtypeStruct((b, D_INNER, D_STATE), F32)),
        scratch_shapes=[pltpu.VMEM((SSD_CHUNK + 8, CONV_DIM), F32), pltpu.VMEM((D_INNER, D_STATE), F32)],
        compiler_params=_cparams(("parallel", "arbitrary")),
        name="ssd_prompt",
    )(xbc, z, dt, *consts)


def _ssm_step_kernel(xbc_ref, cbuf_ref, z_ref, dt_ref, h_ref, wconv_ref, bconv_ref, dtb_ref, ah_ref,
                     dskip_ref, gn_ref, exp_ref, y_ref, hnew_ref):
    conv = bconv_ref[...] + xbc_ref[0] * wconv_ref[3:4, :]
    for w in range(CONV_W - 1):
        conv = conv + cbuf_ref[0, w:w + 1, :] * wconv_ref[w:w + 1, :]
    xc = _silu(conv)
    xs = xc[:, :D_INNER]
    dt = jax.nn.softplus(dt_ref[0] + dtb_ref[...])
    da = jnp.exp(dt * ah_ref[...])
    ex = exp_ref[...]
    da_x = _dot_f32_lhs(jnp.broadcast_to(da, (8, LANE)), ex)[0:1, :]
    xdt = xs * _dot_f32_lhs(jnp.broadcast_to(dt, (8, LANE)), ex)[0:1, :]

    eye = lax.broadcasted_iota(jnp.int32, (LANE, LANE), 0) == lax.broadcasted_iota(jnp.int32, (LANE, LANE), 1)

    def to_col(rowvec):
        return jnp.sum(jnp.where(eye, jnp.broadcast_to(rowvec, (LANE, LANE)), 0.0), axis=-1, keepdims=True)

    ys = []
    for g in range(N_GROUPS):
        bt = xc[:, D_INNER + g * D_STATE:D_INNER + (g + 1) * D_STATE]
        ct = xc[:, D_INNER + (N_GROUPS + g) * D_STATE:D_INNER + (N_GROUPS + g + 1) * D_STATE]
        ct8 = jnp.broadcast_to(ct, (8, D_STATE)).astype(BF16)
        for j in range(GROUP_W // LANE):
            r0 = g * GROUP_W + j * LANE
            h_old = h_ref[0, r0:r0 + LANE, :]
            h_new = h_old * to_col(da_x[:, r0:r0 + LANE]) + to_col(xdt[:, r0:r0 + LANE]) * bt
            hnew_ref[0, r0:r0 + LANE, :] = h_new
            ys.append(_dot_nt(ct8, h_new.astype(BF16))[0:1, :])
    y = jnp.concatenate(ys, axis=-1) + dskip_ref[...] * xs
    yg = y * _silu(z_ref[0])
    gn = gn_ref[...]
    for g in range(N_GROUPS):
        c0, c1 = g * GROUP_W, (g + 1) * GROUP_W
        blk = yg[:, c0:c1]
        y_ref[0, :, c0:c1] = blk * lax.rsqrt(jnp.mean(blk * blk, axis=-1, keepdims=True) + EPS) * gn[:, c0:c1]


def _ssm_sample(xbc, conv_buf, z, dt, h, sw):
    bd = xbc.shape[0]
    vec = lambda w: pl.BlockSpec((1, 1, w), lambda i: (i, 0, 0))
    consts = (sw['w_conv'], sw['b_conv'], sw['dt_bias128'], sw['a_head128'], sw['d_skip_x'], sw['g_norm'],
              sw['expand'])
    y, h_new = pl.pallas_call(
        _ssm_step_kernel,
        grid=(bd,),
        in_specs=[vec(CONV_DIM), pl.BlockSpec((1, CONV_W - 1, CONV_DIM), lambda i: (i, 0, 0)), vec(D_INNER),
                  vec(LANE), pl.BlockSpec((1, D_INNER, D_STATE), lambda i: (i, 0, 0))] + [_whole()] * len(consts),
        out_specs=(vec(D_INNER), pl.BlockSpec((1, D_INNER, D_STATE), lambda i: (i, 0, 0))),
        out_shape=(jax.ShapeDtypeStruct((bd, 1, D_INNER), F32),
                   jax.ShapeDtypeStruct((bd, D_INNER, D_STATE), F32)),
        compiler_params=_cparams(("parallel",)),
        name="ssm_sample",
    )(xbc.reshape(bd, 1, CONV_DIM), conv_buf, z.reshape(bd, 1, D_INNER), dt.reshape(bd, 1, LANE), h, *consts)
    return y.reshape(bd, D_INNER), h_new


def _decode_kernel(pt_ref, qp_ref, qr_ref, gkp_ref, gkr_ref, wuk_ref, segt_ref, ones_ref, cache_c, cache_kr,
                   acc_ref, m_ref, l_ref, cpage, krpage, cb_sc, pcat_sc, krb_sc, kr2b_sc, sem_c, sem_kr):
    b = pl.program_id(0)
    nb = pl.num_programs(0)
    n_chunks = pt_ref.shape[1] // DEC_PPS
    page = cpage.shape[2]
    sub_rows = DEC_SUB * page
    n_sub = DEC_PPS // DEC_SUB
    nj = (H_A * QK_NOPE) // LANE

    def page_copies(pid, slot, i):
        return (pltpu.make_async_copy(cache_c.at[pid], cpage.at[slot, i], sem_c.at[slot]),
                pltpu.make_async_copy(cache_kr.at[pid], krpage.at[slot, i], sem_kr.at[slot]))

    def issue(bb, c, slot):
        for i in range(DEC_PPS):
            for cp in page_copies(pt_ref[bb, c * DEC_PPS + i], slot, i):
                cp.start()

    def wait(slot):
        for i in range(DEC_PPS):
            for cp in page_copies(0, slot, i):
                cp.wait()

    @pl.when(b == 0)
    def _():
        cb_sc[1] = jnp.zeros(cb_sc.shape[1:], BF16)
        pcat_sc[1] = jnp.zeros(pcat_sc.shape[1:], BF16)
        krb_sc[1] = jnp.zeros(krb_sc.shape[1:], BF16)
        kr2b_sc[1] = jnp.zeros(kr2b_sc.shape[1:], BF16)
        issue(0, 0, 0)

    qg = qp_ref[0] * gkp_ref[...]
    qr = (qr_ref[0] * gkr_ref[...]).astype(BF16)
    col_head = (lax.broadcasted_iota(jnp.int32, (H_A, H_A * QK_NOPE), 1) % LANE) // (LANE // H_A)
    q_rows = jnp.where(col_head == lax.broadcasted_iota(jnp.int32, (H_A, H_A * QK_NOPE), 0), qg, 0.0)
    q_hi = q_rows.astype(BF16)
    q_lo = (q_rows - q_hi.astype(F32)).astype(BF16)
    q_abs = _dot_nt(q_hi, wuk_ref[...]) + _dot_nt(q_lo, wuk_ref[...])
    qa_hi = q_abs.astype(BF16)
    qa_lo = (q_abs - qa_hi.astype(F32)).astype(BF16)
    qa2 = jnp.concatenate([qa_hi, qa_lo], axis=0)

    def keys_partial(slot, sub):
        cb = jnp.concatenate([cpage[slot, sub * DEC_SUB + t].astype(BF16) for t in range(DEC_SUB)], axis=0)
        r0 = sub * sub_rows
        cb_sc[slot, r0:r0 + sub_rows, :] = cb
        kk = _dot(cb, wuk_ref[...])
        p_sq = None
        for jj in range(nj):
            blk = kk[:, jj * LANE:(jj + 1) * LANE]
            sq = blk * blk
            p_sq = sq if p_sq is None else p_sq + sq
        pcat_sc[slot, r0:r0 + sub_rows, :] = p_sq.astype(BF16)

    def rope_keys(slot):
        krt = jnp.concatenate([krpage[slot, i] for i in range(DEC_PPS)], axis=1)
        krb_sc[slot] = krt.astype(BF16)
        kr2b_sc[slot] = (krt * krt).astype(BF16)

    def scores(slot):
        s2 = _dot_nt(qa2, cb_sc[slot])
        ns = _dot_nt(segt_ref[...], pcat_sc[slot])
        ns = ns + _dot(qr, krb_sc[slot]) + _dot(ones_ref[...], kr2b_sc[slot])
        s = s2[0:H_A, :] + s2[H_A:2 * H_A, :] + ns[H_A:2 * H_A, :]
        return s * lax.rsqrt(ns[0:H_A, :] * (1.0 / QK_HEAD) + EPS)

    def softmax_step(s, m_prev, l_prev):
        m_next = jnp.maximum(m_prev, jnp.max(s, axis=-1, keepdims=True))
        alpha = jnp.exp2(m_prev - m_next)
        p = jnp.exp2(s - m_next)
        return p, alpha, m_next, alpha * l_prev + jnp.sum(p, axis=-1, keepdims=True)

    def chunk_step(c, slot, carry):
        m_prev, l_prev, acc_prev = carry
        wait(slot)
        last = c + 1 == n_chunks
        nxt_b = jnp.minimum(jnp.where(last, b + 1, b), nb - 1)
        nxt_c = jnp.where(last, 0, c + 1)
        issue(nxt_b, nxt_c, 1 - slot)

        valid = c > 0
        prev = 1 - slot
        keys_partial(slot, 0)
        s = scores(prev)
        keys_partial(slot, 1)
        p, alpha, m_next, l_next = softmax_step(s, m_prev, l_prev)
        keys_partial(slot, 2)
        acc_next = acc_prev * alpha + _dot(p.astype(BF16), cb_sc[prev])
        for sub in range(3, n_sub):
            keys_partial(slot, sub)
        rope_keys(slot)
        return (jnp.where(valid, m_next, m_prev), jnp.where(valid, l_next, l_prev),
                jnp.where(valid, acc_next, acc_prev))

    def pair_step(i, carry):
        carry = chunk_step(2 * i, 0, carry)
        return chunk_step(2 * i + 1, 1, carry)

    init = (jnp.full((H_A, 1), -jnp.inf, F32), jnp.zeros((H_A, 1), F32), jnp.zeros((H_A, KV_LORA), F32))
    m_run, l_run, acc = lax.fori_loop(0, n_chunks // 2, pair_step, init)
    p, alpha, m_run, l_run = softmax_step(scores(1), m_run, l_run)
    acc_ref[0] = acc * alpha + _dot(p.astype(BF16), cb_sc[1])
    m_ref[0] = jnp.broadcast_to(m_run, (H_A, LANE))
    l_ref[0] = jnp.broadcast_to(l_run, (H_A, LANE))

    @pl.when(b == nb - 1)
    def _():
        wait(0)


def _decode_attention(page_table, q_perm, qr_mat, cache_c, cache_krt, dw):
    bd, n_pages = page_table.shape
    page = cache_c.shape[1]
    rows = DEC_PPS * page
    assert n_pages % (2 * DEC_PPS) == 0 and DEC_PPS // DEC_SUB >= 3
    const = lambda shp: pl.BlockSpec(shp, lambda b, pt: (0,) * len(shp))
    in_specs = [pl.BlockSpec((1, 1, H_A * QK_NOPE), lambda b, pt: (b, 0, 0)),
                pl.BlockSpec((1, 2 * H_A, QK_ROPE), lambda b, pt: (b, 0, 0)),
                const((1, H_A * QK_NOPE)), const((1, QK_ROPE)), const((KV_LORA, H_A * QK_NOPE)),
                const((2 * H_A, LANE)), const((2 * H_A, QK_ROPE)),
                pl.BlockSpec(memory_space=pl.ANY), pl.BlockSpec(memory_space=pl.ANY)]
    out_b = lambda w: pl.BlockSpec((1, H_A, w), lambda b, pt: (b, 0, 0))
    grid_spec = pltpu.PrefetchScalarGridSpec(
        num_scalar_prefetch=1,
        grid=(bd,),
        in_specs=in_specs,
        out_specs=(out_b(KV_LORA), out_b(LANE), out_b(LANE)),
        scratch_shapes=[pltpu.VMEM((2, DEC_PPS, page, KV_LORA), F32),
                        pltpu.VMEM((2, DEC_PPS, QK_ROPE, page), F32),
                        pltpu.VMEM((2, rows, KV_LORA), BF16),
                        pltpu.VMEM((2, rows, LANE), BF16),
                        pltpu.VMEM((2, QK_ROPE, rows), BF16),
                        pltpu.VMEM((2, QK_ROPE, rows), BF16),
                        pltpu.SemaphoreType.DMA((2,)), pltpu.SemaphoreType.DMA((2,))],
    )
    return pl.pallas_call(
        _decode_kernel,
        grid_spec=grid_spec,
        out_shape=(jax.ShapeDtypeStruct((bd, H_A, KV_LORA), F32),
                   jax.ShapeDtypeStruct((bd, H_A, LANE), F32),
                   jax.ShapeDtypeStruct((bd, H_A, LANE), F32)),
        compiler_params=_cparams(("arbitrary",)),
        name="decode_attn",
    )(page_table, q_perm.reshape(bd, 1, -1), qr_mat, dw['gk_perm'], dw['gk_rope'], dw['w_uk_perm'], dw['seg_t'],
      dw['ones_rows'], cache_c, cache_krt)


def _decode_final_kernel(q_ref, k_ref, c_ref, m_ref, l_ref, acc_ref, wuv_ref, o_ref):
    c_new = c_ref[...]
    for pair in range(H_A // 2):
        o_pair = None
        for h in (2 * pair, 2 * pair + 1):
            lo, hi = h * HEAD_PAD, (h + 1) * HEAD_PAD
            s_new = jnp.sum(q_ref[:, lo:hi] * k_ref[:, lo:hi].astype(F32), axis=-1, keepdims=True)
            m_old = m_ref[:, h:h + 1]
            m_new = jnp.maximum(m_old, s_new)
            a = jnp.exp2(m_old - m_new)
            pn = jnp.exp2(s_new - m_new)
            l_new = l_ref[:, h:h + 1] * a + pn
            ctx = (acc_ref[:, h * KV_LORA:(h + 1) * KV_LORA] * a + pn * c_new) / l_new
            part = _dot(ctx.astype(BF16), wuv_ref[h])
            o_pair = part if o_pair is None else o_pair + part
        o_ref[:, pair * LANE:(pair + 1) * LANE] = o_pair.astype(o_ref.dtype)


def _decode_final(q, k_new, c_new, m, l, acc, dw):
    bd = q.shape[0]
    return pl.pallas_call(
        _decode_final_kernel,
        in_specs=[_whole()] * 7,
        out_specs=_whole(),
        out_shape=jax.ShapeDtypeStruct((bd, H_A * V_HEAD), BF16),
        compiler_params=pltpu.CompilerParams(vmem_limit_bytes=VMEM_LIMIT),
        name="decode_final",
    )(q, k_new, c_new, m, l, acc, dw['w_uv_pair'])


def _merge_kernel(x_ref, attn_ref, ssm_ref, gates_ref, wa_ref, wb_ref, wo_ref, gffn_ref, wr_ref, br_ref,
                  h_ref, hn_ref, logit_ref):
    g = jax.nn.sigmoid(gates_ref[...])
    a = _dot(attn_ref[...], wa_ref[...])
    b = _dot(ssm_ref[...].astype(BF16), wb_ref[...])
    mixed = g[:, :D_MODEL] * a + g[:, D_MODEL:] * b
    h = x_ref[...] + _dot(mixed.astype(BF16), wo_ref[...])
    h_ref[...] = h
    hn = _rms(h, gffn_ref[...])
    hn_ref[...] = hn
    hi, mid, _ = _split3(hn)
    w_hi = wr_ref[0]
    w_lo = wr_ref[1]
    logit_ref[...] = _dot(hi, w_hi) + _dot(mid, w_hi) + _dot(hi, w_lo) + br_ref[...]


def _merge(x, attn, ssm, gates, mw, *, tm):
    t = x.shape[0]
    row = lambda w: pl.BlockSpec((tm, w), lambda i: (i, 0))
    consts = (mw['w_a_out'], mw['w_b_out'], mw['w_out'], mw['g_ffn'], mw['w_router2'], mw['b_router128'])
    return pl.pallas_call(
        _merge_kernel,
        grid=(t // tm,),
        in_specs=[row(D_MODEL), row(H_A * V_HEAD), row(D_INNER), row(2 * D_MODEL)] + [_whole()] * len(consts),
        out_specs=(row(D_MODEL), row(D_MODEL), row(LANE)),
        out_shape=(jax.ShapeDtypeStruct((t, D_MODEL), F32), jax.ShapeDtypeStruct((t, D_MODEL), F32),
                   jax.ShapeDtypeStruct((t, LANE), F32)),
        compiler_params=_cparams(("parallel",)),
        name="merge",
    )(x, attn, ssm, gates, *consts)


def _moe_kernel(be_ref, cnt_ref, seg_ref, tok_ref, asg_ref, x_hbm, wgu_ref, bgu_ref, wd_ref, bd_ref, y_hbm,
                xbuf, obuf, wgu_b, wd_b, sem_in, sem_out):
    i = pl.program_id(0)
    n = pl.num_programs(0)
    bm = xbuf.shape[1]
    dump0 = y_hbm.shape[0] - bm

    def in_copy(tok, slot, r):
        return pltpu.make_async_copy(x_hbm.at[pl.ds(tok, 1), :], xbuf.at[slot, pl.ds(r, 1), :], sem_in.at[slot])

    def out_copy(row, slot, r):
        return pltpu.make_async_copy(obuf.at[slot, pl.ds(r, 1), :], y_hbm.at[pl.ds(row, 1), :], sem_out.at[slot])

    def gather_loop(blk, slot):
        base = seg_ref[blk]

        def body(r, carry):
            in_copy(tok_ref[base + r], slot, r).start()
            return carry
        lax.fori_loop(0, bm, body, 0, unroll=8)

    def gather_inline(blk, slot):
        base = seg_ref[blk]
        for r in range(bm):
            in_copy(tok_ref[base + r], slot, r).start()

    def gather_wait(slot):
        def body(r, carry):
            in_copy(0, slot, r).wait()
            return carry
        lax.fori_loop(0, bm, body, 0, unroll=8)

    def scattered(blk):
        return (blk < 0) | (cnt_ref[jnp.maximum(blk, 0)] > 0)

    def scatter_rows(blk):
        j = jnp.maximum(blk, 0)
        return seg_ref[j], jnp.where(blk >= 0, cnt_ref[j], 0)

    def scatter_loop(blk, slot):
        base, count = scatter_rows(blk)

        def body(r, carry):
            out_copy(jnp.where(r < count, asg_ref[base + r], dump0 + r), slot, r).start(priority=1)
            return carry
        lax.fori_loop(0, bm, body, 0, unroll=8)

    def scatter_inline(blk, slot):
        base, count = scatter_rows(blk)
        for r in range(bm):
            out_copy(jnp.where(count > r, asg_ref[base + r], dump0 + r), slot, r).start(priority=1)

    def scatter_wait(slot):
        def body(r, carry):
            out_copy(0, slot, r).wait()
            return carry
        lax.fori_loop(0, bm, body, 0, unroll=8)

    valid = cnt_ref[i] > 0
    nxt_valid = (i + 1 < n) & (cnt_ref[jnp.minimum(i + 1, n - 1)] > 0)

    @pl.when(i == 0)
    def _():
        obuf[1] = jnp.zeros(obuf.shape[1:], F32)

    @pl.when((i == 0) & valid)
    def _():
        gather_loop(0, 0)

    @pl.when((i >= 1) & scattered(i - 2))
    def _():
        scatter_wait(i % 2)

    @pl.when((i == 0) | (be_ref[i] != be_ref[jnp.maximum(i - 1, 0)]))
    def _():
        wgu_b[...] = wgu_ref[0].astype(BF16)
        wd_b[...] = wd_ref[0].astype(BF16)

    def expert_block(prefetch_next, slot):
        gather_wait(slot)
        x = xbuf[slot].astype(BF16)
        scatter_inline(i - 1, 1 - slot)
        if prefetch_next:
            gather_inline(i + 1, 1 - slot)
        gu = _dot(x, wgu_b[...]) + bgu_ref[0]
        gate = jnp.minimum(gu[:, :D_FF], SWIGLU_LIMIT)
        up = jnp.clip(gu[:, D_FF:], -SWIGLU_LIMIT, SWIGLU_LIMIT)
        act = (up + 1.0) * gate * jax.nn.sigmoid(SWIGLU_ALPHA * gate)
        obuf[slot] = _dot(act.astype(BF16), wd_b[...]) + bd_ref[0]

    for parity in range(2):
        on_parity = valid & (i % 2 == parity)

        @pl.when(on_parity & nxt_valid)
        def _():
            expert_block(True, parity)

        @pl.when(on_parity & jnp.logical_not(nxt_valid))
        def _():
            expert_block(False, parity)

    @pl.when(jnp.logical_not(valid) & scattered(i - 1))
    def _():
        scatter_loop(i - 1, (i + 1) % 2)

    @pl.when((i == n - 1) & scattered(i - 1))
    def _():
        scatter_wait((i + 1) % 2)


def _moe_experts(block_e, block_cnt, seg_start, tok_sorted, asg_sorted, x, w_gate_up, b_gate_up, w_down, b_down):
    n_blocks = block_e.shape[0]
    bm = MOE_BM
    n_assign = x.shape[0] * TOP_K
    w_spec = lambda shp: pl.BlockSpec(shp, lambda i, be, ct, sg, tk, ag: (be[i], 0, 0))
    grid_spec = pltpu.PrefetchScalarGridSpec(
        num_scalar_prefetch=5,
        grid=(n_blocks,),
        in_specs=[pl.BlockSpec(memory_space=pl.ANY), w_spec((1, D_MODEL, 2 * D_FF)), w_spec((1, 1, 2 * D_FF)),
                  w_spec((1, D_FF, D_MODEL)), w_spec((1, 1, D_MODEL))],
        out_specs=pl.BlockSpec(memory_space=pl.ANY),
        scratch_shapes=[pltpu.VMEM((2, bm, D_MODEL), F32), pltpu.VMEM((2, bm, D_MODEL), F32),
                        pltpu.VMEM((D_MODEL, 2 * D_FF), BF16), pltpu.VMEM((D_FF, D_MODEL), BF16),
                        pltpu.SemaphoreType.DMA((2,)), pltpu.SemaphoreType.DMA((2,))],
    )
    return pl.pallas_call(
        _moe_kernel,
        grid_spec=grid_spec,
        out_shape=jax.ShapeDtypeStruct((n_assign + bm, D_MODEL), F32),
        compiler_params=_cparams(("arbitrary",)),
        name="moe_experts",
    )(block_e, block_cnt, seg_start, tok_sorted, asg_sorted, x, w_gate_up,
      b_gate_up.reshape(N_EXPERTS, 1, 2 * D_FF), w_down, b_down.reshape(N_EXPERTS, 1, D_MODEL))


def _ple_kernel(y_ref, hp_ref, hs_ref, gw_ref, pp_ref, ps_ref, gin_ref, wgate_ref, wple_ref, gple_ref,
                op_ref, os_ref):
    i = pl.program_id(0)
    n = pl.num_programs(0)
    tm = hp_ref.shape[0]
    is_sample = i == n - 1
    gw = gw_ref[...]
    y = y_ref[0:tm, :] * gw[:, 0:1]
    for k in range(1, TOP_K):
        y = y + y_ref[k * tm:(k + 1) * tm, :] * gw[:, k:k + 1]
    h2 = jnp.where(is_sample, hs_ref[...], hp_ref[...]) + y
    p_emb = jnp.where(is_sample, ps_ref[...], pp_ref[...])
    gate = jax.nn.sigmoid(_dot(_rms(h2, gin_ref[...]).astype(BF16), wgate_ref[...]))
    ple = _rms(_dot(p_emb.astype(BF16), wple_ref[...]), gple_ref[...]) * gate
    out = h2 + ple

    @pl.when(i < n - 1)
    def _():
        op_ref[...] = out

    @pl.when(i == n - 1)
    def _():
        os_ref[...] = out


def _combine_ple(y_rows, h_p, h_s, gate_w, p_p, p_s, pw):
    n_prompt = h_p.shape[0]
    tm = PLE_TM
    assert h_s.shape[0] == tm and n_prompt % tm == 0
    n_tiles = n_prompt // tm + 1
    prompt_row = lambda w: pl.BlockSpec((tm, w), lambda i: (jnp.minimum(i, n_tiles - 2), 0))
    sample_row = lambda w: pl.BlockSpec((tm, w), lambda i: (0, 0))
    return pl.pallas_call(
        _ple_kernel,
        grid=(n_tiles,),
        in_specs=[pl.BlockSpec((tm * TOP_K, D_MODEL), lambda i: (i, 0)), prompt_row(D_MODEL), sample_row(D_MODEL),
                  pl.BlockSpec((tm, LANE), lambda i: (i, 0)), prompt_row(PLE_DIM), sample_row(PLE_DIM)]
                 + [_whole()] * 4,
        out_specs=(prompt_row(D_MODEL), sample_row(D_MODEL)),
        out_shape=(jax.ShapeDtypeStruct((n_prompt, D_MODEL), F32), jax.ShapeDtypeStruct((tm, D_MODEL), F32)),
        compiler_params=_cparams(("arbitrary",)),
        name="combine_ple",
    )(y_rows, h_p, h_s, gate_w, p_p, p_s, pw['g_ple_in'], pw['w_ple_gate'], pw['w_ple'], pw['g_ple'])


def _pad_lanes(x, n):
    return jnp.pad(x, [(0, 0)] * (x.ndim - 1) + [(0, n - x.shape[-1])])


def _rot_cols(w):
    half = QK_ROPE // 2
    return jnp.concatenate([-w[..., half:], w[..., :half]], axis=-1)


def _prep_weights(g_mix_norm, w_in, g_q_a, w_q_b, g_kv_a, w_uk, w_uv, g_q_head, g_k_head, w_a_out, w_conv,
                  b_conv, dt_bias, a_log, d_skip, g_ssm_norm, w_b_out, w_out, g_ffn_norm, w_router, b_router,
                  g_ple_in, w_ple_gate, w_ple, g_ple):
    sizes = (Q_LORA, KV_LORA, QK_ROPE, D_INNER, CONV_DIM, H_B, 2 * D_MODEL)
    offs = np.concatenate([[0], np.cumsum(sizes)])
    wq, wkv, wkr, wz, wxbc, wdt, wg = [w_in[:, int(offs[i]):int(offs[i + 1])] for i in range(7)]
    zc = lambda n: jnp.zeros((D_MODEL, n), F32)
    kr128 = jnp.concatenate([zc(QK_NOPE), wkr, zc(LANE - QK_HEAD)], axis=1)
    krrot = jnp.concatenate([zc(QK_NOPE), _rot_cols(wkr), zc(LANE - QK_HEAD)], axis=1)
    w_lat = jnp.concatenate([wq, wkv, kr128, krrot, _pad_lanes(wdt, LANE)], axis=1).astype(BF16)

    wqb = w_q_b.reshape(Q_LORA, H_A, QK_HEAD)
    nope, rope_w = wqb[..., :QK_NOPE], wqb[..., QK_NOPE:]
    z_nope = jnp.zeros_like(nope)
    q128 = _pad_lanes(jnp.concatenate([nope, rope_w], axis=-1), LANE).reshape(Q_LORA, H_A * LANE)
    qrot = _pad_lanes(jnp.concatenate([z_nope, _rot_cols(rope_w)], axis=-1), LANE).reshape(Q_LORA, H_A * LANE)
    w_qb = jnp.concatenate([q128, qrot], axis=1).astype(BF16)

    wuk3 = w_uk.reshape(KV_LORA, H_A, QK_NOPE)
    w_uk_pad = _pad_lanes(wuk3, LANE).reshape(KV_LORA, H_A * LANE).astype(BF16)
    sub = LANE // H_A
    w_uk_perm = wuk3.reshape(KV_LORA, H_A, QK_NOPE // sub, sub).transpose(0, 2, 1, 3)
    w_uk_perm = w_uk_perm.reshape(KV_LORA, H_A * QK_NOPE).astype(BF16)
    gk_perm = jnp.broadcast_to(g_k_head[:QK_NOPE].reshape(1, QK_NOPE // sub, 1, sub),
                               (1, QK_NOPE // sub, H_A, sub)).reshape(1, H_A * QK_NOPE)

    wuv3 = w_uv.reshape(KV_LORA, H_A, V_HEAD)
    even = (jnp.arange(H_A) % 2 == 0)[None, :, None]
    zv = jnp.zeros_like(wuv3)
    w_uv_ext = jnp.where(even, jnp.concatenate([wuv3, zv], axis=-1), jnp.concatenate([zv, wuv3], axis=-1))
    ones_lane = np.zeros((H_A, LANE), np.float32)
    ones_lane[0::2, V_HEAD] = 1.0
    ones_lane[1::2, 0] = 1.0
    front = dict(
        g_mix=g_mix_norm.reshape(1, -1), w_lat=w_lat, w_z=wz.astype(BF16), w_xbc=wxbc.astype(BF16),
        w_g=wg.astype(BF16), g_q_a=g_q_a.reshape(1, -1), w_qb=w_qb, g_kv_a=g_kv_a.reshape(1, -1),
        w_uk_pad=w_uk_pad, w_uv_ext=w_uv_ext.reshape(KV_LORA, H_A * LANE).astype(BF16),
        v_ones=jnp.asarray(ones_lane.reshape(1, H_A * LANE)),
        g_q128=_pad_lanes(g_q_head.reshape(1, -1), LANE) * Q_SCALE,
        g_k128=_pad_lanes(g_k_head.reshape(1, -1), LANE),
    )

    lane_head = np.arange(LANE) // sub
    seg = (lane_head[None, :] == np.arange(H_A)[:, None]).astype(np.float32)
    seg_t = np.zeros((2 * H_A, LANE), np.float32)
    seg_t[:H_A] = seg
    ones_rows = np.zeros((2 * H_A, QK_ROPE), np.float32)
    ones_rows[:H_A] = 1.0
    w_uv_pair = w_uv_ext
    decode = dict(
        gk_perm=gk_perm, gk_rope=g_k_head[QK_NOPE:].reshape(1, QK_ROPE), w_uk_perm=w_uk_perm, seg_t=jnp.asarray(seg_t, BF16), ones_rows=jnp.asarray(ones_rows, BF16),
        w_uv_pair=w_uv_pair.transpose(1, 0, 2).astype(BF16),
    )

    expand = (np.arange(D_INNER)[None, :] // SSM_HEAD == np.arange(LANE)[:, None]).astype(np.float32)
    tri = (np.arange(SSD_CHUNK)[None, :] <= np.arange(SSD_CHUNK)[:, None]).astype(np.float32)
    ssm = dict(
        w_conv=w_conv, b_conv=b_conv.reshape(1, -1), dt_bias128=_pad_lanes(dt_bias.reshape(1, -1), LANE),
        a_head128=_pad_lanes(-jnp.exp(a_log).reshape(1, -1), LANE),
        d_skip_x=jnp.repeat(d_skip, SSM_HEAD).reshape(1, -1), g_norm=g_ssm_norm.reshape(1, -1),
        tri=jnp.asarray(tri, BF16), expand=jnp.asarray(expand, BF16),
    )

    wr = _pad_lanes(w_router, LANE)
    wr_hi = wr.astype(BF16)
    wr_lo = (wr - wr_hi.astype(F32)).astype(BF16)
    merge = dict(
        w_a_out=w_a_out.astype(BF16), w_b_out=w_b_out.astype(BF16), w_out=w_out.astype(BF16),
        g_ffn=g_ffn_norm.reshape(1, -1), w_router2=jnp.stack([wr_hi, wr_lo]),
        b_router128=_pad_lanes(b_router.reshape(1, -1), LANE),
    )
    ple = dict(g_ple_in=g_ple_in.reshape(1, -1), w_ple_gate=w_ple_gate.astype(BF16), w_ple=w_ple.astype(BF16),
               g_ple=g_ple.reshape(1, -1))
    return front, decode, ssm, merge, ple


def _rope_tables(pos):
    half = QK_ROPE // 2
    inv = ROPE_THETA ** (-jnp.arange(half, dtype=F32) / half)
    ang = pos.astype(F32)[:, None] * inv[None, :]
    cos, sin = jnp.cos(ang), jnp.sin(ang)
    n = pos.shape[0]
    cos128 = jnp.concatenate([jnp.ones((n, QK_NOPE), F32), cos, cos, jnp.ones((n, LANE - QK_HEAD), F32)], axis=1)
    sin128 = jnp.concatenate([jnp.zeros((n, QK_NOPE), F32), sin, sin, jnp.zeros((n, LANE - QK_HEAD), F32)], axis=1)
    return cos128, sin128


def _route(logits, n_tok):
    top_val, top_idx = lax.top_k(logits[:, :N_EXPERTS], TOP_K)
    gate_w = jax.nn.softmax(top_val, axis=-1)
    n_assign = n_tok * TOP_K
    flat_e = top_idx.reshape(-1).astype(jnp.int32)
    counts = jnp.sum((flat_e[:, None] == jnp.arange(N_EXPERTS, dtype=jnp.int32)[None, :]).astype(jnp.int32), axis=0)
    bm = MOE_BM
    padded = (counts + bm - 1) // bm * bm
    pend = jnp.cumsum(padded)
    pstarts = pend - padded
    starts = jnp.cumsum(counts) - counts
    idx_bits = max(1, (n_assign - 1).bit_length())
    key = jnp.sort(flat_e * (1 << idx_bits) + jnp.arange(n_assign, dtype=jnp.int32))
    asg = jnp.pad(key & ((1 << idx_bits) - 1), (0, bm)).astype(jnp.int32)
    tok_sorted = asg // TOP_K
    tile_rows = PLE_TM * TOP_K
    asg_sorted = (asg // tile_rows) * tile_rows + (asg % TOP_K) * PLE_TM + tok_sorted % PLE_TM
    n_blocks = -(-n_assign // bm) + N_EXPERTS
    blk_start = jnp.arange(n_blocks, dtype=jnp.int32) * bm
    block_e = jnp.minimum(jnp.sum((pend[None, :] <= blk_start[:, None]).astype(jnp.int32), axis=1), N_EXPERTS - 1)
    offset = blk_start - pstarts[block_e]
    block_cnt = jnp.where(blk_start < pend[-1], jnp.clip(counts[block_e] - offset, 0, bm), 0).astype(jnp.int32)
    seg_start = jnp.clip(starts[block_e] + offset, 0, n_assign).astype(jnp.int32)
    return gate_w, tok_sorted, asg_sorted, block_e.astype(jnp.int32), block_cnt, seg_start


def kernel(x_prompt, x_sample, p_prompt, p_sample, cache_kv_latent, cache_k_rope, page_table, state_conv, state_ssm, g_mix_norm, w_in, g_q_a, w_q_b, g_kv_a, w_uk, w_uv, g_q_head, g_k_head, w_a_out, w_conv, b_conv, dt_bias, a_log, d_skip, g_ssm_norm, w_b_out, w_out, g_ffn_norm, w_router, b_router, w_gate_up, b_gate_up, w_down, b_down, g_ple_in, w_ple_gate, w_ple, g_ple):
    depth = g_mix_norm.shape[0]
    assert depth == 1, "one layer"
    b, s, _ = x_prompt.shape
    bd, sd, _ = x_sample.shape
    assert sd == 1, "one new token per sample sequence"
    n_pages = page_table.shape[1]
    page_size = cache_kv_latent.shape[2]
    assert n_pages % DEC_PPS == 0 and s % SSD_CHUNK == 0
    tp = b * s

    fw, dw, sw, mw, pw = _prep_weights(
        g_mix_norm[0], w_in[0], g_q_a[0], w_q_b[0], g_kv_a[0], w_uk[0], w_uv[0], g_q_head[0], g_k_head[0],
        w_a_out[0], w_conv[0], b_conv[0], dt_bias[0], a_log[0], d_skip[0], g_ssm_norm[0], w_b_out[0], w_out[0],
        g_ffn_norm[0], w_router[0], b_router[0], g_ple_in[0], w_ple_gate[0], w_ple[0], g_ple[0])

    xp = x_prompt.reshape(tp, D_MODEL)
    cos_p, sin_p = _rope_tables(jnp.arange(s))
    tm_p = min(FRONT_TM, s)
    bound = QK_HEAD * jnp.max(jnp.abs(fw['g_q128'])) * jnp.max(jnp.abs(fw['g_k128']))
    bound = (1.02 * bound + 1.0).astype(BF16).astype(F32)
    bias_lane = (jnp.arange(LANE) == QK_HEAD).astype(F32).reshape(1, LANE)
    q_p, k_p, v_p, c_p, kr_p, z_p, xbc_p, dt_p, gates_p = _front(
        xp, cos_p, sin_p, fw, -bound * bias_lane, bias_lane, tm=tm_p, q_dtype=BF16, pos_blocks=s // tm_p)
    qkv = (q_p.reshape(b, s, -1), k_p.reshape(b, s, -1), v_p.reshape(b, s, -1))
    attn_p = lax.cond(bound <= ATTN_MAX_BOUND,
                      lambda q, k, v: _prompt_attention(q, k, v, b, s, bounded=True),
                      lambda q, k, v: _prompt_attention(q, k, v, b, s, bounded=False), *qkv)
    ssm_p, hfin_p = _ssd_prompt(xbc_p, z_p, dt_p, sw, b, s)
    h_p, hn_p, logit_p = _merge(xp, attn_p.reshape(tp, -1), ssm_p, gates_p, mw, tm=min(MERGE_TM, tp))

    xs = x_sample.reshape(bd, D_MODEL)
    cos_s, sin_s = _rope_tables(jnp.full((bd,), n_pages * page_size, jnp.int32))
    no_pad = jnp.zeros((1, LANE), F32)
    q_s, k_s, _, c_s, kr_s, z_s, xbc_s, dt_s, gates_s = _front(
        xs, cos_s, sin_s, fw, no_pad, no_pad, tm=bd, q_dtype=F32, pos_blocks=1)
    q3 = q_s.reshape(bd, H_A, LANE)
    sub = LANE // H_A
    q_perm = q3[:, :, :QK_NOPE].reshape(bd, H_A, QK_NOPE // sub, sub).transpose(0, 2, 1, 3).reshape(bd, -1)
    qr = q3[:, :, QK_NOPE:QK_HEAD]
    qr_mat = jnp.concatenate([jnp.zeros_like(qr), qr], axis=1)
    acc, m_run, l_run = _decode_attention(page_table, q_perm, qr_mat, cache_kv_latent[0],
                                          jnp.swapaxes(cache_k_rope[0], 1, 2), dw)
    attn_s = _decode_final(q_s, k_s, c_s, m_run[:, :, 0], l_run[:, :, 0], acc.reshape(bd, -1), dw)
    ssm_s, hnew_s = _ssm_sample(xbc_s, state_conv[0], z_s, dt_s,
                                state_ssm[0].reshape(bd, D_INNER, D_STATE), sw)
    h_s, hn_s, logit_s = _merge(xs, attn_s, ssm_s, gates_s, mw, tm=bd)

    hn_all = jnp.concatenate([hn_p, hn_s], axis=0)
    logits = jnp.concatenate([logit_p, logit_s], axis=0)
    n_tok = tp + bd
    gate_w, tok_sorted, asg_sorted, block_e, block_cnt, seg_start = _route(logits, n_tok)
    y_rows = _moe_experts(block_e, block_cnt, seg_start, tok_sorted, asg_sorted, hn_all, w_gate_up[0],
                          b_gate_up[0], w_down[0], b_down[0])
    out_p, out_s = _combine_ple(y_rows, h_p, h_s, _pad_lanes(gate_w, LANE), p_prompt[0].reshape(tp, PLE_DIM),
                                p_sample[0].reshape(bd, PLE_DIM), pw)

    y_prompt = out_p.reshape(b, s, D_MODEL)
    y_sample = out_s.reshape(bd, sd, D_MODEL)
    new_c_p = c_p.reshape(1, b, s, KV_LORA)
    new_kr_p = kr_p[:, QK_NOPE:QK_HEAD].reshape(1, b, s, QK_ROPE)
    conv_p = xbc_p.reshape(b, s, CONV_DIM)[:, s - (CONV_W - 1):].reshape(1, b, CONV_W - 1, CONV_DIM)
    ssm_state_p = hfin_p.reshape(1, b, H_B, SSM_HEAD, D_STATE)
    new_c_s = c_s.reshape(1, bd, sd, KV_LORA)
    new_kr_s = kr_s[:, QK_NOPE:QK_HEAD].reshape(1, bd, sd, QK_ROPE)
    conv_s = jnp.concatenate([state_conv[0][:, 1:], xbc_s[:, None, :]], axis=1).reshape(1, bd, CONV_W - 1, CONV_DIM)
    ssm_state_s = hnew_s.reshape(1, bd, H_B, SSM_HEAD, D_STATE)
    return (y_prompt, y_sample, new_c_p, new_kr_p, conv_p, ssm_state_p, new_c_s, new_kr_s, conv_s, ssm_state_s)
```

```python
import functools
import math

import jax
import jax.numpy as jnp
import numpy as np
from jax import lax
from jax.experimental import pallas as pl
from jax.experimental.pallas import tpu as pltpu

F32 = jnp.float32
BF16 = jnp.bfloat16

D_MODEL = 1024
H_A = 16
Q_LORA = 384
KV_LORA = 256
QK_NOPE = 64
QK_ROPE = 32
QK_HEAD = QK_NOPE + QK_ROPE
V_HEAD = 64
ROPE_THETA = 10000.0
D_INNER = 2 * D_MODEL
SSM_HEAD = 64
H_B = D_INNER // SSM_HEAD
N_GROUPS = 4
D_STATE = 128
CONV_W = 4
CONV_DIM = D_INNER + 2 * N_GROUPS * D_STATE
SSD_CHUNK = 128
N_EXPERTS = 32
TOP_K = 4
D_FF = D_MODEL
SWIGLU_LIMIT = 7.0
SWIGLU_ALPHA = 1.702
PLE_DIM = 256
EPS = 1e-6

LANE = 128
HEAD_PAD = LANE
GROUP_W = D_INNER // N_GROUPS
HEADS_PER_GROUP = H_B // N_GROUPS
Q_SCALE = QK_HEAD ** -0.5 * math.log2(math.e)
VMEM_LIMIT = 56 * 1024 * 1024

FRONT_TM = 256
ATTN_TQ = 1024
ATTN_MAX_BOUND = 40.0
MERGE_TM = 256
MOE_BM = 256
PLE_TM = 128
DEC_PPS = 16
DEC_SUB = 4


def _dot(a, b):
    return jnp.dot(a, b, preferred_element_type=F32)


def _dot_nt(a, b):
    return lax.dot_general(a, b, (((1,), (1,)), ((), ())), preferred_element_type=F32)


def _dot_tn(a, b):
    return lax.dot_general(a, b, (((0,), (0,)), ((), ())), preferred_element_type=F32)


def _split3(x):
    hi = x.astype(BF16)
    r1 = x - hi.astype(F32)
    mid = r1.astype(BF16)
    lo = (r1 - mid.astype(F32)).astype(BF16)
    return hi, mid, lo


def _dot_f32_lhs(x, e, terms=3):
    parts = _split3(x)[:terms]
    out = _dot(parts[0], e)
    for part in parts[1:]:
        out = out + _dot(part, e)
    return out


def _dot_f32_rhs(e, x):
    hi, mid, lo = _split3(x)
    return _dot(e, hi) + _dot(e, mid) + _dot(e, lo)


def _rms(x, g):
    return x * lax.rsqrt(jnp.mean(x * x, axis=-1, keepdims=True) + EPS) * g


def _silu(x):
    return x * jax.nn.sigmoid(x)


def _cparams(sem, vmem=VMEM_LIMIT):
    return pltpu.CompilerParams(dimension_semantics=sem, vmem_limit_bytes=vmem)


def _whole():
    return pl.BlockSpec(memory_space=pltpu.VMEM)


def _head_norm(xh, g):
    ss = jnp.sum(xh * xh, axis=-1, keepdims=True)
    return xh * lax.rsqrt(ss * (1.0 / QK_HEAD) + EPS) * g


def _front_kernel(x_ref, cos_ref, sin_ref, gmix_ref, wlat_ref, wz_ref, wxbc_ref, wg_ref,
                  gqa_ref, wqb_ref, gkva_ref, wuk_ref, wuv_ref, gq_ref, gk_ref, qpad_ref, kpad_ref, vpad_ref,
                  q_ref, k_ref, v_ref, c_ref, kr_ref, z_ref, xbc_ref, dt_ref, gates_ref):
    x = x_ref[...]
    ub = _rms(x, gmix_ref[...]).astype(BF16)
    z_ref[...] = _dot(ub, wz_ref[...])
    xbc_ref[...] = _dot(ub, wxbc_ref[...])
    gates_ref[...] = _dot(ub, wg_ref[...])
    lat = _dot(ub, wlat_ref[...])
    q_lat = lat[:, :Q_LORA]
    kv_lat = lat[:, Q_LORA:Q_LORA + KV_LORA]
    o = Q_LORA + KV_LORA
    kr_raw = lat[:, o:o + LANE]
    kr_rot = lat[:, o + LANE:o + 2 * LANE]
    dt_ref[...] = lat[:, o + 2 * LANE:o + 3 * LANE]
    cos = cos_ref[...]
    sin = sin_ref[...]
    c = _rms(kv_lat, gkva_ref[...])
    c_ref[...] = c
    kr = kr_raw * cos + kr_rot * sin
    kr_ref[...] = kr
    qn = _rms(q_lat, gqa_ref[...]).astype(BF16)
    q2 = _dot(qn, wqb_ref[...])
    nq = H_A * HEAD_PAD
    gq = gq_ref[...]
    qpad = qpad_ref[...]
    kpad = kpad_ref[...]
    for h in range(H_A):
        lo, hi = h * HEAD_PAD, (h + 1) * HEAD_PAD
        qh = q2[:, lo:hi] * cos + q2[:, nq + lo:nq + hi] * sin
        q_ref[:, lo:hi] = (_head_norm(qh, gq) + qpad).astype(q_ref.dtype)
    cb = c.astype(BF16)
    kn = _dot(cb, wuk_ref[...])
    gk = gk_ref[...]
    for h in range(H_A):
        lo, hi = h * HEAD_PAD, (h + 1) * HEAD_PAD
        k_ref[:, lo:hi] = (_head_norm(kn[:, lo:hi] + kr, gk) + kpad).astype(k_ref.dtype)
    v_ref[...] = (_dot(cb, wuv_ref[...]) + vpad_ref[...]).astype(v_ref.dtype)


def _front(x, cos, sin, fw, qpad, kpad, *, tm, q_dtype, pos_blocks):
    t = x.shape[0]
    nt = t // tm
    row = lambda w: pl.BlockSpec((tm, w), lambda i: (i, 0))
    pos_spec = pl.BlockSpec((tm, LANE), lambda i: (i % pos_blocks, 0))
    nq = H_A * HEAD_PAD
    out_shape = (
        jax.ShapeDtypeStruct((t, nq), q_dtype),
        jax.ShapeDtypeStruct((t, nq), BF16),
        jax.ShapeDtypeStruct((t, nq), BF16),
        jax.ShapeDtypeStruct((t, KV_LORA), F32),
        jax.ShapeDtypeStruct((t, LANE), F32),
        jax.ShapeDtypeStruct((t, D_INNER), F32),
        jax.ShapeDtypeStruct((t, CONV_DIM), F32),
        jax.ShapeDtypeStruct((t, LANE), F32),
        jax.ShapeDtypeStruct((t, 2 * D_MODEL), F32),
    )
    out_specs = (row(nq), row(nq), row(nq), row(KV_LORA), row(LANE), row(D_INNER),
                 row(CONV_DIM), row(LANE), row(2 * D_MODEL))
    weights = (fw['g_mix'], fw['w_lat'], fw['w_z'], fw['w_xbc'], fw['w_g'], fw['g_q_a'], fw['w_qb'],
               fw['g_kv_a'], fw['w_uk_pad'], fw['w_uv_ext'], fw['g_q128'], fw['g_k128'], qpad, kpad, fw['v_ones'])
    return pl.pallas_call(
        _front_kernel,
        grid=(nt,),
        in_specs=[row(D_MODEL), pos_spec, pos_spec] + [_whole()] * len(weights),
        out_specs=out_specs,
        out_shape=out_shape,
        compiler_params=_cparams(("parallel",)),
        name="front",
    )(x, cos, sin, *weights)


def _attn_kernel(qi_ref, ki_ref, q_ref, k_ref, v_ref, o_ref, acc_sc, *m_scratch, bounded):
    step = pl.program_id(2)
    qi = qi_ref[step]
    ki = ki_ref[step]
    tq = q_ref.shape[0]
    half = tq // 2

    @pl.when(ki == 0)
    def _():
        acc_sc[...] = jnp.zeros(acc_sc.shape, F32)
        if not bounded:
            m_scratch[0][...] = jnp.full(m_scratch[0].shape, -jnp.inf, F32)

    def update(q0, qn, k0, kn, masked):
        for h in range(2):
            cols = slice(h * HEAD_PAD, (h + 1) * HEAD_PAD)
            s = _dot_nt(q_ref[q0:q0 + qn, cols], k_ref[k0:k0 + kn, cols])
            v = v_ref[k0:k0 + kn, cols]
            if masked:
                keep = (lax.broadcasted_iota(jnp.int32, (qn, kn), 1) <= lax.broadcasted_iota(jnp.int32, (qn, kn), 0))
            if bounded:
                p = jnp.exp2(s)
                if masked:
                    p = jnp.where(keep, p, 0.0)
                acc_sc[h, q0:q0 + qn, :] += _dot(p.astype(BF16), v)
            else:
                m_sc = m_scratch[0]
                if masked:
                    s = jnp.where(keep, s, -jnp.inf)
                m_prev = m_sc[h, q0:q0 + qn, :]
                m_next = jnp.maximum(m_prev, jnp.max(s, axis=-1, keepdims=True))
                alpha = jnp.exp2(m_prev - m_next)
                p = jnp.exp2(s - m_next[:, :1])
                acc_sc[h, q0:q0 + qn, :] = alpha * acc_sc[h, q0:q0 + qn, :] + _dot(p.astype(BF16), v)
                m_sc[h, q0:q0 + qn, :] = m_next

    @pl.when(ki < qi)
    def _():
        update(0, tq, 0, tq, False)

    @pl.when(ki == qi)
    def _():
        update(0, half, 0, half, True)
        update(half, half, 0, half, False)
        update(half, half, half, half, True)
        lane = lax.broadcasted_iota(jnp.int32, (tq, LANE), 1)
        a0 = acc_sc[0]
        a1 = acc_sc[1]
        o0 = a0 / a0[:, V_HEAD:V_HEAD + 1]
        o1 = a1 / a1[:, 0:1]
        o_ref[...] = jnp.where(lane < V_HEAD, o0, o1).astype(o_ref.dtype)


def _prompt_attention(q, k, v, b, s, *, bounded):
    tq = min(ATTN_TQ, s)
    nq = s // tq
    pairs = [(i, j) for i in range(nq) for j in range(i + 1)]
    qi_tab = jnp.asarray([p[0] for p in pairs], jnp.int32)
    ki_tab = jnp.asarray([p[1] for p in pairs], jnp.int32)
    blk = lambda sel: pl.BlockSpec((None, tq, 2 * HEAD_PAD), lambda bi, h, t, qt, kt: (bi, sel(qt, kt)[t], h))
    scratch = [pltpu.VMEM((2, tq, LANE), F32)]
    if not bounded:
        scratch.append(pltpu.VMEM((2, tq, LANE), F32))
    grid_spec = pltpu.PrefetchScalarGridSpec(
        num_scalar_prefetch=2,
        grid=(b, H_A // 2, len(pairs)),
        in_specs=[blk(lambda qt, kt: qt), blk(lambda qt, kt: kt), blk(lambda qt, kt: kt)],
        out_specs=pl.BlockSpec((None, tq, 2 * V_HEAD), lambda bi, h, t, qt, kt: (bi, qt[t], h)),
        scratch_shapes=scratch,
    )
    return pl.pallas_call(
        functools.partial(_attn_kernel, bounded=bounded),
        grid_spec=grid_spec,
        out_shape=jax.ShapeDtypeStruct((b, s, H_A * V_HEAD), BF16),
        compiler_params=_cparams(("parallel", "parallel", "arbitrary")),
        name="prompt_attn_bounded" if bounded else "prompt_attn",
    )(qi_tab, ki_tab, q, k, v)


def _ssd_kernel(xbc_ref, z_ref, dt_ref, wconv_ref, bconv_ref, dtb_ref, ah_ref, dskip_ref, gn_ref,
                tri_ref, exp_ref, y_ref, hfin_ref, xbuf, state):
    ci = pl.program_id(1)
    nc = pl.num_programs(1)
    L = SSD_CHUNK

    @pl.when(ci == 0)
    def _():
        xbuf[0:8, :] = jnp.zeros((8, CONV_DIM), F32)
        state[...] = jnp.zeros(state.shape, F32)

    xbuf[8:8 + L, :] = xbc_ref[...]
    conv = bconv_ref[...] + xbuf[8:8 + L, :] * wconv_ref[3:4, :]
    for w in range(CONV_W - 1):
        sh = CONV_W - 1 - w
        conv = conv + xbuf[8 - sh:8 - sh + L, :] * wconv_ref[w:w + 1, :]
    xbuf[0:8, :] = xbuf[L:L + 8, :]
    xc = _silu(conv)
    xs = xc[:, :D_INNER]

    dt = jax.nn.softplus(dt_ref[...] + dtb_ref[...])
    a = dt * ah_ref[...]
    a_cum = _dot_f32_rhs(tri_ref[...], a)
    a_cum_t = a_cum.T
    a_last = a_cum[L - 1:L, :]
    ex = exp_ref[...]
    dt_x = _dot_f32_lhs(dt, ex, terms=2)
    dfs_x = _dot_f32_lhs(jnp.exp(a_cum), ex, terms=2)
    dte_x = _dot_f32_lhs(jnp.exp(a_last - a_cum), ex, terms=2)
    xdt = xs * dt_x
    xdt_b = xdt.astype(BF16)
    xw_b = (xdt * dte_x).astype(BF16)
    chunk_decay = jnp.exp(a_cum_t[:, L - 1:L])

    row = lax.broadcasted_iota(jnp.int32, (L, L), 0)
    col = lax.broadcasted_iota(jnp.int32, (L, L), 1)
    causal = col <= row
    lane = lax.broadcasted_iota(jnp.int32, (L, LANE), 1)
    for g in range(N_GROUPS):
        bm = xc[:, D_INNER + g * D_STATE:D_INNER + (g + 1) * D_STATE].astype(BF16)
        cm = xc[:, D_INNER + (N_GROUPS + g) * D_STATE:D_INNER + (N_GROUPS + g + 1) * D_STATE].astype(BF16)
        cb = _dot_nt(cm, bm)
        c0, c1 = g * GROUP_W, (g + 1) * GROUP_W
        st_prev = state[c0:c1, :]
        y_off = _dot_nt(cm, st_prev.astype(BF16)) * dfs_x[:, c0:c1]
        for j in range(HEADS_PER_GROUP // 2):
            h0 = g * HEADS_PER_GROUP + 2 * j
            x2 = xdt_b[:, h0 * SSM_HEAD:(h0 + 2) * SSM_HEAD]
            ys = []
            for hh in (h0, h0 + 1):
                seg = a_cum[:, hh:hh + 1] - a_cum_t[hh:hh + 1, :]
                sc = jnp.where(causal, cb * jnp.exp(seg), 0.0)
                ys.append(_dot(sc.astype(BF16), x2))
            y2 = jnp.where(lane < SSM_HEAD, ys[0], ys[1])
            lo = h0 * SSM_HEAD
            y_ref[:, lo:lo + LANE] = y2 + y_off[:, lo - c0:lo - c0 + LANE]
        st_new = _dot_tn(xw_b[:, c0:c1], bm)
        carry = jnp.concatenate(
            [jnp.broadcast_to(chunk_decay[hh:hh + 1, :], (SSM_HEAD, D_STATE))
             for hh in range(g * HEADS_PER_GROUP, (g + 1) * HEADS_PER_GROUP)], axis=0)
        state[c0:c1, :] = st_prev * carry + st_new

    y = y_ref[...] + dskip_ref[...] * xs
    yg = y * _silu(z_ref[...])
    gn = gn_ref[...]
    for g in range(N_GROUPS):
        c0, c1 = g * GROUP_W, (g + 1) * GROUP_W
        blk = yg[:, c0:c1]
        y_ref[:, c0:c1] = blk * lax.rsqrt(jnp.mean(blk * blk, axis=-1, keepdims=True) + EPS) * gn[:, c0:c1]

    @pl.when(ci == nc - 1)
    def _():
        hfin_ref[...] = state[...]


def _ssd_prompt(xbc, z, dt, sw, b, s):
    nc = s // SSD_CHUNK
    row = lambda w: pl.BlockSpec((SSD_CHUNK, w), lambda bi, ci: (bi * nc + ci, 0))
    consts = (sw['w_conv'], sw['b_conv'], sw['dt_bias128'], sw['a_head128'], sw['d_skip_x'], sw['g_norm'],
              sw['tri'], sw['expand'])
    return pl.pallas_call(
        _ssd_kernel,
        grid=(b, nc),
        in_specs=[row(CONV_DIM), row(D_INNER), row(LANE)] + [_whole()] * len(consts),
        out_specs=(row(D_INNER), pl.BlockSpec((None, D_INNER, D_STATE), lambda bi, ci: (bi, 0, 0))),
        out_shape=(jax.ShapeDtypeStruct((b * s, D_INNER), F32),
                   jax.ShapeDtypeStruct((b, D_INNER, D_STATE), F32)),
        scratch_shapes=[pltpu.VMEM((SSD_CHUNK + 8, CONV_DIM), F32), pltpu.VMEM((D_INNER, D_STATE), F32)],
        compiler_params=_cparams(("parallel", "arbitrary")),
        name="ssd_prompt",
    )(xbc, z, dt, *consts)


def _ssm_step_kernel(xbc_ref, cbuf_ref, z_ref, dt_ref, h_ref, wconv_ref, bconv_ref, dtb_ref, ah_ref,
                     dskip_ref, gn_ref, exp_ref, y_ref, hnew_ref):
    conv = bconv_ref[...] + xbc_ref[0] * wconv_ref[3:4, :]
    for w in range(CONV_W - 1):
        conv = conv + cbuf_ref[0, w:w + 1, :] * wconv_ref[w:w + 1, :]
    xc = _silu(conv)
    xs = xc[:, :D_INNER]
    dt = jax.nn.softplus(dt_ref[0] + dtb_ref[...])
    da = jnp.exp(dt * ah_ref[...])
    ex = exp_ref[...]
    xdt = xs * _dot_f32_lhs(jnp.broadcast_to(dt, (8, LANE)), ex)[0:1, :]

    eye = lax.broadcasted_iota(jnp.int32, (LANE, LANE), 0) == lax.broadcasted_iota(jnp.int32, (LANE, LANE), 1)

    def to_col(rowvec):
        return jnp.sum(jnp.where(eye, jnp.broadcast_to(rowvec, (LANE, LANE)), 0.0), axis=-1, keepdims=True)

    ys = []
    for g in range(N_GROUPS):
        bt = xc[:, D_INNER + g * D_STATE:D_INNER + (g + 1) * D_STATE]
        ct = xc[:, D_INNER + (N_GROUPS + g) * D_STATE:D_INNER + (N_GROUPS + g + 1) * D_STATE]
        ct8 = jnp.broadcast_to(ct, (8, D_STATE)).astype(BF16)
        for j in range(GROUP_W // LANE):
            r0 = g * GROUP_W + j * LANE
            h0 = r0 // SSM_HEAD
            h_old = h_ref[0, r0:r0 + LANE, :]
            decay = jnp.concatenate([jnp.broadcast_to(da[:, hh:hh + 1], (SSM_HEAD, D_STATE)) for hh in (h0, h0 + 1)],
                                    axis=0)
            h_new = h_old * decay + to_col(xdt[:, r0:r0 + LANE]) * bt
            hnew_ref[0, r0:r0 + LANE, :] = h_new
            ys.append(_dot_nt(ct8, h_new.astype(BF16))[0:1, :])
    y = jnp.concatenate(ys, axis=-1) + dskip_ref[...] * xs
    yg = y * _silu(z_ref[0])
    gn = gn_ref[...]
    for g in range(N_GROUPS):
        c0, c1 = g * GROUP_W, (g + 1) * GROUP_W
        blk = yg[:, c0:c1]
        y_ref[0, :, c0:c1] = blk * lax.rsqrt(jnp.mean(blk * blk, axis=-1, keepdims=True) + EPS) * gn[:, c0:c1]


def _ssm_sample(xbc, conv_buf, z, dt, h, sw):
    bd = xbc.shape[0]
    vec = lambda w: pl.BlockSpec((1, 1, w), lambda i: (i, 0, 0))
    consts = (sw['w_conv'], sw['b_conv'], sw['dt_bias128'], sw['a_head128'], sw['d_skip_x'], sw['g_norm'],
              sw['expand'])
    y, h_new = pl.pallas_call(
        _ssm_step_kernel,
        grid=(bd,),
        in_specs=[vec(CONV_DIM), pl.BlockSpec((1, CONV_W - 1, CONV_DIM), lambda i: (i, 0, 0)), vec(D_INNER),
                  vec(LANE), pl.BlockSpec((1, D_INNER, D_STATE), lambda i: (i, 0, 0))] + [_whole()] * len(consts),
        out_specs=(vec(D_INNER), pl.BlockSpec((1, D_INNER, D_STATE), lambda i: (i, 0, 0))),
        out_shape=(jax.ShapeDtypeStruct((bd, 1, D_INNER), F32),
                   jax.ShapeDtypeStruct((bd, D_INNER, D_STATE), F32)),
        compiler_params=_cparams(("parallel",)),
        name="ssm_sample",
    )(xbc.reshape(bd, 1, CONV_DIM), conv_buf, z.reshape(bd, 1, D_INNER), dt.reshape(bd, 1, LANE), h, *consts)
    return y.reshape(bd, D_INNER), h_new


def _decode_kernel(pt_ref, qa_ref, qr_ref, gkr_ref, wuk_ref, segt_ref, ones_ref, cache_c, cache_kr,
                   acc_ref, m_ref, l_ref, cpage, krpage, cb_sc, pcat_sc, krb_sc, kr2b_sc, sem_c, sem_kr):
    b = pl.program_id(0)
    nb = pl.num_programs(0)
    n_chunks = pt_ref.shape[1] // DEC_PPS
    page = cpage.shape[2]
    sub_rows = DEC_SUB * page
    n_sub = DEC_PPS // DEC_SUB
    nj = (H_A * QK_NOPE) // LANE

    def page_copies(pid, slot, i):
        return (pltpu.make_async_copy(cache_c.at[pid], cpage.at[slot, i], sem_c.at[slot]),
                pltpu.make_async_copy(cache_kr.at[pid], krpage.at[slot, i], sem_kr.at[slot]))

    def issue(bb, c, slot):
        for i in range(DEC_PPS):
            for cp in page_copies(pt_ref[bb, c * DEC_PPS + i], slot, i):
                cp.start()

    def wait(slot):
        for i in range(DEC_PPS):
            for cp in page_copies(0, slot, i):
                cp.wait()

    @pl.when(b == 0)
    def _():
        cb_sc[1] = jnp.zeros(cb_sc.shape[1:], BF16)
        pcat_sc[1] = jnp.zeros(pcat_sc.shape[1:], BF16)
        krb_sc[1] = jnp.zeros(krb_sc.shape[1:], BF16)
        kr2b_sc[1] = jnp.zeros(kr2b_sc.shape[1:], BF16)
        issue(0, 0, 0)

    qr = (qr_ref[0] * gkr_ref[...]).astype(BF16)
    q_abs = qa_ref[0]
    qa_hi = q_abs.astype(BF16)
    qa_lo = (q_abs - qa_hi.astype(F32)).astype(BF16)
    qa2 = jnp.concatenate([qa_hi, qa_lo], axis=0)

    def keys_partial(slot, sub):
        cb = jnp.concatenate([cpage[slot, sub * DEC_SUB + t].astype(BF16) for t in range(DEC_SUB)], axis=0)
        r0 = sub * sub_rows
        cb_sc[slot, r0:r0 + sub_rows, :] = cb
        kk = _dot(cb, wuk_ref[...])
        p_sq = None
        for jj in range(nj):
            blk = kk[:, jj * LANE:(jj + 1) * LANE]
            sq = blk * blk
            p_sq = sq if p_sq is None else p_sq + sq
        pcat_sc[slot, r0:r0 + sub_rows, :] = p_sq.astype(BF16)

    def rope_keys(slot):
        krt = jnp.concatenate([krpage[slot, i] for i in range(DEC_PPS)], axis=1)
        krb_sc[slot] = krt.astype(BF16)
        kr2b_sc[slot] = (krt * krt).astype(BF16)

    def scores(slot):
        s2 = _dot_nt(qa2, cb_sc[slot])
        ns = _dot_nt(segt_ref[...], pcat_sc[slot])
        ns = ns + _dot(qr, krb_sc[slot]) + _dot(ones_ref[...], kr2b_sc[slot])
        s = s2[0:H_A, :] + s2[H_A:2 * H_A, :] + ns[H_A:2 * H_A, :]
        return s * lax.rsqrt(ns[0:H_A, :] * (1.0 / QK_HEAD) + EPS)

    def softmax_step(s, m_prev, l_prev):
        m_next = jnp.maximum(m_prev, jnp.max(s, axis=-1, keepdims=True))
        alpha = jnp.exp2(m_prev - m_next)
        p = jnp.exp2(s - m_next)
        return p, alpha, m_next, alpha * l_prev + jnp.sum(p, axis=-1, keepdims=True)

    def chunk_step(c, slot, carry):
        m_prev, l_prev, acc_prev = carry
        wait(slot)
        last = c + 1 == n_chunks
        nxt_b = jnp.minimum(jnp.where(last, b + 1, b), nb - 1)
        nxt_c = jnp.where(last, 0, c + 1)
        issue(nxt_b, nxt_c, 1 - slot)

        valid = c > 0
        prev = 1 - slot
        keys_partial(slot, 0)
        s = scores(prev)
        keys_partial(slot, 1)
        p, alpha, m_next, l_next = softmax_step(s, m_prev, l_prev)
        keys_partial(slot, 2)
        acc_next = acc_prev * alpha + _dot(p.astype(BF16), cb_sc[prev])
        for sub in range(3, n_sub):
            keys_partial(slot, sub)
        rope_keys(slot)
        return (jnp.where(valid, m_next, m_prev), jnp.where(valid, l_next, l_prev),
                jnp.where(valid, acc_next, acc_prev))

    def pair_step(i, carry):
        carry = chunk_step(2 * i, 0, carry)
        return chunk_step(2 * i + 1, 1, carry)

    init = (jnp.full((H_A, 1), -jnp.inf, F32), jnp.zeros((H_A, 1), F32), jnp.zeros((H_A, KV_LORA), F32))
    m_run, l_run, acc = lax.fori_loop(0, n_chunks // 2, pair_step, init)
    p, alpha, m_run, l_run = softmax_step(scores(1), m_run, l_run)
    acc_ref[0] = acc * alpha + _dot(p.astype(BF16), cb_sc[1])
    m_ref[0] = jnp.broadcast_to(m_run, (H_A, LANE))
    l_ref[0] = jnp.broadcast_to(l_run, (H_A, LANE))

    @pl.when(b == nb - 1)
    def _():
        wait(0)


def _absorb_kernel(q_ref, gk_ref, wukh_ref, qa_ref):
    gk = gk_ref[...]
    for h in range(H_A):
        qg = q_ref[:, h * HEAD_PAD:h * HEAD_PAD + QK_NOPE] * gk[:, :QK_NOPE]
        hi = qg.astype(BF16)
        lo = (qg - hi.astype(F32)).astype(BF16)
        qa_ref[:, h * KV_LORA:(h + 1) * KV_LORA] = _dot(hi, wukh_ref[h]) + _dot(lo, wukh_ref[h])


def _absorb_queries(q, dw):
    bd = q.shape[0]
    return pl.pallas_call(
        _absorb_kernel,
        in_specs=[_whole()] * 3,
        out_specs=_whole(),
        out_shape=jax.ShapeDtypeStruct((bd, H_A * KV_LORA), F32),
        compiler_params=pltpu.CompilerParams(vmem_limit_bytes=VMEM_LIMIT),
        name="absorb_queries",
    )(q, dw['g_k128'], dw['w_uk_heads'])


def _decode_attention(page_table, q_abs, qr_mat, cache_c, cache_krt, dw):
    bd, n_pages = page_table.shape
    page = cache_c.shape[1]
    rows = DEC_PPS * page
    assert n_pages % (2 * DEC_PPS) == 0 and DEC_PPS // DEC_SUB >= 3
    const = lambda shp: pl.BlockSpec(shp, lambda b, pt: (0,) * len(shp))
    in_specs = [pl.BlockSpec((1, H_A, KV_LORA), lambda b, pt: (b, 0, 0)),
                pl.BlockSpec((1, 2 * H_A, QK_ROPE), lambda b, pt: (b, 0, 0)),
                const((1, QK_ROPE)), const((KV_LORA, H_A * QK_NOPE)),
                const((2 * H_A, LANE)), const((2 * H_A, QK_ROPE)),
                pl.BlockSpec(memory_space=pl.ANY), pl.BlockSpec(memory_space=pl.ANY)]
    out_b = lambda w: pl.BlockSpec((1, H_A, w), lambda b, pt: (b, 0, 0))
    grid_spec = pltpu.PrefetchScalarGridSpec(
        num_scalar_prefetch=1,
        grid=(bd,),
        in_specs=in_specs,
        out_specs=(out_b(KV_LORA), out_b(LANE), out_b(LANE)),
        scratch_shapes=[pltpu.VMEM((2, DEC_PPS, page, KV_LORA), F32),
                        pltpu.VMEM((2, DEC_PPS, QK_ROPE, page), F32),
                        pltpu.VMEM((2, rows, KV_LORA), BF16),
                        pltpu.VMEM((2, rows, LANE), BF16),
                        pltpu.VMEM((2, QK_ROPE, rows), BF16),
                        pltpu.VMEM((2, QK_ROPE, rows), BF16),
                        pltpu.SemaphoreType.DMA((2,)), pltpu.SemaphoreType.DMA((2,))],
    )
    return pl.pallas_call(
        _decode_kernel,
        grid_spec=grid_spec,
        out_shape=(jax.ShapeDtypeStruct((bd, H_A, KV_LORA), F32),
                   jax.ShapeDtypeStruct((bd, H_A, LANE), F32),
                   jax.ShapeDtypeStruct((bd, H_A, LANE), F32)),
        compiler_params=_cparams(("arbitrary",)),
        name="decode_attn",
    )(page_table, q_abs, qr_mat, dw['gk_rope'], dw['w_uk_perm'], dw['seg_t'], dw['ones_rows'], cache_c, cache_krt)


def _decode_final_kernel(q_ref, k_ref, c_ref, m_ref, l_ref, acc_ref, wuv_ref, o_ref):
    c_new = c_ref[...]
    for pair in range(H_A // 2):
        o_pair = None
        for h in (2 * pair, 2 * pair + 1):
            lo, hi = h * HEAD_PAD, (h + 1) * HEAD_PAD
            s_new = jnp.sum(q_ref[:, lo:hi] * k_ref[:, lo:hi].astype(F32), axis=-1, keepdims=True)
            m_old = m_ref[:, h:h + 1]
            m_new = jnp.maximum(m_old, s_new)
            a = jnp.exp2(m_old - m_new)
            pn = jnp.exp2(s_new - m_new)
            l_new = l_ref[:, h:h + 1] * a + pn
            ctx = (acc_ref[:, h * KV_LORA:(h + 1) * KV_LORA] * a + pn * c_new) / l_new
            part = _dot(ctx.astype(BF16), wuv_ref[h])
            o_pair = part if o_pair is None else o_pair + part
        o_ref[:, pair * LANE:(pair + 1) * LANE] = o_pair.astype(o_ref.dtype)


def _decode_final(q, k_new, c_new, m, l, acc, dw):
    bd = q.shape[0]
    return pl.pallas_call(
        _decode_final_kernel,
        in_specs=[_whole()] * 7,
        out_specs=_whole(),
        out_shape=jax.ShapeDtypeStruct((bd, H_A * V_HEAD), BF16),
        compiler_params=pltpu.CompilerParams(vmem_limit_bytes=VMEM_LIMIT),
        name="decode_final",
    )(q, k_new, c_new, m, l, acc, dw['w_uv_pair'])


def _merge_kernel(x_ref, attn_ref, ssm_ref, gates_ref, wa_ref, wb_ref, wo_ref, gffn_ref, wr_ref, br_ref,
                  h_ref, hn_ref, logit_ref):
    g = jax.nn.sigmoid(gates_ref[...])
    a = _dot(attn_ref[...], wa_ref[...])
    b = _dot(ssm_ref[...].astype(BF16), wb_ref[...])
    mixed = g[:, :D_MODEL] * a + g[:, D_MODEL:] * b
    h = x_ref[...] + _dot(mixed.astype(BF16), wo_ref[...])
    h_ref[...] = h
    hn = _rms(h, gffn_ref[...])
    hn_ref[...] = hn
    hi, mid, _ = _split3(hn)
    w_hi = wr_ref[0]
    w_lo = wr_ref[1]
    logit_ref[...] = _dot(hi, w_hi) + _dot(mid, w_hi) + _dot(hi, w_lo) + br_ref[...]


def _merge(x, attn, ssm, gates, mw, *, tm):
    t = x.shape[0]
    row = lambda w: pl.BlockSpec((tm, w), lambda i: (i, 0))
    consts = (mw['w_a_out'], mw['w_b_out'], mw['w_out'], mw['g_ffn'], mw['w_router2'], mw['b_router128'])
    return pl.pallas_call(
        _merge_kernel,
        grid=(t // tm,),
        in_specs=[row(D_MODEL), row(H_A * V_HEAD), row(D_INNER), row(2 * D_MODEL)] + [_whole()] * len(consts),
        out_specs=(row(D_MODEL), row(D_MODEL), row(LANE)),
        out_shape=(jax.ShapeDtypeStruct((t, D_MODEL), F32), jax.ShapeDtypeStruct((t, D_MODEL), F32),
                   jax.ShapeDtypeStruct((t, LANE), F32)),
        compiler_params=_cparams(("parallel",)),
        name="merge",
    )(x, attn, ssm, gates, *consts)


def _moe_kernel(be_ref, cnt_ref, seg_ref, tok_ref, asg_ref, x_hbm, wgu_ref, bgu_ref, wd_ref, bd_ref, y_hbm,
                xbuf, obuf, wgu_b, wd_b, sem_in, sem_out):
    i = pl.program_id(0)
    n = pl.num_programs(0)
    bm = xbuf.shape[1]
    dump0 = y_hbm.shape[0] - bm

    def in_copy(tok, slot, r):
        return pltpu.make_async_copy(x_hbm.at[tok], xbuf.at[slot, pl.ds(r, 1), :], sem_in.at[slot])

    def out_copy(row, slot, r):
        return pltpu.make_async_copy(obuf.at[slot, pl.ds(r, 1), :], y_hbm.at[pl.ds(row, 1), :], sem_out.at[slot])

    def gather_loop(blk, slot):
        base = seg_ref[blk]

        def body(r, carry):
            in_copy(tok_ref[base + r], slot, r).start()
            return carry
        lax.fori_loop(0, bm, body, 0, unroll=8)

    def gather_inline(blk, slot):
        base = seg_ref[blk]
        for r in range(bm):
            in_copy(tok_ref[base + r], slot, r).start()

    def gather_wait(slot):
        def body(r, carry):
            in_copy(0, slot, r).wait()
            return carry
        lax.fori_loop(0, bm, body, 0, unroll=8)

    def scattered(blk):
        return (blk < 0) | (cnt_ref[jnp.maximum(blk, 0)] > 0)

    def scatter_rows(blk):
        j = jnp.maximum(blk, 0)
        return seg_ref[j], jnp.where(blk >= 0, cnt_ref[j], 0)

    def scatter_loop(blk, slot):
        base, count = scatter_rows(blk)

        def body(r, carry):
            out_copy(jnp.where(r < count, asg_ref[base + r], dump0 + r), slot, r).start(priority=1)
            return carry
        lax.fori_loop(0, bm, body, 0, unroll=8)

    def scatter_inline(blk, slot):
        base, count = scatter_rows(blk)
        for r in range(bm):
            out_copy(jnp.where(count > r, asg_ref[base + r], dump0 + r), slot, r).start(priority=1)

    def scatter_wait(slot):
        def body(r, carry):
            out_copy(0, slot, r).wait()
            return carry
        lax.fori_loop(0, bm, body, 0, unroll=8)

    valid = cnt_ref[i] > 0
    nxt_valid = (i + 1 < n) & (cnt_ref[jnp.minimum(i + 1, n - 1)] > 0)

    @pl.when(i == 0)
    def _():
        obuf[1] = jnp.zeros(obuf.shape[1:], F32)

    @pl.when((i == 0) & valid)
    def _():
        gather_loop(0, 0)

    @pl.when((i >= 1) & scattered(i - 2))
    def _():
        scatter_wait(i % 2)

    @pl.when((i == 0) | (be_ref[i] != be_ref[jnp.maximum(i - 1, 0)]))
    def _():
        wgu_b[...] = wgu_ref[0].astype(BF16)
        wd_b[...] = wd_ref[0].astype(BF16)

    def expert_block(prefetch_next, slot):
        gather_wait(slot)
        x = xbuf[slot].astype(BF16)
        scatter_inline(i - 1, 1 - slot)
        if prefetch_next:
            gather_inline(i + 1, 1 - slot)
        gu = _dot(x, wgu_b[...]) + bgu_ref[0]
        gate = jnp.minimum(gu[:, :D_FF], SWIGLU_LIMIT)
        up = jnp.clip(gu[:, D_FF:], -SWIGLU_LIMIT, SWIGLU_LIMIT)
        act = (up + 1.0) * gate * jax.nn.sigmoid(SWIGLU_ALPHA * gate)
        obuf[slot] = _dot(act.astype(BF16), wd_b[...]) + bd_ref[0]

    for parity in range(2):
        on_parity = valid & (i % 2 == parity)

        @pl.when(on_parity & nxt_valid)
        def _():
            expert_block(True, parity)

        @pl.when(on_parity & jnp.logical_not(nxt_valid))
        def _():
            expert_block(False, parity)

    @pl.when(jnp.logical_not(valid) & scattered(i - 1))
    def _():
        scatter_loop(i - 1, (i + 1) % 2)

    @pl.when((i == n - 1) & scattered(i - 1))
    def _():
        scatter_wait((i + 1) % 2)


def _moe_experts(block_e, block_cnt, seg_start, tok_sorted, asg_sorted, x, w_gate_up, b_gate_up, w_down, b_down):
    n_blocks = block_e.shape[0]
    bm = MOE_BM
    n_assign = x.shape[0] * TOP_K
    x = x.reshape(x.shape[0], 1, D_MODEL)
    w_spec = lambda shp: pl.BlockSpec(shp, lambda i, be, ct, sg, tk, ag: (be[i], 0, 0))
    grid_spec = pltpu.PrefetchScalarGridSpec(
        num_scalar_prefetch=5,
        grid=(n_blocks,),
        in_specs=[pl.BlockSpec(memory_space=pl.ANY), w_spec((1, D_MODEL, 2 * D_FF)), w_spec((1, 1, 2 * D_FF)),
                  w_spec((1, D_FF, D_MODEL)), w_spec((1, 1, D_MODEL))],
        out_specs=pl.BlockSpec(memory_space=pl.ANY),
        scratch_shapes=[pltpu.VMEM((2, bm, D_MODEL), F32), pltpu.VMEM((2, bm, D_MODEL), F32),
                        pltpu.VMEM((D_MODEL, 2 * D_FF), BF16), pltpu.VMEM((D_FF, D_MODEL), BF16),
                        pltpu.SemaphoreType.DMA((2,)), pltpu.SemaphoreType.DMA((2,))],
    )
    return pl.pallas_call(
        _moe_kernel,
        grid_spec=grid_spec,
        out_shape=jax.ShapeDtypeStruct((n_assign + bm, D_MODEL), F32),
        compiler_params=_cparams(("arbitrary",)),
        name="moe_experts",
    )(block_e, block_cnt, seg_start, tok_sorted, asg_sorted, x, w_gate_up,
      b_gate_up.reshape(N_EXPERTS, 1, 2 * D_FF), w_down, b_down.reshape(N_EXPERTS, 1, D_MODEL))


def _ple_kernel(y_ref, hp_ref, hs_ref, gw_ref, pp_ref, ps_ref, gin_ref, wgate_ref, wple_ref, gple_ref,
                op_ref, os_ref):
    i = pl.program_id(0)
    n = pl.num_programs(0)
    tm = hp_ref.shape[0]
    is_sample = i == n - 1
    gw = gw_ref[...]
    y = y_ref[0:tm, :] * gw[:, 0:1]
    for k in range(1, TOP_K):
        y = y + y_ref[k * tm:(k + 1) * tm, :] * gw[:, k:k + 1]
    h2 = jnp.where(is_sample, hs_ref[...], hp_ref[...]) + y
    p_emb = jnp.where(is_sample, ps_ref[...], pp_ref[...])
    gate = jax.nn.sigmoid(_dot(_rms(h2, gin_ref[...]).astype(BF16), wgate_ref[...]))
    ple = _rms(_dot(p_emb.astype(BF16), wple_ref[...]), gple_ref[...]) * gate
    out = h2 + ple

    @pl.when(i < n - 1)
    def _():
        op_ref[...] = out

    @pl.when(i == n - 1)
    def _():
        os_ref[...] = out


def _combine_ple(y_rows, h_p, h_s, gate_w, p_p, p_s, pw):
    n_prompt = h_p.shape[0]
    tm = PLE_TM
    assert h_s.shape[0] == tm and n_prompt % tm == 0
    n_tiles = n_prompt // tm + 1
    prompt_row = lambda w: pl.BlockSpec((tm, w), lambda i: (jnp.minimum(i, n_tiles - 2), 0))
    sample_row = lambda w: pl.BlockSpec((tm, w), lambda i: (0, 0))
    return pl.pallas_call(
        _ple_kernel,
        grid=(n_tiles,),
        in_specs=[pl.BlockSpec((tm * TOP_K, D_MODEL), lambda i: (i, 0)), prompt_row(D_MODEL), sample_row(D_MODEL),
                  pl.BlockSpec((tm, LANE), lambda i: (i, 0)), prompt_row(PLE_DIM), sample_row(PLE_DIM)]
                 + [_whole()] * 4,
        out_specs=(prompt_row(D_MODEL), sample_row(D_MODEL)),
        out_shape=(jax.ShapeDtypeStruct((n_prompt, D_MODEL), F32), jax.ShapeDtypeStruct((tm, D_MODEL), F32)),
        compiler_params=_cparams(("arbitrary",)),
        name="combine_ple",
    )(y_rows, h_p, h_s, gate_w, p_p, p_s, pw['g_ple_in'], pw['w_ple_gate'], pw['w_ple'], pw['g_ple'])


def _pad_lanes(x, n):
    return jnp.pad(x, [(0, 0)] * (x.ndim - 1) + [(0, n - x.shape[-1])])


def _rot_cols(w):
    half = QK_ROPE // 2
    return jnp.concatenate([-w[..., half:], w[..., :half]], axis=-1)


def _prep_weights(g_mix_norm, w_in, g_q_a, w_q_b, g_kv_a, w_uk, w_uv, g_q_head, g_k_head, w_a_out, w_conv,
                  b_conv, dt_bias, a_log, d_skip, g_ssm_norm, w_b_out, w_out, g_ffn_norm, w_router, b_router,
                  g_ple_in, w_ple_gate, w_ple, g_ple):
    sizes = (Q_LORA, KV_LORA, QK_ROPE, D_INNER, CONV_DIM, H_B, 2 * D_MODEL)
    offs = np.concatenate([[0], np.cumsum(sizes)])
    wq, wkv, wkr, wz, wxbc, wdt, wg = [w_in[:, int(offs[i]):int(offs[i + 1])] for i in range(7)]
    zc = lambda n: jnp.zeros((D_MODEL, n), F32)
    kr128 = jnp.concatenate([zc(QK_NOPE), wkr, zc(LANE - QK_HEAD)], axis=1)
    krrot = jnp.concatenate([zc(QK_NOPE), _rot_cols(wkr), zc(LANE - QK_HEAD)], axis=1)
    w_lat = jnp.concatenate([wq, wkv, kr128, krrot, _pad_lanes(wdt, LANE)], axis=1).astype(BF16)

    wqb = w_q_b.reshape(Q_LORA, H_A, QK_HEAD)
    nope, rope_w = wqb[..., :QK_NOPE], wqb[..., QK_NOPE:]
    z_nope = jnp.zeros_like(nope)
    q128 = _pad_lanes(jnp.concatenate([nope, rope_w], axis=-1), LANE).reshape(Q_LORA, H_A * LANE)
    qrot = _pad_lanes(jnp.concatenate([z_nope, _rot_cols(rope_w)], axis=-1), LANE).reshape(Q_LORA, H_A * LANE)
    w_qb = jnp.concatenate([q128, qrot], axis=1).astype(BF16)

    wuk3 = w_uk.reshape(KV_LORA, H_A, QK_NOPE)
    w_uk_pad = _pad_lanes(wuk3, LANE).reshape(KV_LORA, H_A * LANE).astype(BF16)
    sub = LANE // H_A
    w_uk_perm = wuk3.reshape(KV_LORA, H_A, QK_NOPE // sub, sub).transpose(0, 2, 1, 3)
    w_uk_perm = w_uk_perm.reshape(KV_LORA, H_A * QK_NOPE).astype(BF16)

    wuv3 = w_uv.reshape(KV_LORA, H_A, V_HEAD)
    even = (jnp.arange(H_A) % 2 == 0)[None, :, None]
    zv = jnp.zeros_like(wuv3)
    w_uv_ext = jnp.where(even, jnp.concatenate([wuv3, zv], axis=-1), jnp.concatenate([zv, wuv3], axis=-1))
    ones_lane = np.zeros((H_A, LANE), np.float32)
    ones_lane[0::2, V_HEAD] = 1.0
    ones_lane[1::2, 0] = 1.0
    front = dict(
        g_mix=g_mix_norm.reshape(1, -1), w_lat=w_lat, w_z=wz.astype(BF16), w_xbc=wxbc.astype(BF16),
        w_g=wg.astype(BF16), g_q_a=g_q_a.reshape(1, -1), w_qb=w_qb, g_kv_a=g_kv_a.reshape(1, -1),
        w_uk_pad=w_uk_pad, w_uv_ext=w_uv_ext.reshape(KV_LORA, H_A * LANE).astype(BF16),
        v_ones=jnp.asarray(ones_lane.reshape(1, H_A * LANE)),
        g_q128=_pad_lanes(g_q_head.reshape(1, -1), LANE) * Q_SCALE,
        g_k128=_pad_lanes(g_k_head.reshape(1, -1), LANE),
    )

    lane_head = np.arange(LANE) // sub
    seg = (lane_head[None, :] == np.arange(H_A)[:, None]).astype(np.float32)
    seg_t = np.zeros((2 * H_A, LANE), np.float32)
    seg_t[:H_A] = seg
    ones_rows = np.zeros((2 * H_A, QK_ROPE), np.float32)
    ones_rows[:H_A] = 1.0
    w_uv_pair = w_uv_ext
    decode = dict(
        g_k128=front['g_k128'], gk_rope=g_k_head[QK_NOPE:].reshape(1, QK_ROPE), w_uk_perm=w_uk_perm,
        w_uk_heads=wuk3.transpose(1, 2, 0).astype(BF16),
        seg_t=jnp.asarray(seg_t, BF16), ones_rows=jnp.asarray(ones_rows, BF16),
        w_uv_pair=w_uv_pair.transpose(1, 0, 2).astype(BF16),
    )

    expand = (np.arange(D_INNER)[None, :] // SSM_HEAD == np.arange(LANE)[:, None]).astype(np.float32)
    tri = (np.arange(SSD_CHUNK)[None, :] <= np.arange(SSD_CHUNK)[:, None]).astype(np.float32)
    ssm = dict(
        w_conv=w_conv, b_conv=b_conv.reshape(1, -1), dt_bias128=_pad_lanes(dt_bias.reshape(1, -1), LANE),
        a_head128=_pad_lanes(-jnp.exp(a_log).reshape(1, -1), LANE),
        d_skip_x=jnp.repeat(d_skip, SSM_HEAD).reshape(1, -1), g_norm=g_ssm_norm.reshape(1, -1),
        tri=jnp.asarray(tri, BF16), expand=jnp.asarray(expand, BF16),
    )

    wr = _pad_lanes(w_router, LANE)
    wr_hi = wr.astype(BF16)
    wr_lo = (wr - wr_hi.astype(F32)).astype(BF16)
    merge = dict(
        w_a_out=w_a_out.astype(BF16), w_b_out=w_b_out.astype(BF16), w_out=w_out.astype(BF16),
        g_ffn=g_ffn_norm.reshape(1, -1), w_router2=jnp.stack([wr_hi, wr_lo]),
        b_router128=_pad_lanes(b_router.reshape(1, -1), LANE),
    )
    ple = dict(g_ple_in=g_ple_in.reshape(1, -1), w_ple_gate=w_ple_gate.astype(BF16), w_ple=w_ple.astype(BF16),
               g_ple=g_ple.reshape(1, -1))
    return front, decode, ssm, merge, ple


def _rope_tables(pos):
    half = QK_ROPE // 2
    inv = ROPE_THETA ** (-jnp.arange(half, dtype=F32) / half)
    ang = pos.astype(F32)[:, None] * inv[None, :]
    cos, sin = jnp.cos(ang), jnp.sin(ang)
    n = pos.shape[0]
    cos128 = jnp.concatenate([jnp.ones((n, QK_NOPE), F32), cos, cos, jnp.ones((n, LANE - QK_HEAD), F32)], axis=1)
    sin128 = jnp.concatenate([jnp.zeros((n, QK_NOPE), F32), sin, sin, jnp.zeros((n, LANE - QK_HEAD), F32)], axis=1)
    return cos128, sin128


def _route(logits, n_tok):
    top_val, top_idx = lax.top_k(logits[:, :N_EXPERTS], TOP_K)
    gate_w = jax.nn.softmax(top_val, axis=-1)
    n_assign = n_tok * TOP_K
    flat_e = top_idx.reshape(-1).astype(jnp.int32)
    counts = jnp.sum((flat_e[:, None] == jnp.arange(N_EXPERTS, dtype=jnp.int32)[None, :]).astype(jnp.int32), axis=0)
    bm = MOE_BM
    padded = (counts + bm - 1) // bm * bm
    pend = jnp.cumsum(padded)
    pstarts = pend - padded
    starts = jnp.cumsum(counts) - counts
    idx_bits = max(1, (n_assign - 1).bit_length())
    key = jnp.sort(flat_e * (1 << idx_bits) + jnp.arange(n_assign, dtype=jnp.int32))
    asg = jnp.pad(key & ((1 << idx_bits) - 1), (0, bm)).astype(jnp.int32)
    tok_sorted = asg // TOP_K
    tile_rows = PLE_TM * TOP_K
    asg_sorted = (asg // tile_rows) * tile_rows + (asg % TOP_K) * PLE_TM + tok_sorted % PLE_TM
    n_blocks = -(-n_assign // bm) + N_EXPERTS
    blk_start = jnp.arange(n_blocks, dtype=jnp.int32) * bm
    block_e = jnp.minimum(jnp.sum((pend[None, :] <= blk_start[:, None]).astype(jnp.int32), axis=1), N_EXPERTS - 1)
    offset = blk_start - pstarts[block_e]
    block_cnt = jnp.where(blk_start < pend[-1], jnp.clip(counts[block_e] - offset, 0, bm), 0).astype(jnp.int32)
    seg_start = jnp.clip(starts[block_e] + offset, 0, n_assign).astype(jnp.int32)
    return gate_w, tok_sorted, asg_sorted, block_e.astype(jnp.int32), block_cnt, seg_start


def kernel(x_prompt, x_sample, p_prompt, p_sample, cache_kv_latent, cache_k_rope, page_table, state_conv, state_ssm, g_mix_norm, w_in, g_q_a, w_q_b, g_kv_a, w_uk, w_uv, g_q_head, g_k_head, w_a_out, w_conv, b_conv, dt_bias, a_log, d_skip, g_ssm_norm, w_b_out, w_out, g_ffn_norm, w_router, b_router, w_gate_up, b_gate_up, w_down, b_down, g_ple_in, w_ple_gate, w_ple, g_ple):
    depth = g_mix_norm.shape[0]
    assert depth == 1, "one layer"
    b, s, _ = x_prompt.shape
    bd, sd, _ = x_sample.shape
    assert sd == 1, "one new token per sample sequence"
    n_pages = page_table.shape[1]
    page_size = cache_kv_latent.shape[2]
    assert n_pages % DEC_PPS == 0 and s % SSD_CHUNK == 0
    tp = b * s

    fw, dw, sw, mw, pw = _prep_weights(
        g_mix_norm[0], w_in[0], g_q_a[0], w_q_b[0], g_kv_a[0], w_uk[0], w_uv[0], g_q_head[0], g_k_head[0],
        w_a_out[0], w_conv[0], b_conv[0], dt_bias[0], a_log[0], d_skip[0], g_ssm_norm[0], w_b_out[0], w_out[0],
        g_ffn_norm[0], w_router[0], b_router[0], g_ple_in[0], w_ple_gate[0], w_ple[0], g_ple[0])

    xp = x_prompt.reshape(tp, D_MODEL)
    cos_p, sin_p = _rope_tables(jnp.arange(s))
    tm_p = min(FRONT_TM, s)
    bound = QK_HEAD * jnp.max(jnp.abs(fw['g_q128'])) * jnp.max(jnp.abs(fw['g_k128']))
    bound = (1.02 * bound + 1.0).astype(BF16).astype(F32)
    bias_lane = (jnp.arange(LANE) == QK_HEAD).astype(F32).reshape(1, LANE)
    q_p, k_p, v_p, c_p, kr_p, z_p, xbc_p, dt_p, gates_p = _front(
        xp, cos_p, sin_p, fw, -bound * bias_lane, bias_lane, tm=tm_p, q_dtype=BF16, pos_blocks=s // tm_p)
    qkv = (q_p.reshape(b, s, -1), k_p.reshape(b, s, -1), v_p.reshape(b, s, -1))
    attn_p = lax.cond(bound <= ATTN_MAX_BOUND,
                      lambda q, k, v: _prompt_attention(q, k, v, b, s, bounded=True),
                      lambda q, k, v: _prompt_attention(q, k, v, b, s, bounded=False), *qkv)
    ssm_p, hfin_p = _ssd_prompt(xbc_p, z_p, dt_p, sw, b, s)
    h_p, hn_p, logit_p = _merge(xp, attn_p.reshape(tp, -1), ssm_p, gates_p, mw, tm=min(MERGE_TM, tp))

    xs = x_sample.reshape(bd, D_MODEL)
    cos_s, sin_s = _rope_tables(jnp.full((bd,), n_pages * page_size, jnp.int32))
    no_pad = jnp.zeros((1, LANE), F32)
    q_s, k_s, _, c_s, kr_s, z_s, xbc_s, dt_s, gates_s = _front(
        xs, cos_s, sin_s, fw, no_pad, no_pad, tm=bd, q_dtype=F32, pos_blocks=1)
    q3 = q_s.reshape(bd, H_A, LANE)
    q_abs = _absorb_queries(q_s, dw).reshape(bd, H_A, KV_LORA)
    qr = q3[:, :, QK_NOPE:QK_HEAD]
    qr_mat = jnp.concatenate([jnp.zeros_like(qr), qr], axis=1)
    acc, m_run, l_run = _decode_attention(page_table, q_abs, qr_mat, cache_kv_latent[0],
                                          jnp.swapaxes(cache_k_rope[0], 1, 2), dw)
    attn_s = _decode_final(q_s, k_s, c_s, m_run[:, :, 0], l_run[:, :, 0], acc.reshape(bd, -1), dw)
    ssm_s, hnew_s = _ssm_sample(xbc_s, state_conv[0], z_s, dt_s,
                                state_ssm[0].reshape(bd, D_INNER, D_STATE), sw)
    h_s, hn_s, logit_s = _merge(xs, attn_s, ssm_s, gates_s, mw, tm=bd)

    hn_all = jnp.concatenate([hn_p, hn_s], axis=0)
    logits = jnp.concatenate([logit_p, logit_s], axis=0)
    n_tok = tp + bd
    gate_w, tok_sorted, asg_sorted, block_e, block_cnt, seg_start = _route(logits, n_tok)
    y_rows = _moe_experts(block_e, block_cnt, seg_start, tok_sorted, asg_sorted, hn_all, w_gate_up[0],
                          b_gate_up[0], w_down[0], b_down[0])
    out_p, out_s = _combine_ple(y_rows, h_p, h_s, _pad_lanes(gate_w, LANE), p_prompt[0].reshape(tp, PLE_DIM),
                                p_sample[0].reshape(bd, PLE_DIM), pw)

    y_prompt = out_p.reshape(b, s, D_MODEL)
    y_sample = out_s.reshape(bd, sd, D_MODEL)
    new_c_p = c_p.reshape(1, b, s, KV_LORA)
    new_kr_p = kr_p[:, QK_NOPE:QK_HEAD].reshape(1, b, s, QK_ROPE)
    conv_p = xbc_p.reshape(b, s, CONV_DIM)[:, s - (CONV_W - 1):].reshape(1, b, CONV_W - 1, CONV_DIM)
    ssm_state_p = hfin_p.reshape(1, b, H_B, SSM_HEAD, D_STATE)
    new_c_s = c_s.reshape(1, bd, sd, KV_LORA)
    new_kr_s = kr_s[:, QK_NOPE:QK_HEAD].reshape(1, bd, sd, QK_ROPE)
    conv_s = jnp.concatenate([state_conv[0][:, 1:], xbc_s[:, None, :]], axis=1).reshape(1, bd, CONV_W - 1, CONV_DIM)
    ssm_state_s = hnew_s.reshape(1, bd, H_B, SSM_HEAD, D_STATE)
    return (y_prompt, y_sample, new_c_p, new_kr_p, conv_p, ssm_state_p, new_c_s, new_kr_s, conv_s, ssm_state_s)
```

```python
import functools
import math

import jax
import jax.numpy as jnp
import numpy as np
from jax import lax
from jax.experimental import pallas as pl
from jax.experimental.pallas import tpu as pltpu

F32 = jnp.float32
BF16 = jnp.bfloat16

D_MODEL = 1024
H_A = 16
Q_LORA = 384
KV_LORA = 256
QK_NOPE = 64
QK_ROPE = 32
QK_HEAD = QK_NOPE + QK_ROPE
V_HEAD = 64
ROPE_THETA = 10000.0
D_INNER = 2 * D_MODEL
SSM_HEAD = 64
H_B = D_INNER // SSM_HEAD
N_GROUPS = 4
D_STATE = 128
CONV_W = 4
CONV_DIM = D_INNER + 2 * N_GROUPS * D_STATE
SSD_CHUNK = 128
N_EXPERTS = 32
TOP_K = 4
D_FF = D_MODEL
SWIGLU_LIMIT = 7.0
SWIGLU_ALPHA = 1.702
PLE_DIM = 256
EPS = 1e-6

LANE = 128
HEAD_PAD = LANE
GROUP_W = D_INNER // N_GROUPS
HEADS_PER_GROUP = H_B // N_GROUPS
Q_SCALE = QK_HEAD ** -0.5 * math.log2(math.e)
VMEM_LIMIT = 56 * 1024 * 1024

FRONT_TM = 256
ATTN_TQ = 1024
ATTN_MAX_BOUND = 40.0
MERGE_TM = 256
MOE_BM = 256
PLE_TM = 128
DEC_PPS = 16
DEC_SUB = 4


def _dot(a, b):
    return jnp.dot(a, b, preferred_element_type=F32)


def _dot_nt(a, b):
    return lax.dot_general(a, b, (((1,), (1,)), ((), ())), preferred_element_type=F32)


def _dot_tn(a, b):
    return lax.dot_general(a, b, (((0,), (0,)), ((), ())), preferred_element_type=F32)


def _split3(x):
    hi = x.astype(BF16)
    r1 = x - hi.astype(F32)
    mid = r1.astype(BF16)
    lo = (r1 - mid.astype(F32)).astype(BF16)
    return hi, mid, lo


def _dot_f32_lhs(x, e, terms=3):
    parts = _split3(x)[:terms]
    out = _dot(parts[0], e)
    for part in parts[1:]:
        out = out + _dot(part, e)
    return out


def _dot_f32_rhs(e, x):
    hi, mid, lo = _split3(x)
    return _dot(e, hi) + _dot(e, mid) + _dot(e, lo)


def _rms(x, g):
    return x * lax.rsqrt(jnp.mean(x * x, axis=-1, keepdims=True) + EPS) * g


def _silu(x):
    return x * jax.nn.sigmoid(x)


def _cparams(sem, vmem=VMEM_LIMIT):
    return pltpu.CompilerParams(dimension_semantics=sem, vmem_limit_bytes=vmem)


def _whole():
    return pl.BlockSpec(memory_space=pltpu.VMEM)


def _head_norm(xh, g):
    ss = jnp.sum(xh * xh, axis=-1, keepdims=True)
    return xh * lax.rsqrt(ss * (1.0 / QK_HEAD) + EPS) * g


def _front_kernel(x_ref, cos_ref, sin_ref, gmix_ref, wlat_ref, wz_ref, wxbc_ref, wg_ref,
                  gqa_ref, wqb_ref, gkva_ref, wuk_ref, wuv_ref, gq_ref, gk_ref, qpad_ref, kpad_ref, vpad_ref,
                  q_ref, k_ref, v_ref, c_ref, kr_ref, z_ref, xbc_ref, dt_ref, gates_ref):
    x = x_ref[...]
    ub = _rms(x, gmix_ref[...]).astype(BF16)
    z_ref[...] = _dot(ub, wz_ref[...])
    xbc_ref[...] = _dot(ub, wxbc_ref[...])
    gates_ref[...] = _dot(ub, wg_ref[...])
    lat = _dot(ub, wlat_ref[...])
    q_lat = lat[:, :Q_LORA]
    kv_lat = lat[:, Q_LORA:Q_LORA + KV_LORA]
    o = Q_LORA + KV_LORA
    kr_raw = lat[:, o:o + LANE]
    kr_rot = lat[:, o + LANE:o + 2 * LANE]
    dt_ref[...] = lat[:, o + 2 * LANE:o + 3 * LANE]
    cos = cos_ref[...]
    sin = sin_ref[...]
    c = _rms(kv_lat, gkva_ref[...])
    c_ref[...] = c
    kr = kr_raw * cos + kr_rot * sin
    kr_ref[...] = kr
    qn = _rms(q_lat, gqa_ref[...]).astype(BF16)
    q2 = _dot(qn, wqb_ref[...])
    nq = H_A * HEAD_PAD
    gq = gq_ref[...]
    qpad = qpad_ref[...]
    kpad = kpad_ref[...]
    for h in range(H_A):
        lo, hi = h * HEAD_PAD, (h + 1) * HEAD_PAD
        qh = q2[:, lo:hi] * cos + q2[:, nq + lo:nq + hi] * sin
        q_ref[:, lo:hi] = (_head_norm(qh, gq) + qpad).astype(q_ref.dtype)
    cb = c.astype(BF16)
    kn = _dot(cb, wuk_ref[...])
    gk = gk_ref[...]
    for h in range(H_A):
        lo, hi = h * HEAD_PAD, (h + 1) * HEAD_PAD
        k_ref[:, lo:hi] = (_head_norm(kn[:, lo:hi] + kr, gk) + kpad).astype(k_ref.dtype)
    v_ref[...] = (_dot(cb, wuv_ref[...]) + vpad_ref[...]).astype(v_ref.dtype)


def _front(x, cos, sin, fw, qpad, kpad, *, tm, q_dtype, pos_blocks):
    t = x.shape[0]
    nt = t // tm
    row = lambda w: pl.BlockSpec((tm, w), lambda i: (i, 0))
    pos_spec = pl.BlockSpec((tm, LANE), lambda i: (i % pos_blocks, 0))
    nq = H_A * HEAD_PAD
    out_shape = (
        jax.ShapeDtypeStruct((t, nq), q_dtype),
        jax.ShapeDtypeStruct((t, nq), BF16),
        jax.ShapeDtypeStruct((t, nq), BF16),
        jax.ShapeDtypeStruct((t, KV_LORA), F32),
        jax.ShapeDtypeStruct((t, LANE), F32),
        jax.ShapeDtypeStruct((t, D_INNER), F32),
        jax.ShapeDtypeStruct((t, CONV_DIM), F32),
        jax.ShapeDtypeStruct((t, LANE), F32),
        jax.ShapeDtypeStruct((t, 2 * D_MODEL), F32),
    )
    out_specs = (row(nq), row(nq), row(nq), row(KV_LORA), row(LANE), row(D_INNER),
                 row(CONV_DIM), row(LANE), row(2 * D_MODEL))
    weights = (fw['g_mix'], fw['w_lat'], fw['w_z'], fw['w_xbc'], fw['w_g'], fw['g_q_a'], fw['w_qb'],
               fw['g_kv_a'], fw['w_uk_pad'], fw['w_uv_ext'], fw['g_q128'], fw['g_k128'], qpad, kpad, fw['v_ones'])
    return pl.pallas_call(
        _front_kernel,
        grid=(nt,),
        in_specs=[row(D_MODEL), pos_spec, pos_spec] + [_whole()] * len(weights),
        out_specs=out_specs,
        out_shape=out_shape,
        compiler_params=_cparams(("parallel",)),
        name="front",
    )(x, cos, sin, *weights)


def _attn_kernel(qi_ref, ki_ref, q_ref, k_ref, v_ref, o_ref, acc_sc, *m_scratch, bounded):
    step = pl.program_id(2)
    qi = qi_ref[step]
    ki = ki_ref[step]
    tq = q_ref.shape[0]
    half = tq // 2

    @pl.when(ki == 0)
    def _():
        acc_sc[...] = jnp.zeros(acc_sc.shape, F32)
        if not bounded:
            m_scratch[0][...] = jnp.full(m_scratch[0].shape, -jnp.inf, F32)

    def update(q0, qn, k0, kn, masked):
        for h in range(2):
            cols = slice(h * HEAD_PAD, (h + 1) * HEAD_PAD)
            s = _dot_nt(q_ref[q0:q0 + qn, cols], k_ref[k0:k0 + kn, cols])
            v = v_ref[k0:k0 + kn, cols]
            if masked:
                keep = (lax.broadcasted_iota(jnp.int32, (qn, kn), 1) <= lax.broadcasted_iota(jnp.int32, (qn, kn), 0))
            if bounded:
                p = jnp.exp2(s)
                if masked:
                    p = jnp.where(keep, p, 0.0)
                acc_sc[h, q0:q0 + qn, :] += _dot(p.astype(BF16), v)
            else:
                m_sc = m_scratch[0]
                if masked:
                    s = jnp.where(keep, s, -jnp.inf)
                m_prev = m_sc[h, q0:q0 + qn, :]
                m_next = jnp.maximum(m_prev, jnp.max(s, axis=-1, keepdims=True))
                alpha = jnp.exp2(m_prev - m_next)
                p = jnp.exp2(s - m_next[:, :1])
                acc_sc[h, q0:q0 + qn, :] = alpha * acc_sc[h, q0:q0 + qn, :] + _dot(p.astype(BF16), v)
                m_sc[h, q0:q0 + qn, :] = m_next

    @pl.when(ki < qi)
    def _():
        update(0, tq, 0, tq, False)

    @pl.when(ki == qi)
    def _():
        update(0, half, 0, half, True)
        update(half, half, 0, half, False)
        update(half, half, half, half, True)
        lane = lax.broadcasted_iota(jnp.int32, (tq, LANE), 1)
        a0 = acc_sc[0]
        a1 = acc_sc[1]
        o0 = a0 / a0[:, V_HEAD:V_HEAD + 1]
        o1 = a1 / a1[:, 0:1]
        o_ref[...] = jnp.where(lane < V_HEAD, o0, o1).astype(o_ref.dtype)


def _prompt_attention(q, k, v, b, s, *, bounded):
    tq = min(ATTN_TQ, s)
    nq = s // tq
    pairs = [(i, j) for i in range(nq) for j in range(i + 1)]
    qi_tab = jnp.asarray([p[0] for p in pairs], jnp.int32)
    ki_tab = jnp.asarray([p[1] for p in pairs], jnp.int32)
    blk = lambda sel: pl.BlockSpec((None, tq, 2 * HEAD_PAD), lambda bi, h, t, qt, kt: (bi, sel(qt, kt)[t], h))
    scratch = [pltpu.VMEM((2, tq, LANE), F32)]
    if not bounded:
        scratch.append(pltpu.VMEM((2, tq, LANE), F32))
    grid_spec = pltpu.PrefetchScalarGridSpec(
        num_scalar_prefetch=2,
        grid=(b, H_A // 2, len(pairs)),
        in_specs=[blk(lambda qt, kt: qt), blk(lambda qt, kt: kt), blk(lambda qt, kt: kt)],
        out_specs=pl.BlockSpec((None, tq, 2 * V_HEAD), lambda bi, h, t, qt, kt: (bi, qt[t], h)),
        scratch_shapes=scratch,
    )
    return pl.pallas_call(
        functools.partial(_attn_kernel, bounded=bounded),
        grid_spec=grid_spec,
        out_shape=jax.ShapeDtypeStruct((b, s, H_A * V_HEAD), BF16),
        compiler_params=_cparams(("parallel", "parallel", "arbitrary")),
        name="prompt_attn_bounded" if bounded else "prompt_attn",
    )(qi_tab, ki_tab, q, k, v)


def _ssd_kernel(xbc_ref, z_ref, dt_ref, wconv_ref, bconv_ref, dtb_ref, ah_ref, dskip_ref, gn_ref,
                tri_ref, exp_ref, y_ref, hfin_ref, xbuf, state):
    ci = pl.program_id(1)
    nc = pl.num_programs(1)
    L = SSD_CHUNK

    @pl.when(ci == 0)
    def _():
        xbuf[0:8, :] = jnp.zeros((8, CONV_DIM), F32)
        state[...] = jnp.zeros(state.shape, F32)

    xbuf[8:8 + L, :] = xbc_ref[...]
    conv = bconv_ref[...] + xbuf[8:8 + L, :] * wconv_ref[3:4, :]
    for w in range(CONV_W - 1):
        sh = CONV_W - 1 - w
        conv = conv + xbuf[8 - sh:8 - sh + L, :] * wconv_ref[w:w + 1, :]
    xbuf[0:8, :] = xbuf[L:L + 8, :]
    xc = _silu(conv)
    xs = xc[:, :D_INNER]

    dt = jax.nn.softplus(dt_ref[...] + dtb_ref[...])
    a = dt * ah_ref[...]
    a_cum = _dot_f32_rhs(tri_ref[...], a)
    a_cum_t = a_cum.T
    a_last = a_cum[L - 1:L, :]
    ex = exp_ref[...]
    dt_x = _dot_f32_lhs(dt, ex, terms=2)
    dfs_x = _dot_f32_lhs(jnp.exp(a_cum), ex, terms=2)
    dte_x = _dot_f32_lhs(jnp.exp(a_last - a_cum), ex, terms=2)
    xdt = xs * dt_x
    xdt_b = xdt.astype(BF16)
    xw_b = (xdt * dte_x).astype(BF16)
    chunk_decay = jnp.exp(a_cum_t[:, L - 1:L])

    row = lax.broadcasted_iota(jnp.int32, (L, L), 0)
    col = lax.broadcasted_iota(jnp.int32, (L, L), 1)
    causal = col <= row
    lane = lax.broadcasted_iota(jnp.int32, (L, LANE), 1)
    for g in range(N_GROUPS):
        bm = xc[:, D_INNER + g * D_STATE:D_INNER + (g + 1) * D_STATE].astype(BF16)
        cm = xc[:, D_INNER + (N_GROUPS + g) * D_STATE:D_INNER + (N_GROUPS + g + 1) * D_STATE].astype(BF16)
        cb = _dot_nt(cm, bm)
        c0, c1 = g * GROUP_W, (g + 1) * GROUP_W
        st_prev = state[c0:c1, :]
        y_off = _dot_nt(cm, st_prev.astype(BF16)) * dfs_x[:, c0:c1]
        for j in range(HEADS_PER_GROUP // 2):
            h0 = g * HEADS_PER_GROUP + 2 * j
            x2 = xdt_b[:, h0 * SSM_HEAD:(h0 + 2) * SSM_HEAD]
            ys = []
            for hh in (h0, h0 + 1):
                seg = a_cum[:, hh:hh + 1] - a_cum_t[hh:hh + 1, :]
                sc = jnp.where(causal, cb * jnp.exp(seg), 0.0)
                ys.append(_dot(sc.astype(BF16), x2))
            y2 = jnp.where(lane < SSM_HEAD, ys[0], ys[1])
            lo = h0 * SSM_HEAD
            y_ref[:, lo:lo + LANE] = y2 + y_off[:, lo - c0:lo - c0 + LANE]
        st_new = _dot_tn(xw_b[:, c0:c1], bm)
        carry = jnp.concatenate(
            [jnp.broadcast_to(chunk_decay[hh:hh + 1, :], (SSM_HEAD, D_STATE))
             for hh in range(g * HEADS_PER_GROUP, (g + 1) * HEADS_PER_GROUP)], axis=0)
        state[c0:c1, :] = st_prev * carry + st_new

    y = y_ref[...] + dskip_ref[...] * xs
    yg = y * _silu(z_ref[...])
    gn = gn_ref[...]
    for g in range(N_GROUPS):
        c0, c1 = g * GROUP_W, (g + 1) * GROUP_W
        blk = yg[:, c0:c1]
        y_ref[:, c0:c1] = blk * lax.rsqrt(jnp.mean(blk * blk, axis=-1, keepdims=True) + EPS) * gn[:, c0:c1]

    @pl.when(ci == nc - 1)
    def _():
        hfin_ref[...] = state[...]


def _ssd_prompt(xbc, z, dt, sw, b, s):
    nc = s // SSD_CHUNK
    row = lambda w: pl.BlockSpec((SSD_CHUNK, w), lambda bi, ci: (bi * nc + ci, 0))
    consts = (sw['w_conv'], sw['b_conv'], sw['dt_bias128'], sw['a_head128'], sw['d_skip_x'], sw['g_norm'],
              sw['tri'], sw['expand'])
    return pl.pallas_call(
        _ssd_kernel,
        grid=(b, nc),
        in_specs=[row(CONV_DIM), row(D_INNER), row(LANE)] + [_whole()] * len(consts),
        out_specs=(row(D_INNER), pl.BlockSpec((None, D_INNER, D_STATE), lambda bi, ci: (bi, 0, 0))),
        out_shape=(jax.ShapeDtypeStruct((b * s, D_INNER), F32),
                   jax.ShapeDtypeStruct((b, D_INNER, D_STATE), F32)),
        scratch_shapes=[pltpu.VMEM((SSD_CHUNK + 8, CONV_DIM), F32), pltpu.VMEM((D_INNER, D_STATE), F32)],
        compiler_params=_cparams(("parallel", "arbitrary")),
        name="ssd_prompt",
    )(xbc, z, dt, *consts)


def _ssm_step_kernel(xbc_ref, cbuf_ref, z_ref, dt_ref, h_ref, wconv_ref, bconv_ref, dtb_ref, ah_ref,
                     dskip_ref, gn_ref, exp_ref, y_ref, hnew_ref):
    conv = bconv_ref[...] + xbc_ref[0] * wconv_ref[3:4, :]
    for w in range(CONV_W - 1):
        conv = conv + cbuf_ref[0, w:w + 1, :] * wconv_ref[w:w + 1, :]
    xc = _silu(conv)
    xs = xc[:, :D_INNER]
    dt = jax.nn.softplus(dt_ref[0] + dtb_ref[...])
    da = jnp.exp(dt * ah_ref[...])
    ex = exp_ref[...]
    xdt = xs * _dot_f32_lhs(jnp.broadcast_to(dt, (8, LANE)), ex)[0:1, :]

    eye = lax.broadcasted_iota(jnp.int32, (LANE, LANE), 0) == lax.broadcasted_iota(jnp.int32, (LANE, LANE), 1)

    def to_col(rowvec):
        return jnp.sum(jnp.where(eye, jnp.broadcast_to(rowvec, (LANE, LANE)), 0.0), axis=-1, keepdims=True)

    ys = []
    for g in range(N_GROUPS):
        bt = xc[:, D_INNER + g * D_STATE:D_INNER + (g + 1) * D_STATE]
        ct = xc[:, D_INNER + (N_GROUPS + g) * D_STATE:D_INNER + (N_GROUPS + g + 1) * D_STATE]
        ct8 = jnp.broadcast_to(ct, (8, D_STATE)).astype(BF16)
        for j in range(GROUP_W // LANE):
            r0 = g * GROUP_W + j * LANE
            h0 = r0 // SSM_HEAD
            h_old = h_ref[0, r0:r0 + LANE, :]
            decay = jnp.concatenate([jnp.broadcast_to(da[:, hh:hh + 1], (SSM_HEAD, D_STATE)) for hh in (h0, h0 + 1)],
                                    axis=0)
            h_new = h_old * decay + to_col(xdt[:, r0:r0 + LANE]) * bt
            hnew_ref[0, r0:r0 + LANE, :] = h_new
            ys.append(_dot_nt(ct8, h_new.astype(BF16))[0:1, :])
    y = jnp.concatenate(ys, axis=-1) + dskip_ref[...] * xs
    yg = y * _silu(z_ref[0])
    gn = gn_ref[...]
    for g in range(N_GROUPS):
        c0, c1 = g * GROUP_W, (g + 1) * GROUP_W
        blk = yg[:, c0:c1]
        y_ref[0, :, c0:c1] = blk * lax.rsqrt(jnp.mean(blk * blk, axis=-1, keepdims=True) + EPS) * gn[:, c0:c1]


def _ssm_sample(xbc, conv_buf, z, dt, h, sw):
    bd = xbc.shape[0]
    vec = lambda w: pl.BlockSpec((1, 1, w), lambda i: (i, 0, 0))
    consts = (sw['w_conv'], sw['b_conv'], sw['dt_bias128'], sw['a_head128'], sw['d_skip_x'], sw['g_norm'],
              sw['expand'])
    y, h_new = pl.pallas_call(
        _ssm_step_kernel,
        grid=(bd,),
        in_specs=[vec(CONV_DIM), pl.BlockSpec((1, CONV_W - 1, CONV_DIM), lambda i: (i, 0, 0)), vec(D_INNER),
                  vec(LANE), pl.BlockSpec((1, D_INNER, D_STATE), lambda i: (i, 0, 0))] + [_whole()] * len(consts),
        out_specs=(vec(D_INNER), pl.BlockSpec((1, D_INNER, D_STATE), lambda i: (i, 0, 0))),
        out_shape=(jax.ShapeDtypeStruct((bd, 1, D_INNER), F32),
                   jax.ShapeDtypeStruct((bd, D_INNER, D_STATE), F32)),
        compiler_params=_cparams(("parallel",)),
        name="ssm_sample",
    )(xbc.reshape(bd, 1, CONV_DIM), conv_buf, z.reshape(bd, 1, D_INNER), dt.reshape(bd, 1, LANE), h, *consts)
    return y.reshape(bd, D_INNER), h_new


def _decode_kernel(pt_ref, qa_ref, qr_ref, gkr_ref, wuk_ref, segt_ref, ones_ref, cache_c, cache_kr,
                   acc_ref, m_ref, l_ref, cpage, krpage, cb_sc, pcat_sc, krb_sc, kr2b_sc, sem_c, sem_kr):
    b = pl.program_id(0)
    nb = pl.num_programs(0)
    n_chunks = pt_ref.shape[1] // DEC_PPS
    page = cpage.shape[2]
    sub_rows = DEC_SUB * page
    n_sub = DEC_PPS // DEC_SUB
    nj = (H_A * QK_NOPE) // LANE

    def page_copies(pid, slot, i):
        return (pltpu.make_async_copy(cache_c.at[pid], cpage.at[slot, i], sem_c.at[slot]),
                pltpu.make_async_copy(cache_kr.at[pid], krpage.at[slot, i], sem_kr.at[slot]))

    def issue(bb, c, slot):
        for i in range(DEC_PPS):
            for cp in page_copies(pt_ref[bb, c * DEC_PPS + i], slot, i):
                cp.start()

    def wait(slot):
        for i in range(DEC_PPS):
            for cp in page_copies(0, slot, i):
                cp.wait()

    @pl.when(b == 0)
    def _():
        cb_sc[1] = jnp.zeros(cb_sc.shape[1:], BF16)
        pcat_sc[1] = jnp.zeros(pcat_sc.shape[1:], BF16)
        krb_sc[1] = jnp.zeros(krb_sc.shape[1:], BF16)
        kr2b_sc[1] = jnp.zeros(kr2b_sc.shape[1:], BF16)
        issue(0, 0, 0)

    qr = (qr_ref[0] * gkr_ref[...]).astype(BF16)
    q_abs = qa_ref[0]
    qa_hi = q_abs.astype(BF16)
    qa_lo = (q_abs - qa_hi.astype(F32)).astype(BF16)
    qa2 = jnp.concatenate([qa_hi, qa_lo], axis=0)

    def keys_partial(slot, sub):
        cb = jnp.concatenate([cpage[slot, sub * DEC_SUB + t].astype(BF16) for t in range(DEC_SUB)], axis=0)
        r0 = sub * sub_rows
        cb_sc[slot, r0:r0 + sub_rows, :] = cb
        kk = _dot(cb, wuk_ref[...])
        p_sq = None
        for jj in range(nj):
            blk = kk[:, jj * LANE:(jj + 1) * LANE]
            sq = blk * blk
            p_sq = sq if p_sq is None else p_sq + sq
        pcat_sc[slot, r0:r0 + sub_rows, :] = p_sq.astype(BF16)

    def rope_keys(slot):
        krt = jnp.concatenate([krpage[slot, i] for i in range(DEC_PPS)], axis=1)
        krb_sc[slot] = krt.astype(BF16)
        kr2b_sc[slot] = (krt * krt).astype(BF16)

    def scores(slot):
        s2 = _dot_nt(qa2, cb_sc[slot])
        ns = _dot_nt(segt_ref[...], pcat_sc[slot])
        ns = ns + _dot(qr, krb_sc[slot]) + _dot(ones_ref[...], kr2b_sc[slot])
        s = s2[0:H_A, :] + s2[H_A:2 * H_A, :] + ns[H_A:2 * H_A, :]
        return s * lax.rsqrt(ns[0:H_A, :] * (1.0 / QK_HEAD) + EPS)

    def softmax_step(s, m_prev, l_prev):
        m_next = jnp.maximum(m_prev, jnp.max(s, axis=-1, keepdims=True))
        alpha = jnp.exp2(m_prev - m_next)
        p = jnp.exp2(s - m_next)
        return p, alpha, m_next, alpha * l_prev + jnp.sum(p, axis=-1, keepdims=True)

    def chunk_step(c, slot, carry):
        m_prev, l_prev, acc_prev = carry
        wait(slot)
        last = c + 1 == n_chunks
        nxt_b = jnp.minimum(jnp.where(last, b + 1, b), nb - 1)
        nxt_c = jnp.where(last, 0, c + 1)
        issue(nxt_b, nxt_c, 1 - slot)

        valid = c > 0
        prev = 1 - slot
        keys_partial(slot, 0)
        s = scores(prev)
        keys_partial(slot, 1)
        p, alpha, m_next, l_next = softmax_step(s, m_prev, l_prev)
        keys_partial(slot, 2)
        acc_next = acc_prev * alpha + _dot(p.astype(BF16), cb_sc[prev])
        for sub in range(3, n_sub):
            keys_partial(slot, sub)
        rope_keys(slot)
        return (jnp.where(valid, m_next, m_prev), jnp.where(valid, l_next, l_prev),
                jnp.where(valid, acc_next, acc_prev))

    def pair_step(i, carry):
        carry = chunk_step(2 * i, 0, carry)
        return chunk_step(2 * i + 1, 1, carry)

    init = (jnp.full((H_A, 1), -jnp.inf, F32), jnp.zeros((H_A, 1), F32), jnp.zeros((H_A, KV_LORA), F32))
    m_run, l_run, acc = lax.fori_loop(0, n_chunks // 2, pair_step, init)
    p, alpha, m_run, l_run = softmax_step(scores(1), m_run, l_run)
    acc_ref[0] = acc * alpha + _dot(p.astype(BF16), cb_sc[1])
    m_ref[0] = jnp.broadcast_to(m_run, (H_A, LANE))
    l_ref[0] = jnp.broadcast_to(l_run, (H_A, LANE))

    @pl.when(b == nb - 1)
    def _():
        wait(0)


def _absorb_kernel(q_ref, gk_ref, wukh_ref, qa_ref):
    gk = gk_ref[...]
    for h in range(H_A):
        qg = q_ref[:, h * HEAD_PAD:h * HEAD_PAD + QK_NOPE] * gk[:, :QK_NOPE]
        hi = qg.astype(BF16)
        lo = (qg - hi.astype(F32)).astype(BF16)
        qa_ref[:, h * KV_LORA:(h + 1) * KV_LORA] = _dot(hi, wukh_ref[h]) + _dot(lo, wukh_ref[h])


def _absorb_queries(q, dw):
    bd = q.shape[0]
    return pl.pallas_call(
        _absorb_kernel,
        in_specs=[_whole()] * 3,
        out_specs=_whole(),
        out_shape=jax.ShapeDtypeStruct((bd, H_A * KV_LORA), F32),
        compiler_params=pltpu.CompilerParams(vmem_limit_bytes=VMEM_LIMIT),
        name="absorb_queries",
    )(q, dw['g_k128'], dw['w_uk_heads'])


def _decode_attention(page_table, q_abs, qr_mat, cache_c, cache_krt, dw):
    bd, n_pages = page_table.shape
    page = cache_c.shape[1]
    rows = DEC_PPS * page
    assert n_pages % (2 * DEC_PPS) == 0 and DEC_PPS // DEC_SUB >= 3
    const = lambda shp: pl.BlockSpec(shp, lambda b, pt: (0,) * len(shp))
    in_specs = [pl.BlockSpec((1, H_A, KV_LORA), lambda b, pt: (b, 0, 0)),
                pl.BlockSpec((1, 2 * H_A, QK_ROPE), lambda b, pt: (b, 0, 0)),
                const((1, QK_ROPE)), const((KV_LORA, H_A * QK_NOPE)),
                const((2 * H_A, LANE)), const((2 * H_A, QK_ROPE)),
                pl.BlockSpec(memory_space=pl.ANY), pl.BlockSpec(memory_space=pl.ANY)]
    out_b = lambda w: pl.BlockSpec((1, H_A, w), lambda b, pt: (b, 0, 0))
    grid_spec = pltpu.PrefetchScalarGridSpec(
        num_scalar_prefetch=1,
        grid=(bd,),
        in_specs=in_specs,
        out_specs=(out_b(KV_LORA), out_b(LANE), out_b(LANE)),
        scratch_shapes=[pltpu.VMEM((2, DEC_PPS, page, KV_LORA), F32),
                        pltpu.VMEM((2, DEC_PPS, QK_ROPE, page), F32),
                        pltpu.VMEM((2, rows, KV_LORA), BF16),
                        pltpu.VMEM((2, rows, LANE), BF16),
                        pltpu.VMEM((2, QK_ROPE, rows), BF16),
                        pltpu.VMEM((2, QK_ROPE, rows), BF16),
                        pltpu.SemaphoreType.DMA((2,)), pltpu.SemaphoreType.DMA((2,))],
    )
    return pl.pallas_call(
        _decode_kernel,
        grid_spec=grid_spec,
        out_shape=(jax.ShapeDtypeStruct((bd, H_A, KV_LORA), F32),
                   jax.ShapeDtypeStruct((bd, H_A, LANE), F32),
                   jax.ShapeDtypeStruct((bd, H_A, LANE), F32)),
        compiler_params=_cparams(("arbitrary",)),
        name="decode_attn",
    )(page_table, q_abs, qr_mat, dw['gk_rope'], dw['w_uk_perm'], dw['seg_t'], dw['ones_rows'], cache_c, cache_krt)


def _decode_final_kernel(q_ref, k_ref, c_ref, m_ref, l_ref, acc_ref, wuv_ref, o_ref):
    c_new = c_ref[...]
    for pair in range(H_A // 2):
        o_pair = None
        for h in (2 * pair, 2 * pair + 1):
            lo, hi = h * HEAD_PAD, (h + 1) * HEAD_PAD
            s_new = jnp.sum(q_ref[:, lo:hi] * k_ref[:, lo:hi].astype(F32), axis=-1, keepdims=True)
            m_old = m_ref[:, h:h + 1]
            m_new = jnp.maximum(m_old, s_new)
            a = jnp.exp2(m_old - m_new)
            pn = jnp.exp2(s_new - m_new)
            l_new = l_ref[:, h:h + 1] * a + pn
            ctx = (acc_ref[:, h * KV_LORA:(h + 1) * KV_LORA] * a + pn * c_new) / l_new
            part = _dot(ctx.astype(BF16), wuv_ref[h])
            o_pair = part if o_pair is None else o_pair + part
        o_ref[:, pair * LANE:(pair + 1) * LANE] = o_pair.astype(o_ref.dtype)


def _decode_final(q, k_new, c_new, m, l, acc, dw):
    bd = q.shape[0]
    return pl.pallas_call(
        _decode_final_kernel,
        in_specs=[_whole()] * 7,
        out_specs=_whole(),
        out_shape=jax.ShapeDtypeStruct((bd, H_A * V_HEAD), BF16),
        compiler_params=pltpu.CompilerParams(vmem_limit_bytes=VMEM_LIMIT),
        name="decode_final",
    )(q, k_new, c_new, m, l, acc, dw['w_uv_pair'])


def _merge_kernel(x_ref, attn_ref, ssm_ref, gates_ref, wa_ref, wb_ref, wo_ref, gffn_ref, wr_ref, br_ref,
                  h_ref, hn_ref, logit_ref):
    g = jax.nn.sigmoid(gates_ref[...])
    a = _dot(attn_ref[...], wa_ref[...])
    b = _dot(ssm_ref[...].astype(BF16), wb_ref[...])
    mixed = g[:, :D_MODEL] * a + g[:, D_MODEL:] * b
    h = x_ref[...] + _dot(mixed.astype(BF16), wo_ref[...])
    h_ref[...] = h
    hn = _rms(h, gffn_ref[...])
    hn_ref[...] = hn.reshape(hn_ref.shape)
    hi, mid, _ = _split3(hn)
    w_hi = wr_ref[0]
    w_lo = wr_ref[1]
    logit_ref[...] = _dot(hi, w_hi) + _dot(mid, w_hi) + _dot(hi, w_lo) + br_ref[...]


def _merge(x, attn, ssm, gates, mw, *, tm):
    t = x.shape[0]
    row = lambda w: pl.BlockSpec((tm, w), lambda i: (i, 0))
    consts = (mw['w_a_out'], mw['w_b_out'], mw['w_out'], mw['g_ffn'], mw['w_router2'], mw['b_router128'])
    return pl.pallas_call(
        _merge_kernel,
        grid=(t // tm,),
        in_specs=[row(D_MODEL), row(H_A * V_HEAD), row(D_INNER), row(2 * D_MODEL)] + [_whole()] * len(consts),
        out_specs=(row(D_MODEL), pl.BlockSpec((tm, 1, D_MODEL), lambda i: (i, 0, 0)), row(LANE)),
        out_shape=(jax.ShapeDtypeStruct((t, D_MODEL), F32), jax.ShapeDtypeStruct((t, 1, D_MODEL), F32),
                   jax.ShapeDtypeStruct((t, LANE), F32)),
        compiler_params=_cparams(("parallel",)),
        name="merge",
    )(x, attn, ssm, gates, *consts)


def _moe_kernel(be_ref, cnt_ref, seg_ref, tok_ref, asg_ref, x_hbm, wgu_ref, bgu_ref, wd_ref, bd_ref, y_hbm,
                xbuf, obuf, wgu_b, wd_b, sem_in, sem_out):
    i = pl.program_id(0)
    n = pl.num_programs(0)
    bm = xbuf.shape[1]
    dump0 = y_hbm.shape[0] - bm

    def in_copy(tok, slot, r):
        return pltpu.make_async_copy(x_hbm.at[tok], xbuf.at[slot, pl.ds(r, 1), :], sem_in.at[slot])

    def out_copy(row, slot, r):
        return pltpu.make_async_copy(obuf.at[slot, pl.ds(r, 1), :], y_hbm.at[row], sem_out.at[slot])

    def gather_loop(blk, slot):
        base = seg_ref[blk]

        def body(r, carry):
            in_copy(tok_ref[base + r], slot, r).start()
            return carry
        lax.fori_loop(0, bm, body, 0, unroll=8)

    def gather_inline(blk, slot):
        base = seg_ref[blk]
        for r in range(bm):
            in_copy(tok_ref[base + r], slot, r).start()

    def gather_wait(slot):
        def body(r, carry):
            in_copy(0, slot, r).wait()
            return carry
        lax.fori_loop(0, bm, body, 0, unroll=8)

    def scattered(blk):
        return (blk < 0) | (cnt_ref[jnp.maximum(blk, 0)] > 0)

    def scatter_rows(blk):
        j = jnp.maximum(blk, 0)
        return seg_ref[j], jnp.where(blk >= 0, cnt_ref[j], 0)

    def scatter_loop(blk, slot):
        base, count = scatter_rows(blk)

        def body(r, carry):
            out_copy(jnp.where(r < count, asg_ref[base + r], dump0 + r), slot, r).start(priority=1)
            return carry
        lax.fori_loop(0, bm, body, 0, unroll=8)

    def scatter_inline(blk, slot):
        base, count = scatter_rows(blk)
        for r in range(bm):
            out_copy(jnp.where(count > r, asg_ref[base + r], dump0 + r), slot, r).start(priority=1)

    def scatter_wait(slot):
        def body(r, carry):
            out_copy(0, slot, r).wait()
            return carry
        lax.fori_loop(0, bm, body, 0, unroll=8)

    valid = cnt_ref[i] > 0
    nxt_valid = (i + 1 < n) & (cnt_ref[jnp.minimum(i + 1, n - 1)] > 0)

    @pl.when(i == 0)
    def _():
        obuf[1] = jnp.zeros(obuf.shape[1:], F32)

    @pl.when((i == 0) & valid)
    def _():
        gather_loop(0, 0)

    @pl.when((i >= 1) & scattered(i - 2))
    def _():
        scatter_wait(i % 2)

    @pl.when((i == 0) | (be_ref[i] != be_ref[jnp.maximum(i - 1, 0)]))
    def _():
        wgu_b[...] = wgu_ref[0].astype(BF16)
        wd_b[...] = wd_ref[0].astype(BF16)

    def expert_block(prefetch_next, slot):
        gather_wait(slot)
        x = xbuf[slot].astype(BF16)
        scatter_inline(i - 1, 1 - slot)
        if prefetch_next:
            gather_inline(i + 1, 1 - slot)
        gu = _dot(x, wgu_b[...]) + bgu_ref[0]
        gate = jnp.minimum(gu[:, :D_FF], SWIGLU_LIMIT)
        up = jnp.clip(gu[:, D_FF:], -SWIGLU_LIMIT, SWIGLU_LIMIT)
        act = (up + 1.0) * gate * jax.nn.sigmoid(SWIGLU_ALPHA * gate)
        obuf[slot] = _dot(act.astype(BF16), wd_b[...]) + bd_ref[0]

    for parity in range(2):
        on_parity = valid & (i % 2 == parity)

        @pl.when(on_parity & nxt_valid)
        def _():
            expert_block(True, parity)

        @pl.when(on_parity & jnp.logical_not(nxt_valid))
        def _():
            expert_block(False, parity)

    @pl.when(jnp.logical_not(valid) & scattered(i - 1))
    def _():
        scatter_loop(i - 1, (i + 1) % 2)

    @pl.when((i == n - 1) & scattered(i - 1))
    def _():
        scatter_wait((i + 1) % 2)


def _moe_experts(block_e, block_cnt, seg_start, tok_sorted, asg_sorted, x, w_gate_up, b_gate_up, w_down, b_down):
    n_blocks = block_e.shape[0]
    bm = MOE_BM
    n_assign = x.shape[0] * TOP_K
    w_spec = lambda shp: pl.BlockSpec(shp, lambda i, be, ct, sg, tk, ag: (be[i], 0, 0))
    grid_spec = pltpu.PrefetchScalarGridSpec(
        num_scalar_prefetch=5,
        grid=(n_blocks,),
        in_specs=[pl.BlockSpec(memory_space=pl.ANY), w_spec((1, D_MODEL, 2 * D_FF)), w_spec((1, 1, 2 * D_FF)),
                  w_spec((1, D_FF, D_MODEL)), w_spec((1, 1, D_MODEL))],
        out_specs=pl.BlockSpec(memory_space=pl.ANY),
        scratch_shapes=[pltpu.VMEM((2, bm, D_MODEL), F32), pltpu.VMEM((2, bm, D_MODEL), F32),
                        pltpu.VMEM((D_MODEL, 2 * D_FF), BF16), pltpu.VMEM((D_FF, D_MODEL), BF16),
                        pltpu.SemaphoreType.DMA((2,)), pltpu.SemaphoreType.DMA((2,))],
    )
    return pl.pallas_call(
        _moe_kernel,
        grid_spec=grid_spec,
        out_shape=jax.ShapeDtypeStruct((n_assign + bm, 1, D_MODEL), F32),
        compiler_params=_cparams(("arbitrary",)),
        name="moe_experts",
    )(block_e, block_cnt, seg_start, tok_sorted, asg_sorted, x, w_gate_up,
      b_gate_up.reshape(N_EXPERTS, 1, 2 * D_FF), w_down, b_down.reshape(N_EXPERTS, 1, D_MODEL))


def _ple_kernel(y_ref, hp_ref, hs_ref, gw_ref, pp_ref, ps_ref, gin_ref, wgate_ref, wple_ref, gple_ref,
                op_ref, os_ref):
    i = pl.program_id(0)
    n = pl.num_programs(0)
    tm = hp_ref.shape[0]
    is_sample = i == n - 1
    gw = gw_ref[...]
    y = y_ref[0:tm, 0, :] * gw[:, 0:1]
    for k in range(1, TOP_K):
        y = y + y_ref[k * tm:(k + 1) * tm, 0, :] * gw[:, k:k + 1]
    h2 = jnp.where(is_sample, hs_ref[...], hp_ref[...]) + y
    p_emb = jnp.where(is_sample, ps_ref[...], pp_ref[...])
    gate = jax.nn.sigmoid(_dot(_rms(h2, gin_ref[...]).astype(BF16), wgate_ref[...]))
    ple = _rms(_dot(p_emb.astype(BF16), wple_ref[...]), gple_ref[...]) * gate
    out = h2 + ple

    @pl.when(i < n - 1)
    def _():
        op_ref[...] = out

    @pl.when(i == n - 1)
    def _():
        os_ref[...] = out


def _combine_ple(y_rows, h_p, h_s, gate_w, p_p, p_s, pw):
    n_prompt = h_p.shape[0]
    tm = PLE_TM
    assert h_s.shape[0] == tm and n_prompt % tm == 0
    n_tiles = n_prompt // tm + 1
    prompt_row = lambda w: pl.BlockSpec((tm, w), lambda i: (jnp.minimum(i, n_tiles - 2), 0))
    sample_row = lambda w: pl.BlockSpec((tm, w), lambda i: (0, 0))
    return pl.pallas_call(
        _ple_kernel,
        grid=(n_tiles,),
        in_specs=[pl.BlockSpec((tm * TOP_K, 1, D_MODEL), lambda i: (i, 0, 0)), prompt_row(D_MODEL), sample_row(D_MODEL),
                  pl.BlockSpec((tm, LANE), lambda i: (i, 0)), prompt_row(PLE_DIM), sample_row(PLE_DIM)]
                 + [_whole()] * 4,
        out_specs=(prompt_row(D_MODEL), sample_row(D_MODEL)),
        out_shape=(jax.ShapeDtypeStruct((n_prompt, D_MODEL), F32), jax.ShapeDtypeStruct((tm, D_MODEL), F32)),
        compiler_params=_cparams(("arbitrary",)),
        name="combine_ple",
    )(y_rows, h_p, h_s, gate_w, p_p, p_s, pw['g_ple_in'], pw['w_ple_gate'], pw['w_ple'], pw['g_ple'])


def _pad_lanes(x, n):
    return jnp.pad(x, [(0, 0)] * (x.ndim - 1) + [(0, n - x.shape[-1])])


def _rot_cols(w):
    half = QK_ROPE // 2
    return jnp.concatenate([-w[..., half:], w[..., :half]], axis=-1)


def _prep_weights(g_mix_norm, w_in, g_q_a, w_q_b, g_kv_a, w_uk, w_uv, g_q_head, g_k_head, w_a_out, w_conv,
                  b_conv, dt_bias, a_log, d_skip, g_ssm_norm, w_b_out, w_out, g_ffn_norm, w_router, b_router,
                  g_ple_in, w_ple_gate, w_ple, g_ple):
    sizes = (Q_LORA, KV_LORA, QK_ROPE, D_INNER, CONV_DIM, H_B, 2 * D_MODEL)
    offs = np.concatenate([[0], np.cumsum(sizes)])
    wq, wkv, wkr, wz, wxbc, wdt, wg = [w_in[:, int(offs[i]):int(offs[i + 1])] for i in range(7)]
    zc = lambda n: jnp.zeros((D_MODEL, n), F32)
    kr128 = jnp.concatenate([zc(QK_NOPE), wkr, zc(LANE - QK_HEAD)], axis=1)
    krrot = jnp.concatenate([zc(QK_NOPE), _rot_cols(wkr), zc(LANE - QK_HEAD)], axis=1)
    w_lat = jnp.concatenate([wq, wkv, kr128, krrot, _pad_lanes(wdt, LANE)], axis=1).astype(BF16)

    wqb = w_q_b.reshape(Q_LORA, H_A, QK_HEAD)
    nope, rope_w = wqb[..., :QK_NOPE], wqb[..., QK_NOPE:]
    z_nope = jnp.zeros_like(nope)
    q128 = _pad_lanes(jnp.concatenate([nope, rope_w], axis=-1), LANE).reshape(Q_LORA, H_A * LANE)
    qrot = _pad_lanes(jnp.concatenate([z_nope, _rot_cols(rope_w)], axis=-1), LANE).reshape(Q_LORA, H_A * LANE)
    w_qb = jnp.concatenate([q128, qrot], axis=1).astype(BF16)

    wuk3 = w_uk.reshape(KV_LORA, H_A, QK_NOPE)
    w_uk_pad = _pad_lanes(wuk3, LANE).reshape(KV_LORA, H_A * LANE).astype(BF16)
    sub = LANE // H_A
    w_uk_perm = wuk3.reshape(KV_LORA, H_A, QK_NOPE // sub, sub).transpose(0, 2, 1, 3)
    w_uk_perm = w_uk_perm.reshape(KV_LORA, H_A * QK_NOPE).astype(BF16)

    wuv3 = w_uv.reshape(KV_LORA, H_A, V_HEAD)
    even = (jnp.arange(H_A) % 2 == 0)[None, :, None]
    zv = jnp.zeros_like(wuv3)
    w_uv_ext = jnp.where(even, jnp.concatenate([wuv3, zv], axis=-1), jnp.concatenate([zv, wuv3], axis=-1))
    ones_lane = np.zeros((H_A, LANE), np.float32)
    ones_lane[0::2, V_HEAD] = 1.0
    ones_lane[1::2, 0] = 1.0
    front = dict(
        g_mix=g_mix_norm.reshape(1, -1), w_lat=w_lat, w_z=wz.astype(BF16), w_xbc=wxbc.astype(BF16),
        w_g=wg.astype(BF16), g_q_a=g_q_a.reshape(1, -1), w_qb=w_qb, g_kv_a=g_kv_a.reshape(1, -1),
        w_uk_pad=w_uk_pad, w_uv_ext=w_uv_ext.reshape(KV_LORA, H_A * LANE).astype(BF16),
        v_ones=jnp.asarray(ones_lane.reshape(1, H_A * LANE)),
        g_q128=_pad_lanes(g_q_head.reshape(1, -1), LANE) * Q_SCALE,
        g_k128=_pad_lanes(g_k_head.reshape(1, -1), LANE),
    )

    lane_head = np.arange(LANE) // sub
    seg = (lane_head[None, :] == np.arange(H_A)[:, None]).astype(np.float32)
    seg_t = np.zeros((2 * H_A, LANE), np.float32)
    seg_t[:H_A] = seg
    ones_rows = np.zeros((2 * H_A, QK_ROPE), np.float32)
    ones_rows[:H_A] = 1.0
    w_uv_pair = w_uv_ext
    decode = dict(
        g_k128=front['g_k128'], gk_rope=g_k_head[QK_NOPE:].reshape(1, QK_ROPE), w_uk_perm=w_uk_perm,
        w_uk_heads=wuk3.transpose(1, 2, 0).astype(BF16),
        seg_t=jnp.asarray(seg_t, BF16), ones_rows=jnp.asarray(ones_rows, BF16),
        w_uv_pair=w_uv_pair.transpose(1, 0, 2).astype(BF16),
    )

    expand = (np.arange(D_INNER)[None, :] // SSM_HEAD == np.arange(LANE)[:, None]).astype(np.float32)
    tri = (np.arange(SSD_CHUNK)[None, :] <= np.arange(SSD_CHUNK)[:, None]).astype(np.float32)
    ssm = dict(
        w_conv=w_conv, b_conv=b_conv.reshape(1, -1), dt_bias128=_pad_lanes(dt_bias.reshape(1, -1), LANE),
        a_head128=_pad_lanes(-jnp.exp(a_log).reshape(1, -1), LANE),
        d_skip_x=jnp.repeat(d_skip, SSM_HEAD).reshape(1, -1), g_norm=g_ssm_norm.reshape(1, -1),
        tri=jnp.asarray(tri, BF16), expand=jnp.asarray(expand, BF16),
    )

    wr = _pad_lanes(w_router, LANE)
    wr_hi = wr.astype(BF16)
    wr_lo = (wr - wr_hi.astype(F32)).astype(BF16)
    merge = dict(
        w_a_out=w_a_out.astype(BF16), w_b_out=w_b_out.astype(BF16), w_out=w_out.astype(BF16),
        g_ffn=g_ffn_norm.reshape(1, -1), w_router2=jnp.stack([wr_hi, wr_lo]),
        b_router128=_pad_lanes(b_router.reshape(1, -1), LANE),
    )
    ple = dict(g_ple_in=g_ple_in.reshape(1, -1), w_ple_gate=w_ple_gate.astype(BF16), w_ple=w_ple.astype(BF16),
               g_ple=g_ple.reshape(1, -1))
    return front, decode, ssm, merge, ple


def _rope_tables(pos):
    half = QK_ROPE // 2
    inv = ROPE_THETA ** (-jnp.arange(half, dtype=F32) / half)
    ang = pos.astype(F32)[:, None] * inv[None, :]
    cos, sin = jnp.cos(ang), jnp.sin(ang)
    n = pos.shape[0]
    cos128 = jnp.concatenate([jnp.ones((n, QK_NOPE), F32), cos, cos, jnp.ones((n, LANE - QK_HEAD), F32)], axis=1)
    sin128 = jnp.concatenate([jnp.zeros((n, QK_NOPE), F32), sin, sin, jnp.zeros((n, LANE - QK_HEAD), F32)], axis=1)
    return cos128, sin128


def _route(logits, n_tok):
    top_val, top_idx = lax.top_k(logits[:, :N_EXPERTS], TOP_K)
    gate_w = jax.nn.softmax(top_val, axis=-1)
    n_assign = n_tok * TOP_K
    flat_e = top_idx.reshape(-1).astype(jnp.int32)
    counts = jnp.sum((flat_e[:, None] == jnp.arange(N_EXPERTS, dtype=jnp.int32)[None, :]).astype(jnp.int32), axis=0)
    bm = MOE_BM
    padded = (counts + bm - 1) // bm * bm
    pend = jnp.cumsum(padded)
    pstarts = pend - padded
    starts = jnp.cumsum(counts) - counts
    idx_bits = max(1, (n_assign - 1).bit_length())
    key = jnp.sort(flat_e * (1 << idx_bits) + jnp.arange(n_assign, dtype=jnp.int32))
    asg = jnp.pad(key & ((1 << idx_bits) - 1), (0, bm)).astype(jnp.int32)
    tok_sorted = asg // TOP_K
    tile_rows = PLE_TM * TOP_K
    asg_sorted = (asg // tile_rows) * tile_rows + (asg % TOP_K) * PLE_TM + tok_sorted % PLE_TM
    n_blocks = -(-n_assign // bm) + N_EXPERTS
    blk_start = jnp.arange(n_blocks, dtype=jnp.int32) * bm
    block_e = jnp.minimum(jnp.sum((pend[None, :] <= blk_start[:, None]).astype(jnp.int32), axis=1), N_EXPERTS - 1)
    offset = blk_start - pstarts[block_e]
    block_cnt = jnp.where(blk_start < pend[-1], jnp.clip(counts[block_e] - offset, 0, bm), 0).astype(jnp.int32)
    seg_start = jnp.clip(starts[block_e] + offset, 0, n_assign).astype(jnp.int32)
    return gate_w, tok_sorted, asg_sorted, block_e.astype(jnp.int32), block_cnt, seg_start


def kernel(x_prompt, x_sample, p_prompt, p_sample, cache_kv_latent, cache_k_rope, page_table, state_conv, state_ssm, g_mix_norm, w_in, g_q_a, w_q_b, g_kv_a, w_uk, w_uv, g_q_head, g_k_head, w_a_out, w_conv, b_conv, dt_bias, a_log, d_skip, g_ssm_norm, w_b_out, w_out, g_ffn_norm, w_router, b_router, w_gate_up, b_gate_up, w_down, b_down, g_ple_in, w_ple_gate, w_ple, g_ple):
    depth = g_mix_norm.shape[0]
    assert depth == 1, "one layer"
    b, s, _ = x_prompt.shape
    bd, sd, _ = x_sample.shape
    assert sd == 1, "one new token per sample sequence"
    n_pages = page_table.shape[1]
    page_size = cache_kv_latent.shape[2]
    assert n_pages % DEC_PPS == 0 and s % SSD_CHUNK == 0
    tp = b * s

    fw, dw, sw, mw, pw = _prep_weights(
        g_mix_norm[0], w_in[0], g_q_a[0], w_q_b[0], g_kv_a[0], w_uk[0], w_uv[0], g_q_head[0], g_k_head[0],
        w_a_out[0], w_conv[0], b_conv[0], dt_bias[0], a_log[0], d_skip[0], g_ssm_norm[0], w_b_out[0], w_out[0],
        g_ffn_norm[0], w_router[0], b_router[0], g_ple_in[0], w_ple_gate[0], w_ple[0], g_ple[0])

    xp = x_prompt.reshape(tp, D_MODEL)
    cos_p, sin_p = _rope_tables(jnp.arange(s))
    tm_p = min(FRONT_TM, s)
    bound = QK_HEAD * jnp.max(jnp.abs(fw['g_q128'])) * jnp.max(jnp.abs(fw['g_k128']))
    bound = (1.02 * bound + 1.0).astype(BF16).astype(F32)
    bias_lane = (jnp.arange(LANE) == QK_HEAD).astype(F32).reshape(1, LANE)
    q_p, k_p, v_p, c_p, kr_p, z_p, xbc_p, dt_p, gates_p = _front(
        xp, cos_p, sin_p, fw, -bound * bias_lane, bias_lane, tm=tm_p, q_dtype=BF16, pos_blocks=s // tm_p)
    qkv = (q_p.reshape(b, s, -1), k_p.reshape(b, s, -1), v_p.reshape(b, s, -1))
    attn_p = lax.cond(bound <= ATTN_MAX_BOUND,
                      lambda q, k, v: _prompt_attention(q, k, v, b, s, bounded=True),
                      lambda q, k, v: _prompt_attention(q, k, v, b, s, bounded=False), *qkv)
    ssm_p, hfin_p = _ssd_prompt(xbc_p, z_p, dt_p, sw, b, s)
    h_p, hn_p, logit_p = _merge(xp, attn_p.reshape(tp, -1), ssm_p, gates_p, mw, tm=min(MERGE_TM, tp))

    xs = x_sample.reshape(bd, D_MODEL)
    cos_s, sin_s = _rope_tables(jnp.full((bd,), n_pages * page_size, jnp.int32))
    no_pad = jnp.zeros((1, LANE), F32)
    q_s, k_s, _, c_s, kr_s, z_s, xbc_s, dt_s, gates_s = _front(
        xs, cos_s, sin_s, fw, no_pad, no_pad, tm=bd, q_dtype=F32, pos_blocks=1)
    q3 = q_s.reshape(bd, H_A, LANE)
    q_abs = _absorb_queries(q_s, dw).reshape(bd, H_A, KV_LORA)
    qr = q3[:, :, QK_NOPE:QK_HEAD]
    qr_mat = jnp.concatenate([jnp.zeros_like(qr), qr], axis=1)
    acc, m_run, l_run = _decode_attention(page_table, q_abs, qr_mat, cache_kv_latent[0],
                                          jnp.swapaxes(cache_k_rope[0], 1, 2), dw)
    attn_s = _decode_final(q_s, k_s, c_s, m_run[:, :, 0], l_run[:, :, 0], acc.reshape(bd, -1), dw)
    ssm_s, hnew_s = _ssm_sample(xbc_s, state_conv[0], z_s, dt_s,
                                state_ssm[0].reshape(bd, D_INNER, D_STATE), sw)
    h_s, hn_s, logit_s = _merge(xs, attn_s, ssm_s, gates_s, mw, tm=bd)

    hn_all = jnp.concatenate([hn_p, hn_s], axis=0)
    logits = jnp.concatenate([logit_p, logit_s], axis=0)
    n_tok = tp + bd
    gate_w, tok_sorted, asg_sorted, block_e, block_cnt, seg_start = _route(logits, n_tok)
    y_rows = _moe_experts(block_e, block_cnt, seg_start, tok_sorted, asg_sorted, hn_all, w_gate_up[0],
                          b_gate_up[0], w_down[0], b_down[0])
    out_p, out_s = _combine_ple(y_rows, h_p, h_s, _pad_lanes(gate_w, LANE), p_prompt[0].reshape(tp, PLE_DIM),
                                p_sample[0].reshape(bd, PLE_DIM), pw)

    y_prompt = out_p.reshape(b, s, D_MODEL)
    y_sample = out_s.reshape(bd, sd, D_MODEL)
    new_c_p = c_p.reshape(1, b, s, KV_LORA)
    new_kr_p = kr_p[:, QK_NOPE:QK_HEAD].reshape(1, b, s, QK_ROPE)
    conv_p = xbc_p.reshape(b, s, CONV_DIM)[:, s - (CONV_W - 1):].reshape(1, b, CONV_W - 1, CONV_DIM)
    ssm_state_p = hfin_p.reshape(1, b, H_B, SSM_HEAD, D_STATE)
    new_c_s = c_s.reshape(1, bd, sd, KV_LORA)
    new_kr_s = kr_s[:, QK_NOPE:QK_HEAD].reshape(1, bd, sd, QK_ROPE)
    conv_s = jnp.concatenate([state_conv[0][:, 1:], xbc_s[:, None, :]], axis=1).reshape(1, bd, CONV_W - 1, CONV_DIM)
    ssm_state_s = hnew_s.reshape(1, bd, H_B, SSM_HEAD, D_STATE)
    return (y_prompt, y_sample, new_c_p, new_kr_p, conv_p, ssm_state_p, new_c_s, new_kr_s, conv_s, ssm_state_s)
```

```python
import functools
import math

import jax
import jax.numpy as jnp
import numpy as np
from jax import lax
from jax.experimental import pallas as pl
from jax.experimental.pallas import tpu as pltpu

F32 = jnp.float32
BF16 = jnp.bfloat16

D_MODEL = 1024
H_A = 16
Q_LORA = 384
KV_LORA = 256
QK_NOPE = 64
QK_ROPE = 32
QK_HEAD = QK_NOPE + QK_ROPE
V_HEAD = 64
ROPE_THETA = 10000.0
D_INNER = 2 * D_MODEL
SSM_HEAD = 64
H_B = D_INNER // SSM_HEAD
N_GROUPS = 4
D_STATE = 128
CONV_W = 4
CONV_DIM = D_INNER + 2 * N_GROUPS * D_STATE
SSD_CHUNK = 128
N_EXPERTS = 32
TOP_K = 4
D_FF = D_MODEL
SWIGLU_LIMIT = 7.0
SWIGLU_ALPHA = 1.702
PLE_DIM = 256
EPS = 1e-6

LANE = 128
HEAD_PAD = LANE
GROUP_W = D_INNER // N_GROUPS
HEADS_PER_GROUP = H_B // N_GROUPS
Q_SCALE = QK_HEAD ** -0.5 * math.log2(math.e)
VMEM_LIMIT = 56 * 1024 * 1024

FRONT_TM = 256
ATTN_TQ = 1024
ATTN_MAX_BOUND = 40.0
MERGE_TM = 256
MOE_BM = 256
PLE_TM = 128
DEC_PPS = 16
DEC_SUB = 4


def _dot(a, b):
    return jnp.dot(a, b, preferred_element_type=F32)


def _dot_nt(a, b):
    return lax.dot_general(a, b, (((1,), (1,)), ((), ())), preferred_element_type=F32)


def _dot_tn(a, b):
    return lax.dot_general(a, b, (((0,), (0,)), ((), ())), preferred_element_type=F32)


def _split3(x):
    hi = x.astype(BF16)
    r1 = x - hi.astype(F32)
    mid = r1.astype(BF16)
    lo = (r1 - mid.astype(F32)).astype(BF16)
    return hi, mid, lo


def _dot_f32_lhs(x, e, terms=3):
    parts = _split3(x)[:terms]
    out = _dot(parts[0], e)
    for part in parts[1:]:
        out = out + _dot(part, e)
    return out


def _dot_f32_rhs(e, x):
    hi, mid, lo = _split3(x)
    return _dot(e, hi) + _dot(e, mid) + _dot(e, lo)


def _rms(x, g):
    return x * lax.rsqrt(jnp.mean(x * x, axis=-1, keepdims=True) + EPS) * g


def _silu(x):
    return x * jax.nn.sigmoid(x)


def _cparams(sem, vmem=VMEM_LIMIT):
    return pltpu.CompilerParams(dimension_semantics=sem, vmem_limit_bytes=vmem)


def _whole():
    return pl.BlockSpec(memory_space=pltpu.VMEM)


def _head_norm(xh, g):
    ss = jnp.sum(xh * xh, axis=-1, keepdims=True)
    return xh * lax.rsqrt(ss * (1.0 / QK_HEAD) + EPS) * g


def _front_kernel(x_ref, cos_ref, sin_ref, gmix_ref, wlat_ref, wz_ref, wxbc_ref, wg_ref,
                  gqa_ref, wqb_ref, gkva_ref, wuk_ref, wuv_ref, gq_ref, gk_ref, qpad_ref, kpad_ref, vpad_ref,
                  q_ref, k_ref, v_ref, c_ref, kr_ref, z_ref, xbc_ref, dt_ref, gates_ref):
    x = x_ref[...]
    ub = _rms(x, gmix_ref[...]).astype(BF16)
    z_ref[...] = _dot(ub, wz_ref[...])
    xbc_ref[...] = _dot(ub, wxbc_ref[...])
    gates_ref[...] = _dot(ub, wg_ref[...])
    lat = _dot(ub, wlat_ref[...])
    q_lat = lat[:, :Q_LORA]
    kv_lat = lat[:, Q_LORA:Q_LORA + KV_LORA]
    o = Q_LORA + KV_LORA
    kr_raw = lat[:, o:o + LANE]
    kr_rot = lat[:, o + LANE:o + 2 * LANE]
    dt_ref[...] = lat[:, o + 2 * LANE:o + 3 * LANE]
    cos = cos_ref[...]
    sin = sin_ref[...]
    c = _rms(kv_lat, gkva_ref[...])
    c_ref[...] = c
    kr = kr_raw * cos + kr_rot * sin
    kr_ref[...] = kr
    qn = _rms(q_lat, gqa_ref[...]).astype(BF16)
    q2 = _dot(qn, wqb_ref[...])
    nq = H_A * HEAD_PAD
    gq = gq_ref[...]
    qpad = qpad_ref[...]
    kpad = kpad_ref[...]
    for h in range(H_A):
        lo, hi = h * HEAD_PAD, (h + 1) * HEAD_PAD
        qh = q2[:, lo:hi] * cos + q2[:, nq + lo:nq + hi] * sin
        q_ref[:, lo:hi] = (_head_norm(qh, gq) + qpad).astype(q_ref.dtype)
    cb = c.astype(BF16)
    kn = _dot(cb, wuk_ref[...])
    gk = gk_ref[...]
    for h in range(H_A):
        lo, hi = h * HEAD_PAD, (h + 1) * HEAD_PAD
        k_ref[:, lo:hi] = (_head_norm(kn[:, lo:hi] + kr, gk) + kpad).astype(k_ref.dtype)
    v_ref[...] = (_dot(cb, wuv_ref[...]) + vpad_ref[...]).astype(v_ref.dtype)


def _front(x, cos, sin, fw, qpad, kpad, *, tm, q_dtype, pos_blocks):
    t = x.shape[0]
    nt = t // tm
    row = lambda w: pl.BlockSpec((tm, w), lambda i: (i, 0))
    pos_spec = pl.BlockSpec((tm, LANE), lambda i: (i % pos_blocks, 0))
    nq = H_A * HEAD_PAD
    out_shape = (
        jax.ShapeDtypeStruct((t, nq), q_dtype),
        jax.ShapeDtypeStruct((t, nq), BF16),
        jax.ShapeDtypeStruct((t, nq), BF16),
        jax.ShapeDtypeStruct((t, KV_LORA), F32),
        jax.ShapeDtypeStruct((t, LANE), F32),
        jax.ShapeDtypeStruct((t, D_INNER), F32),
        jax.ShapeDtypeStruct((t, CONV_DIM), F32),
        jax.ShapeDtypeStruct((t, LANE), F32),
        jax.ShapeDtypeStruct((t, 2 * D_MODEL), F32),
    )
    out_specs = (row(nq), row(nq), row(nq), row(KV_LORA), row(LANE), row(D_INNER),
                 row(CONV_DIM), row(LANE), row(2 * D_MODEL))
    weights = (fw['g_mix'], fw['w_lat'], fw['w_z'], fw['w_xbc'], fw['w_g'], fw['g_q_a'], fw['w_qb'],
               fw['g_kv_a'], fw['w_uk_pad'], fw['w_uv_ext'], fw['g_q128'], fw['g_k128'], qpad, kpad, fw['v_ones'])
    return pl.pallas_call(
        _front_kernel,
        grid=(nt,),
        in_specs=[row(D_MODEL), pos_spec, pos_spec] + [_whole()] * len(weights),
        out_specs=out_specs,
        out_shape=out_shape,
        compiler_params=_cparams(("parallel",)),
        name="front",
    )(x, cos, sin, *weights)


def _attn_kernel(qi_ref, ki_ref, q_ref, k_ref, v_ref, o_ref, acc_sc, *m_scratch, bounded):
    step = pl.program_id(2)
    qi = qi_ref[step]
    ki = ki_ref[step]
    tq = q_ref.shape[0]
    half = tq // 2

    @pl.when(ki == 0)
    def _():
        acc_sc[...] = jnp.zeros(acc_sc.shape, F32)
        if not bounded:
            m_scratch[0][...] = jnp.full(m_scratch[0].shape, -jnp.inf, F32)

    def update(q0, qn, k0, kn, masked):
        for h in range(2):
            cols = slice(h * HEAD_PAD, (h + 1) * HEAD_PAD)
            s = _dot_nt(q_ref[q0:q0 + qn, cols], k_ref[k0:k0 + kn, cols])
            v = v_ref[k0:k0 + kn, cols]
            if masked:
                keep = (lax.broadcasted_iota(jnp.int32, (qn, kn), 1) <= lax.broadcasted_iota(jnp.int32, (qn, kn), 0))
            if bounded:
                p = jnp.exp2(s)
                if masked:
                    p = jnp.where(keep, p, 0.0)
                acc_sc[h, q0:q0 + qn, :] += _dot(p.astype(BF16), v)
            else:
                m_sc = m_scratch[0]
                if masked:
                    s = jnp.where(keep, s, -jnp.inf)
                m_prev = m_sc[h, q0:q0 + qn, :]
                m_next = jnp.maximum(m_prev, jnp.max(s, axis=-1, keepdims=True))
                alpha = jnp.exp2(m_prev - m_next)
                p = jnp.exp2(s - m_next[:, :1])
                acc_sc[h, q0:q0 + qn, :] = alpha * acc_sc[h, q0:q0 + qn, :] + _dot(p.astype(BF16), v)
                m_sc[h, q0:q0 + qn, :] = m_next

    @pl.when(ki < qi)
    def _():
        update(0, tq, 0, tq, False)

    @pl.when(ki == qi)
    def _():
        update(0, half, 0, half, True)
        update(half, half, 0, half, False)
        update(half, half, half, half, True)
        lane = lax.broadcasted_iota(jnp.int32, (tq, LANE), 1)
        a0 = acc_sc[0]
        a1 = acc_sc[1]
        o0 = a0 / a0[:, V_HEAD:V_HEAD + 1]
        o1 = a1 / a1[:, 0:1]
        o_ref[...] = jnp.where(lane < V_HEAD, o0, o1).astype(o_ref.dtype)


def _prompt_attention(q, k, v, b, s, *, bounded):
    tq = min(ATTN_TQ, s)
    nq = s // tq
    pairs = [(i, j) for i in range(nq) for j in range(i + 1)]
    qi_tab = jnp.asarray([p[0] for p in pairs], jnp.int32)
    ki_tab = jnp.asarray([p[1] for p in pairs], jnp.int32)
    blk = lambda sel: pl.BlockSpec((None, tq, 2 * HEAD_PAD), lambda bi, h, t, qt, kt: (bi, sel(qt, kt)[t], h))
    scratch = [pltpu.VMEM((2, tq, LANE), F32)]
    if not bounded:
        scratch.append(pltpu.VMEM((2, tq, LANE), F32))
    grid_spec = pltpu.PrefetchScalarGridSpec(
        num_scalar_prefetch=2,
        grid=(b, H_A // 2, len(pairs)),
        in_specs=[blk(lambda qt, kt: qt), blk(lambda qt, kt: kt), blk(lambda qt, kt: kt)],
        out_specs=pl.BlockSpec((None, tq, 2 * V_HEAD), lambda bi, h, t, qt, kt: (bi, qt[t], h)),
        scratch_shapes=scratch,
    )
    return pl.pallas_call(
        functools.partial(_attn_kernel, bounded=bounded),
        grid_spec=grid_spec,
        out_shape=jax.ShapeDtypeStruct((b, s, H_A * V_HEAD), BF16),
        compiler_params=_cparams(("parallel", "parallel", "arbitrary")),
        name="prompt_attn_bounded" if bounded else "prompt_attn",
    )(qi_tab, ki_tab, q, k, v)


def _ssd_kernel(xbc_ref, z_ref, dt_ref, wconv_ref, bconv_ref, dtb_ref, ah_ref, dskip_ref, gn_ref,
                tri_ref, exp_ref, y_ref, hfin_ref, xbuf, state):
    ci = pl.program_id(1)
    nc = pl.num_programs(1)
    L = SSD_CHUNK

    @pl.when(ci == 0)
    def _():
        xbuf[0:8, :] = jnp.zeros((8, CONV_DIM), F32)
        state[...] = jnp.zeros(state.shape, F32)

    xbuf[8:8 + L, :] = xbc_ref[...]
    conv = bconv_ref[...] + xbuf[8:8 + L, :] * wconv_ref[3:4, :]
    for w in range(CONV_W - 1):
        sh = CONV_W - 1 - w
        conv = conv + xbuf[8 - sh:8 - sh + L, :] * wconv_ref[w:w + 1, :]
    xbuf[0:8, :] = xbuf[L:L + 8, :]
    xc = _silu(conv)
    xs = xc[:, :D_INNER]

    dt = jax.nn.softplus(dt_ref[...] + dtb_ref[...])
    a = dt * ah_ref[...]
    a_cum = _dot_f32_rhs(tri_ref[...], a)
    a_cum_t = a_cum.T
    a_last = a_cum[L - 1:L, :]
    ex = exp_ref[...]
    dt_x = _dot_f32_lhs(dt, ex, terms=2)
    dfs_x = _dot_f32_lhs(jnp.exp(a_cum), ex, terms=2)
    dte_x = _dot_f32_lhs(jnp.exp(a_last - a_cum), ex, terms=2)
    xdt = xs * dt_x
    xdt_b = xdt.astype(BF16)
    xw_b = (xdt * dte_x).astype(BF16)
    chunk_decay = jnp.exp(a_cum_t[:, L - 1:L])

    row = lax.broadcasted_iota(jnp.int32, (L, L), 0)
    col = lax.broadcasted_iota(jnp.int32, (L, L), 1)
    causal = col <= row
    lane = lax.broadcasted_iota(jnp.int32, (L, LANE), 1)
    for g in range(N_GROUPS):
        bm = xc[:, D_INNER + g * D_STATE:D_INNER + (g + 1) * D_STATE].astype(BF16)
        cm = xc[:, D_INNER + (N_GROUPS + g) * D_STATE:D_INNER + (N_GROUPS + g + 1) * D_STATE].astype(BF16)
        cb = _dot_nt(cm, bm)
        c0, c1 = g * GROUP_W, (g + 1) * GROUP_W
        st_prev = state[c0:c1, :]
        y_off = _dot_nt(cm, st_prev.astype(BF16)) * dfs_x[:, c0:c1]
        for j in range(HEADS_PER_GROUP // 2):
            h0 = g * HEADS_PER_GROUP + 2 * j
            x2 = xdt_b[:, h0 * SSM_HEAD:(h0 + 2) * SSM_HEAD]
            ys = []
            for hh in (h0, h0 + 1):
                seg = a_cum[:, hh:hh + 1] - a_cum_t[hh:hh + 1, :]
                sc = jnp.where(causal, cb * jnp.exp(seg), 0.0)
                ys.append(_dot(sc.astype(BF16), x2))
            y2 = jnp.where(lane < SSM_HEAD, ys[0], ys[1])
            lo = h0 * SSM_HEAD
            y_ref[:, lo:lo + LANE] = y2 + y_off[:, lo - c0:lo - c0 + LANE]
        st_new = _dot_tn(xw_b[:, c0:c1], bm)
        carry = jnp.concatenate(
            [jnp.broadcast_to(chunk_decay[hh:hh + 1, :], (SSM_HEAD, D_STATE))
             for hh in range(g * HEADS_PER_GROUP, (g + 1) * HEADS_PER_GROUP)], axis=0)
        state[c0:c1, :] = st_prev * carry + st_new

    y = y_ref[...] + dskip_ref[...] * xs
    yg = y * _silu(z_ref[...])
    gn = gn_ref[...]
    for g in range(N_GROUPS):
        c0, c1 = g * GROUP_W, (g + 1) * GROUP_W
        blk = yg[:, c0:c1]
        y_ref[:, c0:c1] = blk * lax.rsqrt(jnp.mean(blk * blk, axis=-1, keepdims=True) + EPS) * gn[:, c0:c1]

    @pl.when(ci == nc - 1)
    def _():
        hfin_ref[...] = state[...]


def _ssd_prompt(xbc, z, dt, sw, b, s):
    nc = s // SSD_CHUNK
    row = lambda w: pl.BlockSpec((SSD_CHUNK, w), lambda bi, ci: (bi * nc + ci, 0))
    consts = (sw['w_conv'], sw['b_conv'], sw['dt_bias128'], sw['a_head128'], sw['d_skip_x'], sw['g_norm'],
              sw['tri'], sw['expand'])
    return pl.pallas_call(
        _ssd_kernel,
        grid=(b, nc),
        in_specs=[row(CONV_DIM), row(D_INNER), row(LANE)] + [_whole()] * len(consts),
        out_specs=(row(D_INNER), pl.BlockSpec((None, D_INNER, D_STATE), lambda bi, ci: (bi, 0, 0))),
        out_shape=(jax.ShapeDtypeStruct((b * s, D_INNER), F32),
                   jax.ShapeDtypeStruct((b, D_INNER, D_STATE), F32)),
        scratch_shapes=[pltpu.VMEM((SSD_CHUNK + 8, CONV_DIM), F32), pltpu.VMEM((D_INNER, D_STATE), F32)],
        compiler_params=_cparams(("parallel", "arbitrary")),
        name="ssd_prompt",
    )(xbc, z, dt, *consts)


def _ssm_step_kernel(xbc_ref, cbuf_ref, z_ref, dt_ref, h_ref, wconv_ref, bconv_ref, dtb_ref, ah_ref,
                     dskip_ref, gn_ref, exp_ref, y_ref, hnew_ref):
    conv = bconv_ref[...] + xbc_ref[0] * wconv_ref[3:4, :]
    for w in range(CONV_W - 1):
        conv = conv + cbuf_ref[0, w:w + 1, :] * wconv_ref[w:w + 1, :]
    xc = _silu(conv)
    xs = xc[:, :D_INNER]
    dt = jax.nn.softplus(dt_ref[0] + dtb_ref[...])
    da = jnp.exp(dt * ah_ref[...])
    ex = exp_ref[...]
    xdt = xs * _dot_f32_lhs(jnp.broadcast_to(dt, (8, LANE)), ex)[0:1, :]

    eye = lax.broadcasted_iota(jnp.int32, (LANE, LANE), 0) == lax.broadcasted_iota(jnp.int32, (LANE, LANE), 1)

    def to_col(rowvec):
        return jnp.sum(jnp.where(eye, jnp.broadcast_to(rowvec, (LANE, LANE)), 0.0), axis=-1, keepdims=True)

    ys = []
    for g in range(N_GROUPS):
        bt = xc[:, D_INNER + g * D_STATE:D_INNER + (g + 1) * D_STATE]
        ct = xc[:, D_INNER + (N_GROUPS + g) * D_STATE:D_INNER + (N_GROUPS + g + 1) * D_STATE]
        ct8 = jnp.broadcast_to(ct, (8, D_STATE)).astype(BF16)
        for j in range(GROUP_W // LANE):
            r0 = g * GROUP_W + j * LANE
            h0 = r0 // SSM_HEAD
            h_old = h_ref[0, r0:r0 + LANE, :]
            decay = jnp.concatenate([jnp.broadcast_to(da[:, hh:hh + 1], (SSM_HEAD, D_STATE)) for hh in (h0, h0 + 1)],
                                    axis=0)
            h_new = h_old * decay + to_col(xdt[:, r0:r0 + LANE]) * bt
            hnew_ref[0, r0:r0 + LANE, :] = h_new
            ys.append(_dot_nt(ct8, h_new.astype(BF16))[0:1, :])
    y = jnp.concatenate(ys, axis=-1) + dskip_ref[...] * xs
    yg = y * _silu(z_ref[0])
    gn = gn_ref[...]
    for g in range(N_GROUPS):
        c0, c1 = g * GROUP_W, (g + 1) * GROUP_W
        blk = yg[:, c0:c1]
        y_ref[0, :, c0:c1] = blk * lax.rsqrt(jnp.mean(blk * blk, axis=-1, keepdims=True) + EPS) * gn[:, c0:c1]


def _ssm_sample(xbc, conv_buf, z, dt, h, sw):
    bd = xbc.shape[0]
    vec = lambda w: pl.BlockSpec((1, 1, w), lambda i: (i, 0, 0))
    consts = (sw['w_conv'], sw['b_conv'], sw['dt_bias128'], sw['a_head128'], sw['d_skip_x'], sw['g_norm'],
              sw['expand'])
    y, h_new = pl.pallas_call(
        _ssm_step_kernel,
        grid=(bd,),
        in_specs=[vec(CONV_DIM), pl.BlockSpec((1, CONV_W - 1, CONV_DIM), lambda i: (i, 0, 0)), vec(D_INNER),
                  vec(LANE), pl.BlockSpec((1, D_INNER, D_STATE), lambda i: (i, 0, 0))] + [_whole()] * len(consts),
        out_specs=(vec(D_INNER), pl.BlockSpec((1, D_INNER, D_STATE), lambda i: (i, 0, 0))),
        out_shape=(jax.ShapeDtypeStruct((bd, 1, D_INNER), F32),
                   jax.ShapeDtypeStruct((bd, D_INNER, D_STATE), F32)),
        compiler_params=_cparams(("parallel",)),
        name="ssm_sample",
    )(xbc.reshape(bd, 1, CONV_DIM), conv_buf, z.reshape(bd, 1, D_INNER), dt.reshape(bd, 1, LANE), h, *consts)
    return y.reshape(bd, D_INNER), h_new


def _decode_kernel(pt_ref, qa_ref, qr_ref, gkr_ref, wuk_ref, segt_ref, ones_ref, cache_c, cache_kr,
                   acc_ref, m_ref, l_ref, cpage, krpage, cb_sc, pcat_sc, krb_sc, kr2b_sc, sem_c, sem_kr):
    b = pl.program_id(0)
    nb = pl.num_programs(0)
    n_chunks = pt_ref.shape[1] // DEC_PPS
    page = cpage.shape[2]
    sub_rows = DEC_SUB * page
    n_sub = DEC_PPS // DEC_SUB
    nj = (H_A * QK_NOPE) // LANE

    def page_copies(pid, slot, i):
        return (pltpu.make_async_copy(cache_c.at[pid], cpage.at[slot, i], sem_c.at[slot]),
                pltpu.make_async_copy(cache_kr.at[pid], krpage.at[slot, i], sem_kr.at[slot]))

    def issue(bb, c, slot):
        for i in range(DEC_PPS):
            for cp in page_copies(pt_ref[bb, c * DEC_PPS + i], slot, i):
                cp.start()

    def wait(slot):
        for i in range(DEC_PPS):
            for cp in page_copies(0, slot, i):
                cp.wait()

    @pl.when(b == 0)
    def _():
        cb_sc[1] = jnp.zeros(cb_sc.shape[1:], BF16)
        pcat_sc[1] = jnp.zeros(pcat_sc.shape[1:], BF16)
        krb_sc[1] = jnp.zeros(krb_sc.shape[1:], BF16)
        kr2b_sc[1] = jnp.zeros(kr2b_sc.shape[1:], BF16)
        issue(0, 0, 0)

    qr = (qr_ref[0] * gkr_ref[...]).astype(BF16)
    q_abs = qa_ref[0]
    qa_hi = q_abs.astype(BF16)
    qa_lo = (q_abs - qa_hi.astype(F32)).astype(BF16)
    qa2 = jnp.concatenate([qa_hi, qa_lo], axis=0)

    def keys_partial(slot, sub):
        cb = jnp.concatenate([cpage[slot, sub * DEC_SUB + t].astype(BF16) for t in range(DEC_SUB)], axis=0)
        r0 = sub * sub_rows
        cb_sc[slot, r0:r0 + sub_rows, :] = cb
        kk = _dot(cb, wuk_ref[...])
        p_sq = None
        for jj in range(nj):
            blk = kk[:, jj * LANE:(jj + 1) * LANE]
            sq = blk * blk
            p_sq = sq if p_sq is None else p_sq + sq
        pcat_sc[slot, r0:r0 + sub_rows, :] = p_sq.astype(BF16)

    def rope_keys(slot):
        krt = jnp.concatenate([krpage[slot, i] for i in range(DEC_PPS)], axis=1)
        krb_sc[slot] = krt.astype(BF16)
        kr2b_sc[slot] = (krt * krt).astype(BF16)

    def scores(slot):
        s2 = _dot_nt(qa2, cb_sc[slot])
        ns = _dot_nt(segt_ref[...], pcat_sc[slot])
        ns = ns + _dot(qr, krb_sc[slot]) + _dot(ones_ref[...], kr2b_sc[slot])
        s = s2[0:H_A, :] + s2[H_A:2 * H_A, :] + ns[H_A:2 * H_A, :]
        return s * lax.rsqrt(ns[0:H_A, :] * (1.0 / QK_HEAD) + EPS)

    def softmax_step(s, m_prev, l_prev):
        m_next = jnp.maximum(m_prev, jnp.max(s, axis=-1, keepdims=True))
        alpha = jnp.exp2(m_prev - m_next)
        p = jnp.exp2(s - m_next)
        return p, alpha, m_next, alpha * l_prev + jnp.sum(p, axis=-1, keepdims=True)

    def chunk_step(c, slot, carry):
        m_prev, l_prev, acc_prev = carry
        wait(slot)
        last = c + 1 == n_chunks
        nxt_b = jnp.minimum(jnp.where(last, b + 1, b), nb - 1)
        nxt_c = jnp.where(last, 0, c + 1)
        issue(nxt_b, nxt_c, 1 - slot)

        valid = c > 0
        prev = 1 - slot
        keys_partial(slot, 0)
        s = scores(prev)
        keys_partial(slot, 1)
        p, alpha, m_next, l_next = softmax_step(s, m_prev, l_prev)
        keys_partial(slot, 2)
        acc_next = acc_prev * alpha + _dot(p.astype(BF16), cb_sc[prev])
        for sub in range(3, n_sub):
            keys_partial(slot, sub)
        rope_keys(slot)
        return (jnp.where(valid, m_next, m_prev), jnp.where(valid, l_next, l_prev),
                jnp.where(valid, acc_next, acc_prev))

    def pair_step(i, carry):
        carry = chunk_step(2 * i, 0, carry)
        return chunk_step(2 * i + 1, 1, carry)

    init = (jnp.full((H_A, 1), -jnp.inf, F32), jnp.zeros((H_A, 1), F32), jnp.zeros((H_A, KV_LORA), F32))
    m_run, l_run, acc = lax.fori_loop(0, n_chunks // 2, pair_step, init)
    p, alpha, m_run, l_run = softmax_step(scores(1), m_run, l_run)
    acc_ref[0] = acc * alpha + _dot(p.astype(BF16), cb_sc[1])
    m_ref[0] = jnp.broadcast_to(m_run, (H_A, LANE))
    l_ref[0] = jnp.broadcast_to(l_run, (H_A, LANE))

    @pl.when(b == nb - 1)
    def _():
        wait(0)


def _absorb_kernel(q_ref, gk_ref, wukh_ref, qa_ref):
    gk = gk_ref[...]
    for h in range(H_A):
        qg = q_ref[:, h * HEAD_PAD:h * HEAD_PAD + QK_NOPE] * gk[:, :QK_NOPE]
        hi = qg.astype(BF16)
        lo = (qg - hi.astype(F32)).astype(BF16)
        qa_ref[:, h * KV_LORA:(h + 1) * KV_LORA] = _dot(hi, wukh_ref[h]) + _dot(lo, wukh_ref[h])


def _absorb_queries(q, dw):
    bd = q.shape[0]
    return pl.pallas_call(
        _absorb_kernel,
        in_specs=[_whole()] * 3,
        out_specs=_whole(),
        out_shape=jax.ShapeDtypeStruct((bd, H_A * KV_LORA), F32),
        compiler_params=pltpu.CompilerParams(vmem_limit_bytes=VMEM_LIMIT),
        name="absorb_queries",
    )(q, dw['g_k128'], dw['w_uk_heads'])


def _decode_attention(page_table, q_abs, qr_mat, cache_c, cache_krt, dw):
    bd, n_pages = page_table.shape
    page = cache_c.shape[1]
    rows = DEC_PPS * page
    assert n_pages % (2 * DEC_PPS) == 0 and DEC_PPS // DEC_SUB >= 3
    const = lambda shp: pl.BlockSpec(shp, lambda b, pt: (0,) * len(shp))
    in_specs = [pl.BlockSpec((1, H_A, KV_LORA), lambda b, pt: (b, 0, 0)),
                pl.BlockSpec((1, 2 * H_A, QK_ROPE), lambda b, pt: (b, 0, 0)),
                const((1, QK_ROPE)), const((KV_LORA, H_A * QK_NOPE)),
                const((2 * H_A, LANE)), const((2 * H_A, QK_ROPE)),
                pl.BlockSpec(memory_space=pl.ANY), pl.BlockSpec(memory_space=pl.ANY)]
    out_b = lambda w: pl.BlockSpec((1, H_A, w), lambda b, pt: (b, 0, 0))
    grid_spec = pltpu.PrefetchScalarGridSpec(
        num_scalar_prefetch=1,
        grid=(bd,),
        in_specs=in_specs,
        out_specs=(out_b(KV_LORA), out_b(LANE), out_b(LANE)),
        scratch_shapes=[pltpu.VMEM((2, DEC_PPS, page, KV_LORA), F32),
                        pltpu.VMEM((2, DEC_PPS, QK_ROPE, page), F32),
                        pltpu.VMEM((2, rows, KV_LORA), BF16),
                        pltpu.VMEM((2, rows, LANE), BF16),
                        pltpu.VMEM((2, QK_ROPE, rows), BF16),
                        pltpu.VMEM((2, QK_ROPE, rows), BF16),
                        pltpu.SemaphoreType.DMA((2,)), pltpu.SemaphoreType.DMA((2,))],
    )
    return pl.pallas_call(
        _decode_kernel,
        grid_spec=grid_spec,
        out_shape=(jax.ShapeDtypeStruct((bd, H_A, KV_LORA), F32),
                   jax.ShapeDtypeStruct((bd, H_A, LANE), F32),
                   jax.ShapeDtypeStruct((bd, H_A, LANE), F32)),
        compiler_params=_cparams(("arbitrary",)),
        name="decode_attn",
    )(page_table, q_abs, qr_mat, dw['gk_rope'], dw['w_uk_perm'], dw['seg_t'], dw['ones_rows'], cache_c, cache_krt)


def _decode_final_kernel(q_ref, k_ref, c_ref, m_ref, l_ref, acc_ref, wuv_ref, o_ref):
    c_new = c_ref[...]
    for pair in range(H_A // 2):
        o_pair = None
        for h in (2 * pair, 2 * pair + 1):
            lo, hi = h * HEAD_PAD, (h + 1) * HEAD_PAD
            s_new = jnp.sum(q_ref[:, lo:hi] * k_ref[:, lo:hi].astype(F32), axis=-1, keepdims=True)
            m_old = m_ref[:, h:h + 1]
            m_new = jnp.maximum(m_old, s_new)
            a = jnp.exp2(m_old - m_new)
            pn = jnp.exp2(s_new - m_new)
            l_new = l_ref[:, h:h + 1] * a + pn
            ctx = (acc_ref[:, h * KV_LORA:(h + 1) * KV_LORA] * a + pn * c_new) / l_new
            part = _dot(ctx.astype(BF16), wuv_ref[h])
            o_pair = part if o_pair is None else o_pair + part
        o_ref[:, pair * LANE:(pair + 1) * LANE] = o_pair.astype(o_ref.dtype)


def _decode_final(q, k_new, c_new, m, l, acc, dw):
    bd = q.shape[0]
    return pl.pallas_call(
        _decode_final_kernel,
        in_specs=[_whole()] * 7,
        out_specs=_whole(),
        out_shape=jax.ShapeDtypeStruct((bd, H_A * V_HEAD), BF16),
        compiler_params=pltpu.CompilerParams(vmem_limit_bytes=VMEM_LIMIT),
        name="decode_final",
    )(q, k_new, c_new, m, l, acc, dw['w_uv_pair'])


def _merge_kernel(x_ref, attn_ref, ssm_ref, gates_ref, wa_ref, wb_ref, wo_ref, gffn_ref, wr_ref, br_ref,
                  h_ref, hn_ref, logit_ref):
    g = jax.nn.sigmoid(gates_ref[...])
    a = _dot(attn_ref[...], wa_ref[...])
    b = _dot(ssm_ref[...].astype(BF16), wb_ref[...])
    mixed = g[:, :D_MODEL] * a + g[:, D_MODEL:] * b
    h = x_ref[...] + _dot(mixed.astype(BF16), wo_ref[...])
    h_ref[...] = h
    hn = _rms(h, gffn_ref[...])
    hn_ref[...] = hn
    hi, mid, _ = _split3(hn)
    w_hi = wr_ref[0]
    w_lo = wr_ref[1]
    logit_ref[...] = _dot(hi, w_hi) + _dot(mid, w_hi) + _dot(hi, w_lo) + br_ref[...]


def _merge(x, attn, ssm, gates, mw, *, tm):
    t = x.shape[0]
    row = lambda w: pl.BlockSpec((tm, w), lambda i: (i, 0))
    consts = (mw['w_a_out'], mw['w_b_out'], mw['w_out'], mw['g_ffn'], mw['w_router2'], mw['b_router128'])
    return pl.pallas_call(
        _merge_kernel,
        grid=(t // tm,),
        in_specs=[row(D_MODEL), row(H_A * V_HEAD), row(D_INNER), row(2 * D_MODEL)] + [_whole()] * len(consts),
        out_specs=(row(D_MODEL), row(D_MODEL), row(LANE)),
        out_shape=(jax.ShapeDtypeStruct((t, D_MODEL), F32), jax.ShapeDtypeStruct((t, D_MODEL), F32),
                   jax.ShapeDtypeStruct((t, LANE), F32)),
        compiler_params=_cparams(("parallel",)),
        name="merge",
    )(x, attn, ssm, gates, *consts)


def _moe_kernel(be_ref, cnt_ref, seg_ref, tok_ref, asg_ref, x_hbm, wgu_ref, bgu_ref, wd_ref, bd_ref, y_hbm,
                xbuf, obuf, wgu_b, wd_b, sem_in, sem_out):
    i = pl.program_id(0)
    n = pl.num_programs(0)
    bm = xbuf.shape[1]
    dump0 = y_hbm.shape[0] - bm

    def in_copy(tok, slot, r):
        return pltpu.make_async_copy(x_hbm.at[pl.ds(tok, 1), :], xbuf.at[slot, pl.ds(r, 1), :], sem_in.at[slot])

    def out_copy(row, slot, r):
        return pltpu.make_async_copy(obuf.at[slot, pl.ds(r, 1), :], y_hbm.at[pl.ds(row, 1), :], sem_out.at[slot])

    def gather_loop(blk, slot):
        base = seg_ref[blk]

        def body(r, carry):
            in_copy(tok_ref[base + r], slot, r).start()
            return carry
        lax.fori_loop(0, bm, body, 0, unroll=8)

    def gather_inline(blk, slot):
        base = seg_ref[blk]
        for r in range(bm):
            in_copy(tok_ref[base + r], slot, r).start()

    def gather_wait(slot):
        pltpu.make_async_copy(xbuf.at[slot], xbuf.at[slot], sem_in.at[slot]).wait()

    def scattered(blk):
        return (blk < 0) | (cnt_ref[jnp.maximum(blk, 0)] > 0)

    def scatter_rows(blk):
        j = jnp.maximum(blk, 0)
        return seg_ref[j], jnp.where(blk >= 0, cnt_ref[j], 0)

    def scatter_loop(blk, slot):
        base, count = scatter_rows(blk)

        def body(r, carry):
            out_copy(jnp.where(r < count, asg_ref[base + r], dump0 + r), slot, r).start(priority=1)
            return carry
        lax.fori_loop(0, bm, body, 0, unroll=8)

    def scatter_inline(blk, slot):
        base, count = scatter_rows(blk)
        for r in range(bm):
            out_copy(jnp.where(count > r, asg_ref[base + r], dump0 + r), slot, r).start(priority=1)

    def scatter_wait(slot):
        pltpu.make_async_copy(obuf.at[slot], obuf.at[slot], sem_out.at[slot]).wait()

    valid = cnt_ref[i] > 0
    nxt_valid = (i + 1 < n) & (cnt_ref[jnp.minimum(i + 1, n - 1)] > 0)

    @pl.when(i == 0)
    def _():
        obuf[1] = jnp.zeros(obuf.shape[1:], F32)

    @pl.when((i == 0) & valid)
    def _():
        gather_loop(0, 0)

    @pl.when((i >= 1) & scattered(i - 2))
    def _():
        scatter_wait(i % 2)

    @pl.when((i == 0) | (be_ref[i] != be_ref[jnp.maximum(i - 1, 0)]))
    def _():
        wgu_b[...] = wgu_ref[0].astype(BF16)
        wd_b[...] = wd_ref[0].astype(BF16)

    def expert_block(prefetch_next, slot):
        gather_wait(slot)
        x = xbuf[slot].astype(BF16)
        scatter_inline(i - 1, 1 - slot)
        if prefetch_next:
            gather_inline(i + 1, 1 - slot)
        gu = _dot(x, wgu_b[...]) + bgu_ref[0]
        gate = jnp.minimum(gu[:, :D_FF], SWIGLU_LIMIT)
        up = jnp.clip(gu[:, D_FF:], -SWIGLU_LIMIT, SWIGLU_LIMIT)
        act = (up + 1.0) * gate * jax.nn.sigmoid(SWIGLU_ALPHA * gate)
        obuf[slot] = _dot(act.astype(BF16), wd_b[...]) + bd_ref[0]

    for parity in range(2):
        on_parity = valid & (i % 2 == parity)

        @pl.when(on_parity & nxt_valid)
        def _():
            expert_block(True, parity)

        @pl.when(on_parity & jnp.logical_not(nxt_valid))
        def _():
            expert_block(False, parity)

    @pl.when(jnp.logical_not(valid) & scattered(i - 1))
    def _():
        scatter_loop(i - 1, (i + 1) % 2)

    @pl.when((i == n - 1) & scattered(i - 1))
    def _():
        scatter_wait((i + 1) % 2)


def _moe_experts(block_e, block_cnt, seg_start, tok_sorted, asg_sorted, x, w_gate_up, b_gate_up, w_down, b_down):
    n_blocks = block_e.shape[0]
    bm = MOE_BM
    n_assign = x.shape[0] * TOP_K
    w_spec = lambda shp: pl.BlockSpec(shp, lambda i, be, ct, sg, tk, ag: (be[i], 0, 0))
    grid_spec = pltpu.PrefetchScalarGridSpec(
        num_scalar_prefetch=5,
        grid=(n_blocks,),
        in_specs=[pl.BlockSpec(memory_space=pl.ANY), w_spec((1, D_MODEL, 2 * D_FF)), w_spec((1, 1, 2 * D_FF)),
                  w_spec((1, D_FF, D_MODEL)), w_spec((1, 1, D_MODEL))],
        out_specs=pl.BlockSpec(memory_space=pl.ANY),
        scratch_shapes=[pltpu.VMEM((2, bm, D_MODEL), F32), pltpu.VMEM((2, bm, D_MODEL), F32),
                        pltpu.VMEM((D_MODEL, 2 * D_FF), BF16), pltpu.VMEM((D_FF, D_MODEL), BF16),
                        pltpu.SemaphoreType.DMA((2,)), pltpu.SemaphoreType.DMA((2,))],
    )
    return pl.pallas_call(
        _moe_kernel,
        grid_spec=grid_spec,
        out_shape=jax.ShapeDtypeStruct((n_assign + bm, D_MODEL), F32),
        compiler_params=_cparams(("arbitrary",)),
        name="moe_experts",
    )(block_e, block_cnt, seg_start, tok_sorted, asg_sorted, x, w_gate_up,
      b_gate_up.reshape(N_EXPERTS, 1, 2 * D_FF), w_down, b_down.reshape(N_EXPERTS, 1, D_MODEL))


def _ple_kernel(y_ref, hp_ref, hs_ref, gw_ref, pp_ref, ps_ref, gin_ref, wgate_ref, wple_ref, gple_ref,
                op_ref, os_ref):
    i = pl.program_id(0)
    n = pl.num_programs(0)
    tm = hp_ref.shape[0]
    is_sample = i == n - 1
    gw = gw_ref[...]
    y = y_ref[0:tm, :] * gw[:, 0:1]
    for k in range(1, TOP_K):
        y = y + y_ref[k * tm:(k + 1) * tm, :] * gw[:, k:k + 1]
    h2 = jnp.where(is_sample, hs_ref[...], hp_ref[...]) + y
    p_emb = jnp.where(is_sample, ps_ref[...], pp_ref[...])
    gate = jax.nn.sigmoid(_dot(_rms(h2, gin_ref[...]).astype(BF16), wgate_ref[...]))
    ple = _rms(_dot(p_emb.astype(BF16), wple_ref[...]), gple_ref[...]) * gate
    out = h2 + ple

    @pl.when(i < n - 1)
    def _():
        op_ref[...] = out

    @pl.when(i == n - 1)
    def _():
        os_ref[...] = out


def _combine_ple(y_rows, h_p, h_s, gate_w, p_p, p_s, pw):
    n_prompt = h_p.shape[0]
    tm = PLE_TM
    assert h_s.shape[0] == tm and n_prompt % tm == 0
    n_tiles = n_prompt // tm + 1
    prompt_row = lambda w: pl.BlockSpec((tm, w), lambda i: (jnp.minimum(i, n_tiles - 2), 0))
    sample_row = lambda w: pl.BlockSpec((tm, w), lambda i: (0, 0))
    return pl.pallas_call(
        _ple_kernel,
        grid=(n_tiles,),
        in_specs=[pl.BlockSpec((tm * TOP_K, D_MODEL), lambda i: (i, 0)), prompt_row(D_MODEL), sample_row(D_MODEL),
                  pl.BlockSpec((tm, LANE), lambda i: (i, 0)), prompt_row(PLE_DIM), sample_row(PLE_DIM)]
                 + [_whole()] * 4,
        out_specs=(prompt_row(D_MODEL), sample_row(D_MODEL)),
        out_shape=(jax.ShapeDtypeStruct((n_prompt, D_MODEL), F32), jax.ShapeDtypeStruct((tm, D_MODEL), F32)),
        compiler_params=_cparams(("arbitrary",)),
        name="combine_ple",
    )(y_rows, h_p, h_s, gate_w, p_p, p_s, pw['g_ple_in'], pw['w_ple_gate'], pw['w_ple'], pw['g_ple'])


def _pad_lanes(x, n):
    return jnp.pad(x, [(0, 0)] * (x.ndim - 1) + [(0, n - x.shape[-1])])


def _rot_cols(w):
    half = QK_ROPE // 2
    return jnp.concatenate([-w[..., half:], w[..., :half]], axis=-1)


def _prep_weights(g_mix_norm, w_in, g_q_a, w_q_b, g_kv_a, w_uk, w_uv, g_q_head, g_k_head, w_a_out, w_conv,
                  b_conv, dt_bias, a_log, d_skip, g_ssm_norm, w_b_out, w_out, g_ffn_norm, w_router, b_router,
                  g_ple_in, w_ple_gate, w_ple, g_ple):
    sizes = (Q_LORA, KV_LORA, QK_ROPE, D_INNER, CONV_DIM, H_B, 2 * D_MODEL)
    offs = np.concatenate([[0], np.cumsum(sizes)])
    wq, wkv, wkr, wz, wxbc, wdt, wg = [w_in[:, int(offs[i]):int(offs[i + 1])] for i in range(7)]
    zc = lambda n: jnp.zeros((D_MODEL, n), F32)
    kr128 = jnp.concatenate([zc(QK_NOPE), wkr, zc(LANE - QK_HEAD)], axis=1)
    krrot = jnp.concatenate([zc(QK_NOPE), _rot_cols(wkr), zc(LANE - QK_HEAD)], axis=1)
    w_lat = jnp.concatenate([wq, wkv, kr128, krrot, _pad_lanes(wdt, LANE)], axis=1).astype(BF16)

    wqb = w_q_b.reshape(Q_LORA, H_A, QK_HEAD)
    nope, rope_w = wqb[..., :QK_NOPE], wqb[..., QK_NOPE:]
    z_nope = jnp.zeros_like(nope)
    q128 = _pad_lanes(jnp.concatenate([nope, rope_w], axis=-1), LANE).reshape(Q_LORA, H_A * LANE)
    qrot = _pad_lanes(jnp.concatenate([z_nope, _rot_cols(rope_w)], axis=-1), LANE).reshape(Q_LORA, H_A * LANE)
    w_qb = jnp.concatenate([q128, qrot], axis=1).astype(BF16)

    wuk3 = w_uk.reshape(KV_LORA, H_A, QK_NOPE)
    w_uk_pad = _pad_lanes(wuk3, LANE).reshape(KV_LORA, H_A * LANE).astype(BF16)
    sub = LANE // H_A
    w_uk_perm = wuk3.reshape(KV_LORA, H_A, QK_NOPE // sub, sub).transpose(0, 2, 1, 3)
    w_uk_perm = w_uk_perm.reshape(KV_LORA, H_A * QK_NOPE).astype(BF16)

    wuv3 = w_uv.reshape(KV_LORA, H_A, V_HEAD)
    even = (jnp.arange(H_A) % 2 == 0)[None, :, None]
    zv = jnp.zeros_like(wuv3)
    w_uv_ext = jnp.where(even, jnp.concatenate([wuv3, zv], axis=-1), jnp.concatenate([zv, wuv3], axis=-1))
    ones_lane = np.zeros((H_A, LANE), np.float32)
    ones_lane[0::2, V_HEAD] = 1.0
    ones_lane[1::2, 0] = 1.0
    front = dict(
        g_mix=g_mix_norm.reshape(1, -1), w_lat=w_lat, w_z=wz.astype(BF16), w_xbc=wxbc.astype(BF16),
        w_g=wg.astype(BF16), g_q_a=g_q_a.reshape(1, -1), w_qb=w_qb, g_kv_a=g_kv_a.reshape(1, -1),
        w_uk_pad=w_uk_pad, w_uv_ext=w_uv_ext.reshape(KV_LORA, H_A * LANE).astype(BF16),
        v_ones=jnp.asarray(ones_lane.reshape(1, H_A * LANE)),
        g_q128=_pad_lanes(g_q_head.reshape(1, -1), LANE) * Q_SCALE,
        g_k128=_pad_lanes(g_k_head.reshape(1, -1), LANE),
    )

    lane_head = np.arange(LANE) // sub
    seg = (lane_head[None, :] == np.arange(H_A)[:, None]).astype(np.float32)
    seg_t = np.zeros((2 * H_A, LANE), np.float32)
    seg_t[:H_A] = seg
    ones_rows = np.zeros((2 * H_A, QK_ROPE), np.float32)
    ones_rows[:H_A] = 1.0
    w_uv_pair = w_uv_ext
    decode = dict(
        g_k128=front['g_k128'], gk_rope=g_k_head[QK_NOPE:].reshape(1, QK_ROPE), w_uk_perm=w_uk_perm,
        w_uk_heads=wuk3.transpose(1, 2, 0).astype(BF16),
        seg_t=jnp.asarray(seg_t, BF16), ones_rows=jnp.asarray(ones_rows, BF16),
        w_uv_pair=w_uv_pair.transpose(1, 0, 2).astype(BF16),
    )

    expand = (np.arange(D_INNER)[None, :] // SSM_HEAD == np.arange(LANE)[:, None]).astype(np.float32)
    tri = (np.arange(SSD_CHUNK)[None, :] <= np.arange(SSD_CHUNK)[:, None]).astype(np.float32)
    ssm = dict(
        w_conv=w_conv, b_conv=b_conv.reshape(1, -1), dt_bias128=_pad_lanes(dt_bias.reshape(1, -1), LANE),
        a_head128=_pad_lanes(-jnp.exp(a_log).reshape(1, -1), LANE),
        d_skip_x=jnp.repeat(d_skip, SSM_HEAD).reshape(1, -1), g_norm=g_ssm_norm.reshape(1, -1),
        tri=jnp.asarray(tri, BF16), expand=jnp.asarray(expand, BF16),
    )

    wr = _pad_lanes(w_router, LANE)
    wr_hi = wr.astype(BF16)
    wr_lo = (wr - wr_hi.astype(F32)).astype(BF16)
    merge = dict(
        w_a_out=w_a_out.astype(BF16), w_b_out=w_b_out.astype(BF16), w_out=w_out.astype(BF16),
        g_ffn=g_ffn_norm.reshape(1, -1), w_router2=jnp.stack([wr_hi, wr_lo]),
        b_router128=_pad_lanes(b_router.reshape(1, -1), LANE),
    )
    ple = dict(g_ple_in=g_ple_in.reshape(1, -1), w_ple_gate=w_ple_gate.astype(BF16), w_ple=w_ple.astype(BF16),
               g_ple=g_ple.reshape(1, -1))
    return front, decode, ssm, merge, ple


def _rope_tables(pos):
    half = QK_ROPE // 2
    inv = ROPE_THETA ** (-jnp.arange(half, dtype=F32) / half)
    ang = pos.astype(F32)[:, None] * inv[None, :]
    cos, sin = jnp.cos(ang), jnp.sin(ang)
    n = pos.shape[0]
    cos128 = jnp.concatenate([jnp.ones((n, QK_NOPE), F32), cos, cos, jnp.ones((n, LANE - QK_HEAD), F32)], axis=1)
    sin128 = jnp.concatenate([jnp.zeros((n, QK_NOPE), F32), sin, sin, jnp.zeros((n, LANE - QK_HEAD), F32)], axis=1)
    return cos128, sin128


def _route(logits, n_tok):
    top_val, top_idx = lax.top_k(logits[:, :N_EXPERTS], TOP_K)
    gate_w = jax.nn.softmax(top_val, axis=-1)
    n_assign = n_tok * TOP_K
    flat_e = top_idx.reshape(-1).astype(jnp.int32)
    counts = jnp.sum((flat_e[:, None] == jnp.arange(N_EXPERTS, dtype=jnp.int32)[None, :]).astype(jnp.int32), axis=0)
    bm = MOE_BM
    padded = (counts + bm - 1) // bm * bm
    pend = jnp.cumsum(padded)
    pstarts = pend - padded
    starts = jnp.cumsum(counts) - counts
    idx_bits = max(1, (n_assign - 1).bit_length())
    key = jnp.sort(flat_e * (1 << idx_bits) + jnp.arange(n_assign, dtype=jnp.int32))
    asg = jnp.pad(key & ((1 << idx_bits) - 1), (0, bm)).astype(jnp.int32)
    tok_sorted = asg // TOP_K
    tile_rows = PLE_TM * TOP_K
    asg_sorted = (asg // tile_rows) * tile_rows + (asg % TOP_K) * PLE_TM + tok_sorted % PLE_TM
    n_blocks = -(-n_assign // bm) + N_EXPERTS
    blk_start = jnp.arange(n_blocks, dtype=jnp.int32) * bm
    block_e = jnp.minimum(jnp.sum((pend[None, :] <= blk_start[:, None]).astype(jnp.int32), axis=1), N_EXPERTS - 1)
    offset = blk_start - pstarts[block_e]
    block_cnt = jnp.where(blk_start < pend[-1], jnp.clip(counts[block_e] - offset, 0, bm), 0).astype(jnp.int32)
    seg_start = jnp.clip(starts[block_e] + offset, 0, n_assign).astype(jnp.int32)
    return gate_w, tok_sorted, asg_sorted, block_e.astype(jnp.int32), block_cnt, seg_start


def kernel(x_prompt, x_sample, p_prompt, p_sample, cache_kv_latent, cache_k_rope, page_table, state_conv, state_ssm, g_mix_norm, w_in, g_q_a, w_q_b, g_kv_a, w_uk, w_uv, g_q_head, g_k_head, w_a_out, w_conv, b_conv, dt_bias, a_log, d_skip, g_ssm_norm, w_b_out, w_out, g_ffn_norm, w_router, b_router, w_gate_up, b_gate_up, w_down, b_down, g_ple_in, w_ple_gate, w_ple, g_ple):
    depth = g_mix_norm.shape[0]
    assert depth == 1, "one layer"
    b, s, _ = x_prompt.shape
    bd, sd, _ = x_sample.shape
    assert sd == 1, "one new token per sample sequence"
    n_pages = page_table.shape[1]
    page_size = cache_kv_latent.shape[2]
    assert n_pages % DEC_PPS == 0 and s % SSD_CHUNK == 0
    tp = b * s

    fw, dw, sw, mw, pw = _prep_weights(
        g_mix_norm[0], w_in[0], g_q_a[0], w_q_b[0], g_kv_a[0], w_uk[0], w_uv[0], g_q_head[0], g_k_head[0],
        w_a_out[0], w_conv[0], b_conv[0], dt_bias[0], a_log[0], d_skip[0], g_ssm_norm[0], w_b_out[0], w_out[0],
        g_ffn_norm[0], w_router[0], b_router[0], g_ple_in[0], w_ple_gate[0], w_ple[0], g_ple[0])

    xp = x_prompt.reshape(tp, D_MODEL)
    cos_p, sin_p = _rope_tables(jnp.arange(s))
    tm_p = min(FRONT_TM, s)
    bound = QK_HEAD * jnp.max(jnp.abs(fw['g_q128'])) * jnp.max(jnp.abs(fw['g_k128']))
    bound = (1.02 * bound + 1.0).astype(BF16).astype(F32)
    bias_lane = (jnp.arange(LANE) == QK_HEAD).astype(F32).reshape(1, LANE)
    q_p, k_p, v_p, c_p, kr_p, z_p, xbc_p, dt_p, gates_p = _front(
        xp, cos_p, sin_p, fw, -bound * bias_lane, bias_lane, tm=tm_p, q_dtype=BF16, pos_blocks=s // tm_p)
    qkv = (q_p.reshape(b, s, -1), k_p.reshape(b, s, -1), v_p.reshape(b, s, -1))
    attn_p = lax.cond(bound <= ATTN_MAX_BOUND,
                      lambda q, k, v: _prompt_attention(q, k, v, b, s, bounded=True),
                      lambda q, k, v: _prompt_attention(q, k, v, b, s, bounded=False), *qkv)
    ssm_p, hfin_p = _ssd_prompt(xbc_p, z_p, dt_p, sw, b, s)
    h_p, hn_p, logit_p = _merge(xp, attn_p.reshape(tp, -1), ssm_p, gates_p, mw, tm=min(MERGE_TM, tp))

    xs = x_sample.reshape(bd, D_MODEL)
    cos_s, sin_s = _rope_tables(jnp.full((bd,), n_pages * page_size, jnp.int32))
    no_pad = jnp.zeros((1, LANE), F32)
    q_s, k_s, _, c_s, kr_s, z_s, xbc_s, dt_s, gates_s = _front(
        xs, cos_s, sin_s, fw, no_pad, no_pad, tm=bd, q_dtype=F32, pos_blocks=1)
    q3 = q_s.reshape(bd, H_A, LANE)
    q_abs = _absorb_queries(q_s, dw).reshape(bd, H_A, KV_LORA)
    qr = q3[:, :, QK_NOPE:QK_HEAD]
    qr_mat = jnp.concatenate([jnp.zeros_like(qr), qr], axis=1)
    acc, m_run, l_run = _decode_attention(page_table, q_abs, qr_mat, cache_kv_latent[0],
                                          jnp.swapaxes(cache_k_rope[0], 1, 2), dw)
    attn_s = _decode_final(q_s, k_s, c_s, m_run[:, :, 0], l_run[:, :, 0], acc.reshape(bd, -1), dw)
    ssm_s, hnew_s = _ssm_sample(xbc_s, state_conv[0], z_s, dt_s,
                                state_ssm[0].reshape(bd, D_INNER, D_STATE), sw)
    h_s, hn_s, logit_s = _merge(xs, attn_s, ssm_s, gates_s, mw, tm=bd)

    hn_all = jnp.concatenate([hn_p, hn_s], axis=0)
    logits = jnp.concatenate([logit_p, logit_s], axis=0)
    n_tok = tp + bd
    gate_w, tok_sorted, asg_sorted, block_e, block_cnt, seg_start = _route(logits, n_tok)
    y_rows = _moe_experts(block_e, block_cnt, seg_start, tok_sorted, asg_sorted, hn_all, w_gate_up[0],
                          b_gate_up[0], w_down[0], b_down[0])
    out_p, out_s = _combine_ple(y_rows, h_p, h_s, _pad_lanes(gate_w, LANE), p_prompt[0].reshape(tp, PLE_DIM),
                                p_sample[0].reshape(bd, PLE_DIM), pw)

    y_prompt = out_p.reshape(b, s, D_MODEL)
    y_sample = out_s.reshape(bd, sd, D_MODEL)
    new_c_p = c_p.reshape(1, b, s, KV_LORA)
    new_kr_p = kr_p[:, QK_NOPE:QK_HEAD].reshape(1, b, s, QK_ROPE)
    conv_p = xbc_p.reshape(b, s, CONV_DIM)[:, s - (CONV_W - 1):].reshape(1, b, CONV_W - 1, CONV_DIM)
    ssm_state_p = hfin_p.reshape(1, b, H_B, SSM_HEAD, D_STATE)
    new_c_s = c_s.reshape(1, bd, sd, KV_LORA)
    new_kr_s = kr_s[:, QK_NOPE:QK_HEAD].reshape(1, bd, sd, QK_ROPE)
    conv_s = jnp.concatenate([state_conv[0][:, 1:], xbc_s[:, None, :]], axis=1).reshape(1, bd, CONV_W - 1, CONV_DIM)
    ssm_state_s = hnew_s.reshape(1, bd, H_B, SSM_HEAD, D_STATE)
    return (y_prompt, y_sample, new_c_p, new_kr_p, conv_p, ssm_state_p, new_c_s, new_kr_s, conv_s, ssm_state_s)
```

```python
import functools
import math

import jax
import jax.numpy as jnp
import numpy as np
from jax import lax
from jax.experimental import pallas as pl
from jax.experimental.pallas import tpu as pltpu

F32 = jnp.float32
BF16 = jnp.bfloat16

D_MODEL = 1024
H_A = 16
Q_LORA = 384
KV_LORA = 256
QK_NOPE = 64
QK_ROPE = 32
QK_HEAD = QK_NOPE + QK_ROPE
V_HEAD = 64
ROPE_THETA = 10000.0
D_INNER = 2 * D_MODEL
SSM_HEAD = 64
H_B = D_INNER // SSM_HEAD
N_GROUPS = 4
D_STATE = 128
CONV_W = 4
CONV_DIM = D_INNER + 2 * N_GROUPS * D_STATE
SSD_CHUNK = 128
N_EXPERTS = 32
TOP_K = 4
D_FF = D_MODEL
SWIGLU_LIMIT = 7.0
SWIGLU_ALPHA = 1.702
PLE_DIM = 256
EPS = 1e-6

LANE = 128
HEAD_PAD = LANE
GROUP_W = D_INNER // N_GROUPS
HEADS_PER_GROUP = H_B // N_GROUPS
Q_SCALE = QK_HEAD ** -0.5 * math.log2(math.e)
VMEM_LIMIT = 56 * 1024 * 1024

FRONT_TM = 256
ATTN_TQ = 1024
ATTN_MAX_BOUND = 40.0
MERGE_TM = 256
MOE_BM = 256
PLE_TM = 128
DEC_PPS = 16
DEC_SUB = 4


def _dot(a, b):
    return jnp.dot(a, b, preferred_element_type=F32)


def _dot_nt(a, b):
    return lax.dot_general(a, b, (((1,), (1,)), ((), ())), preferred_element_type=F32)


def _dot_tn(a, b):
    return lax.dot_general(a, b, (((0,), (0,)), ((), ())), preferred_element_type=F32)


def _split3(x):
    hi = x.astype(BF16)
    r1 = x - hi.astype(F32)
    mid = r1.astype(BF16)
    lo = (r1 - mid.astype(F32)).astype(BF16)
    return hi, mid, lo


def _dot_f32_lhs(x, e, terms=3):
    parts = _split3(x)[:terms]
    out = _dot(parts[0], e)
    for part in parts[1:]:
        out = out + _dot(part, e)
    return out


def _dot_f32_rhs(e, x):
    hi, mid, lo = _split3(x)
    return _dot(e, hi) + _dot(e, mid) + _dot(e, lo)


def _rms(x, g):
    return x * lax.rsqrt(jnp.mean(x * x, axis=-1, keepdims=True) + EPS) * g


def _silu(x):
    return x * jax.nn.sigmoid(x)


def _cparams(sem, vmem=VMEM_LIMIT):
    return pltpu.CompilerParams(dimension_semantics=sem, vmem_limit_bytes=vmem)


def _whole():
    return pl.BlockSpec(memory_space=pltpu.VMEM)


def _head_norm(xh, g):
    ss = jnp.sum(xh * xh, axis=-1, keepdims=True)
    return xh * lax.rsqrt(ss * (1.0 / QK_HEAD) + EPS) * g


def _front_kernel(x_ref, cos_ref, sin_ref, gmix_ref, wlat_ref, wz_ref, wxbc_ref, wg_ref,
                  gqa_ref, wqb_ref, gkva_ref, wuk_ref, wuv_ref, gq_ref, gk_ref, qpad_ref, kpad_ref, vpad_ref,
                  q_ref, k_ref, v_ref, c_ref, kr_ref, z_ref, xbc_ref, dt_ref, gates_ref):
    x = x_ref[...]
    ub = _rms(x, gmix_ref[...]).astype(BF16)
    z_ref[...] = _dot(ub, wz_ref[...])
    xbc_ref[...] = _dot(ub, wxbc_ref[...])
    gates_ref[...] = _dot(ub, wg_ref[...])
    lat = _dot(ub, wlat_ref[...])
    q_lat = lat[:, :Q_LORA]
    kv_lat = lat[:, Q_LORA:Q_LORA + KV_LORA]
    o = Q_LORA + KV_LORA
    kr_raw = lat[:, o:o + LANE]
    kr_rot = lat[:, o + LANE:o + 2 * LANE]
    dt_ref[...] = lat[:, o + 2 * LANE:o + 3 * LANE]
    cos = cos_ref[...]
    sin = sin_ref[...]
    c = _rms(kv_lat, gkva_ref[...])
    c_ref[...] = c
    kr = kr_raw * cos + kr_rot * sin
    kr_ref[...] = kr
    qn = _rms(q_lat, gqa_ref[...]).astype(BF16)
    q2 = _dot(qn, wqb_ref[...])
    nq = H_A * HEAD_PAD
    gq = gq_ref[...]
    qpad = qpad_ref[...]
    kpad = kpad_ref[...]
    for h in range(H_A):
        lo, hi = h * HEAD_PAD, (h + 1) * HEAD_PAD
        qh = q2[:, lo:hi] * cos + q2[:, nq + lo:nq + hi] * sin
        q_ref[:, lo:hi] = (_head_norm(qh, gq) + qpad).astype(q_ref.dtype)
    cb = c.astype(BF16)
    kn = _dot(cb, wuk_ref[...])
    gk = gk_ref[...]
    for h in range(H_A):
        lo, hi = h * HEAD_PAD, (h + 1) * HEAD_PAD
        k_ref[:, lo:hi] = (_head_norm(kn[:, lo:hi] + kr, gk) + kpad).astype(k_ref.dtype)
    v_ref[...] = (_dot(cb, wuv_ref[...]) + vpad_ref[...]).astype(v_ref.dtype)


def _front(x, cos, sin, fw, qpad, kpad, *, tm, q_dtype, pos_blocks):
    t = x.shape[0]
    nt = t // tm
    row = lambda w: pl.BlockSpec((tm, w), lambda i: (i, 0))
    pos_spec = pl.BlockSpec((tm, LANE), lambda i: (i % pos_blocks, 0))
    nq = H_A * HEAD_PAD
    out_shape = (
        jax.ShapeDtypeStruct((t, nq), q_dtype),
        jax.ShapeDtypeStruct((t, nq), BF16),
        jax.ShapeDtypeStruct((t, nq), BF16),
        jax.ShapeDtypeStruct((t, KV_LORA), F32),
        jax.ShapeDtypeStruct((t, LANE), F32),
        jax.ShapeDtypeStruct((t, D_INNER), F32),
        jax.ShapeDtypeStruct((t, CONV_DIM), F32),
        jax.ShapeDtypeStruct((t, LANE), F32),
        jax.ShapeDtypeStruct((t, 2 * D_MODEL), F32),
    )
    out_specs = (row(nq), row(nq), row(nq), row(KV_LORA), row(LANE), row(D_INNER),
                 row(CONV_DIM), row(LANE), row(2 * D_MODEL))
    weights = (fw['g_mix'], fw['w_lat'], fw['w_z'], fw['w_xbc'], fw['w_g'], fw['g_q_a'], fw['w_qb'],
               fw['g_kv_a'], fw['w_uk_pad'], fw['w_uv_ext'], fw['g_q128'], fw['g_k128'], qpad, kpad, fw['v_ones'])
    return pl.pallas_call(
        _front_kernel,
        grid=(nt,),
        in_specs=[row(D_MODEL), pos_spec, pos_spec] + [_whole()] * len(weights),
        out_specs=out_specs,
        out_shape=out_shape,
        compiler_params=_cparams(("parallel",)),
        name="front",
    )(x, cos, sin, *weights)


def _attn_kernel(qi_ref, ki_ref, q_ref, k_ref, v_ref, o_ref, acc_sc, *m_scratch, bounded):
    step = pl.program_id(2)
    qi = qi_ref[step]
    ki = ki_ref[step]
    tq = q_ref.shape[0]
    half = tq // 2

    @pl.when(ki == 0)
    def _():
        acc_sc[...] = jnp.zeros(acc_sc.shape, F32)
        if not bounded:
            m_scratch[0][...] = jnp.full(m_scratch[0].shape, -jnp.inf, F32)

    def update(q0, qn, k0, kn, masked):
        for h in range(2):
            cols = slice(h * HEAD_PAD, (h + 1) * HEAD_PAD)
            s = _dot_nt(q_ref[q0:q0 + qn, cols], k_ref[k0:k0 + kn, cols])
            v = v_ref[k0:k0 + kn, cols]
            if masked:
                keep = (lax.broadcasted_iota(jnp.int32, (qn, kn), 1) <= lax.broadcasted_iota(jnp.int32, (qn, kn), 0))
            if bounded:
                p = jnp.exp2(s)
                if masked:
                    p = jnp.where(keep, p, 0.0)
                acc_sc[h, q0:q0 + qn, :] += _dot(p.astype(BF16), v)
            else:
                m_sc = m_scratch[0]
                if masked:
                    s = jnp.where(keep, s, -jnp.inf)
                m_prev = m_sc[h, q0:q0 + qn, :]
                m_next = jnp.maximum(m_prev, jnp.max(s, axis=-1, keepdims=True))
                alpha = jnp.exp2(m_prev - m_next)
                p = jnp.exp2(s - m_next[:, :1])
                acc_sc[h, q0:q0 + qn, :] = alpha * acc_sc[h, q0:q0 + qn, :] + _dot(p.astype(BF16), v)
                m_sc[h, q0:q0 + qn, :] = m_next

    @pl.when(ki < qi)
    def _():
        update(0, tq, 0, tq, False)

    @pl.when(ki == qi)
    def _():
        update(0, half, 0, half, True)
        update(half, half, 0, half, False)
        update(half, half, half, half, True)
        lane = lax.broadcasted_iota(jnp.int32, (tq, LANE), 1)
        a0 = acc_sc[0]
        a1 = acc_sc[1]
        o0 = a0 / a0[:, V_HEAD:V_HEAD + 1]
        o1 = a1 / a1[:, 0:1]
        o_ref[...] = jnp.where(lane < V_HEAD, o0, o1).astype(o_ref.dtype)


def _prompt_attention(q, k, v, b, s, *, bounded):
    tq = min(ATTN_TQ, s)
    nq = s // tq
    pairs = [(i, j) for i in range(nq) for j in range(i + 1)]
    qi_tab = jnp.asarray([p[0] for p in pairs], jnp.int32)
    ki_tab = jnp.asarray([p[1] for p in pairs], jnp.int32)
    blk = lambda sel: pl.BlockSpec((None, tq, 2 * HEAD_PAD), lambda bi, h, t, qt, kt: (bi, sel(qt, kt)[t], h))
    scratch = [pltpu.VMEM((2, tq, LANE), F32)]
    if not bounded:
        scratch.append(pltpu.VMEM((2, tq, LANE), F32))
    grid_spec = pltpu.PrefetchScalarGridSpec(
        num_scalar_prefetch=2,
        grid=(b, H_A // 2, len(pairs)),
        in_specs=[blk(lambda qt, kt: qt), blk(lambda qt, kt: kt), blk(lambda qt, kt: kt)],
        out_specs=pl.BlockSpec((None, tq, 2 * V_HEAD), lambda bi, h, t, qt, kt: (bi, qt[t], h)),
        scratch_shapes=scratch,
    )
    return pl.pallas_call(
        functools.partial(_attn_kernel, bounded=bounded),
        grid_spec=grid_spec,
        out_shape=jax.ShapeDtypeStruct((b, s, H_A * V_HEAD), BF16),
        compiler_params=_cparams(("parallel", "parallel", "arbitrary")),
        name="prompt_attn_bounded" if bounded else "prompt_attn",
    )(qi_tab, ki_tab, q, k, v)


def _ssd_kernel(xbc_ref, z_ref, dt_ref, wconv_ref, bconv_ref, dtb_ref, ah_ref, dskip_ref, gn_ref,
                tri_ref, exp_ref, y_ref, hfin_ref, xbuf, state):
    ci = pl.program_id(1)
    nc = pl.num_programs(1)
    L = SSD_CHUNK

    @pl.when(ci == 0)
    def _():
        xbuf[0:8, :] = jnp.zeros((8, CONV_DIM), F32)
        state[...] = jnp.zeros(state.shape, F32)

    xbuf[8:8 + L, :] = xbc_ref[...]
    conv = bconv_ref[...] + xbuf[8:8 + L, :] * wconv_ref[3:4, :]
    for w in range(CONV_W - 1):
        sh = CONV_W - 1 - w
        conv = conv + xbuf[8 - sh:8 - sh + L, :] * wconv_ref[w:w + 1, :]
    xbuf[0:8, :] = xbuf[L:L + 8, :]
    xc = _silu(conv)
    xs = xc[:, :D_INNER]

    dt = jax.nn.softplus(dt_ref[...] + dtb_ref[...])
    a = dt * ah_ref[...]
    a_cum = _dot_f32_rhs(tri_ref[...], a)
    a_cum_t = a_cum.T
    a_last = a_cum[L - 1:L, :]
    ex = exp_ref[...]
    dt_x = _dot_f32_lhs(dt, ex, terms=2)
    dfs_x = _dot_f32_lhs(jnp.exp(a_cum), ex, terms=2)
    dte_x = _dot_f32_lhs(jnp.exp(a_last - a_cum), ex, terms=2)
    xdt = xs * dt_x
    xdt_b = xdt.astype(BF16)
    xw_b = (xdt * dte_x).astype(BF16)
    chunk_decay = jnp.exp(a_cum_t[:, L - 1:L])

    row = lax.broadcasted_iota(jnp.int32, (L, L), 0)
    col = lax.broadcasted_iota(jnp.int32, (L, L), 1)
    causal = col <= row
    lane = lax.broadcasted_iota(jnp.int32, (L, LANE), 1)
    for g in range(N_GROUPS):
        bm = xc[:, D_INNER + g * D_STATE:D_INNER + (g + 1) * D_STATE].astype(BF16)
        cm = xc[:, D_INNER + (N_GROUPS + g) * D_STATE:D_INNER + (N_GROUPS + g + 1) * D_STATE].astype(BF16)
        cb = _dot_nt(cm, bm)
        c0, c1 = g * GROUP_W, (g + 1) * GROUP_W
        st_prev = state[c0:c1, :]
        y_off = _dot_nt(cm, st_prev.astype(BF16)) * dfs_x[:, c0:c1]
        for j in range(HEADS_PER_GROUP // 2):
            h0 = g * HEADS_PER_GROUP + 2 * j
            x2 = xdt_b[:, h0 * SSM_HEAD:(h0 + 2) * SSM_HEAD]
            ys = []
            for hh in (h0, h0 + 1):
                seg = a_cum[:, hh:hh + 1] - a_cum_t[hh:hh + 1, :]
                sc = jnp.where(causal, cb * jnp.exp(seg), 0.0)
                ys.append(_dot(sc.astype(BF16), x2))
            y2 = jnp.where(lane < SSM_HEAD, ys[0], ys[1])
            lo = h0 * SSM_HEAD
            y_ref[:, lo:lo + LANE] = y2 + y_off[:, lo - c0:lo - c0 + LANE]
        st_new = _dot_tn(xw_b[:, c0:c1], bm)
        carry = jnp.concatenate(
            [jnp.broadcast_to(chunk_decay[hh:hh + 1, :], (SSM_HEAD, D_STATE))
             for hh in range(g * HEADS_PER_GROUP, (g + 1) * HEADS_PER_GROUP)], axis=0)
        state[c0:c1, :] = st_prev * carry + st_new

    y = y_ref[...] + dskip_ref[...] * xs
    yg = y * _silu(z_ref[...])
    gn = gn_ref[...]
    for g in range(N_GROUPS):
        c0, c1 = g * GROUP_W, (g + 1) * GROUP_W
        blk = yg[:, c0:c1]
        y_ref[:, c0:c1] = blk * lax.rsqrt(jnp.mean(blk * blk, axis=-1, keepdims=True) + EPS) * gn[:, c0:c1]

    @pl.when(ci == nc - 1)
    def _():
        hfin_ref[...] = state[...]


def _ssd_prompt(xbc, z, dt, sw, b, s):
    nc = s // SSD_CHUNK
    row = lambda w: pl.BlockSpec((SSD_CHUNK, w), lambda bi, ci: (bi * nc + ci, 0))
    consts = (sw['w_conv'], sw['b_conv'], sw['dt_bias128'], sw['a_head128'], sw['d_skip_x'], sw['g_norm'],
              sw['tri'], sw['expand'])
    return pl.pallas_call(
        _ssd_kernel,
        grid=(b, nc),
        in_specs=[row(CONV_DIM), row(D_INNER), row(LANE)] + [_whole()] * len(consts),
        out_specs=(row(D_INNER), pl.BlockSpec((None, D_INNER, D_STATE), lambda bi, ci: (bi, 0, 0))),
        out_shape=(jax.ShapeDtypeStruct((b * s, D_INNER), F32),
                   jax.ShapeDtypeStruct((b, D_INNER, D_STATE), F32)),
        scratch_shapes=[pltpu.VMEM((SSD_CHUNK + 8, CONV_DIM), F32), pltpu.VMEM((D_INNER, D_STATE), F32)],
        compiler_params=_cparams(("parallel", "arbitrary")),
        name="ssd_prompt",
    )(xbc, z, dt, *consts)


def _ssm_step_kernel(xbc_ref, cbuf_ref, z_ref, dt_ref, h_ref, wconv_ref, bconv_ref, dtb_ref, ah_ref,
                     dskip_ref, gn_ref, exp_ref, y_ref, hnew_ref):
    conv = bconv_ref[...] + xbc_ref[0] * wconv_ref[3:4, :]
    for w in range(CONV_W - 1):
        conv = conv + cbuf_ref[0, w:w + 1, :] * wconv_ref[w:w + 1, :]
    xc = _silu(conv)
    xs = xc[:, :D_INNER]
    dt = jax.nn.softplus(dt_ref[0] + dtb_ref[...])
    da = jnp.exp(dt * ah_ref[...])
    ex = exp_ref[...]
    xdt = xs * _dot_f32_lhs(jnp.broadcast_to(dt, (8, LANE)), ex)[0:1, :]

    eye = lax.broadcasted_iota(jnp.int32, (LANE, LANE), 0) == lax.broadcasted_iota(jnp.int32, (LANE, LANE), 1)

    def to_col(rowvec):
        return jnp.sum(jnp.where(eye, jnp.broadcast_to(rowvec, (LANE, LANE)), 0.0), axis=-1, keepdims=True)

    ys = []
    for g in range(N_GROUPS):
        bt = xc[:, D_INNER + g * D_STATE:D_INNER + (g + 1) * D_STATE]
        ct = xc[:, D_INNER + (N_GROUPS + g) * D_STATE:D_INNER + (N_GROUPS + g + 1) * D_STATE]
        ct8 = jnp.broadcast_to(ct, (8, D_STATE)).astype(BF16)
        for j in range(GROUP_W // LANE):
            r0 = g * GROUP_W + j * LANE
            h0 = r0 // SSM_HEAD
            h_old = h_ref[0, r0:r0 + LANE, :]
            decay = jnp.concatenate([jnp.broadcast_to(da[:, hh:hh + 1], (SSM_HEAD, D_STATE)) for hh in (h0, h0 + 1)],
                                    axis=0)
            h_new = h_old * decay + to_col(xdt[:, r0:r0 + LANE]) * bt
            hnew_ref[0, r0:r0 + LANE, :] = h_new
            ys.append(_dot_nt(ct8, h_new.astype(BF16))[0:1, :])
    y = jnp.concatenate(ys, axis=-1) + dskip_ref[...] * xs
    yg = y * _silu(z_ref[0])
    gn = gn_ref[...]
    for g in range(N_GROUPS):
        c0, c1 = g * GROUP_W, (g + 1) * GROUP_W
        blk = yg[:, c0:c1]
        y_ref[0, :, c0:c1] = blk * lax.rsqrt(jnp.mean(blk * blk, axis=-1, keepdims=True) + EPS) * gn[:, c0:c1]


def _ssm_sample(xbc, conv_buf, z, dt, h, sw):
    bd = xbc.shape[0]
    vec = lambda w: pl.BlockSpec((1, 1, w), lambda i: (i, 0, 0))
    consts = (sw['w_conv'], sw['b_conv'], sw['dt_bias128'], sw['a_head128'], sw['d_skip_x'], sw['g_norm'],
              sw['expand'])
    y, h_new = pl.pallas_call(
        _ssm_step_kernel,
        grid=(bd,),
        in_specs=[vec(CONV_DIM), pl.BlockSpec((1, CONV_W - 1, CONV_DIM), lambda i: (i, 0, 0)), vec(D_INNER),
                  vec(LANE), pl.BlockSpec((1, D_INNER, D_STATE), lambda i: (i, 0, 0))] + [_whole()] * len(consts),
        out_specs=(vec(D_INNER), pl.BlockSpec((1, D_INNER, D_STATE), lambda i: (i, 0, 0))),
        out_shape=(jax.ShapeDtypeStruct((bd, 1, D_INNER), F32),
                   jax.ShapeDtypeStruct((bd, D_INNER, D_STATE), F32)),
        compiler_params=_cparams(("parallel",)),
        name="ssm_sample",
    )(xbc.reshape(bd, 1, CONV_DIM), conv_buf, z.reshape(bd, 1, D_INNER), dt.reshape(bd, 1, LANE), h, *consts)
    return y.reshape(bd, D_INNER), h_new


def _decode_kernel(pt_ref, qa_ref, qr_ref, gkr_ref, wuk_ref, segt_ref, ones_ref, cache_c, cache_kr,
                   acc_ref, m_ref, l_ref, cpage, krpage, cb_sc, pcat_sc, krb_sc, kr2b_sc, sem_c, sem_kr):
    b = pl.program_id(0)
    nb = pl.num_programs(0)
    n_chunks = pt_ref.shape[1] // DEC_PPS
    page = cpage.shape[2]
    sub_rows = DEC_SUB * page
    n_sub = DEC_PPS // DEC_SUB
    nj = (H_A * QK_NOPE) // LANE

    def page_copies(pid, slot, i):
        return (pltpu.make_async_copy(cache_c.at[pid], cpage.at[slot, i], sem_c.at[slot]),
                pltpu.make_async_copy(cache_kr.at[pid], krpage.at[slot, i], sem_kr.at[slot]))

    def issue(bb, c, slot):
        for i in range(DEC_PPS):
            for cp in page_copies(pt_ref[bb, c * DEC_PPS + i], slot, i):
                cp.start()

    def wait(slot):
        for i in range(DEC_PPS):
            for cp in page_copies(0, slot, i):
                cp.wait()

    @pl.when(b == 0)
    def _():
        cb_sc[1] = jnp.zeros(cb_sc.shape[1:], BF16)
        pcat_sc[1] = jnp.zeros(pcat_sc.shape[1:], BF16)
        krb_sc[1] = jnp.zeros(krb_sc.shape[1:], BF16)
        kr2b_sc[1] = jnp.zeros(kr2b_sc.shape[1:], BF16)
        issue(0, 0, 0)

    qr = (qr_ref[0] * gkr_ref[...]).astype(BF16)
    q_abs = qa_ref[0]
    qa_hi = q_abs.astype(BF16)
    qa_lo = (q_abs - qa_hi.astype(F32)).astype(BF16)
    qa2 = jnp.concatenate([qa_hi, qa_lo], axis=0)

    def keys_partial(slot, sub):
        cb = jnp.concatenate([cpage[slot, sub * DEC_SUB + t].astype(BF16) for t in range(DEC_SUB)], axis=0)
        r0 = sub * sub_rows
        cb_sc[slot, r0:r0 + sub_rows, :] = cb
        kk = _dot(cb, wuk_ref[...])
        p_sq = None
        for jj in range(nj):
            blk = kk[:, jj * LANE:(jj + 1) * LANE]
            sq = blk * blk
            p_sq = sq if p_sq is None else p_sq + sq
        pcat_sc[slot, r0:r0 + sub_rows, :] = p_sq.astype(BF16)

    def rope_keys(slot):
        krt = jnp.concatenate([krpage[slot, i] for i in range(DEC_PPS)], axis=1)
        krb_sc[slot] = krt.astype(BF16)
        kr2b_sc[slot] = (krt * krt).astype(BF16)

    def scores(slot):
        s2 = _dot_nt(qa2, cb_sc[slot])
        ns = _dot_nt(segt_ref[...], pcat_sc[slot])
        ns = ns + _dot(qr, krb_sc[slot]) + _dot(ones_ref[...], kr2b_sc[slot])
        s = s2[0:H_A, :] + s2[H_A:2 * H_A, :] + ns[H_A:2 * H_A, :]
        return s * lax.rsqrt(ns[0:H_A, :] * (1.0 / QK_HEAD) + EPS)

    def softmax_step(s, m_prev, l_prev):
        m_next = jnp.maximum(m_prev, jnp.max(s, axis=-1, keepdims=True))
        alpha = jnp.exp2(m_prev - m_next)
        p = jnp.exp2(s - m_next)
        return p, alpha, m_next, alpha * l_prev + jnp.sum(p, axis=-1, keepdims=True)

    def chunk_step(c, slot, carry):
        m_prev, l_prev, acc_prev = carry
        wait(slot)
        last = c + 1 == n_chunks
        nxt_b = jnp.minimum(jnp.where(last, b + 1, b), nb - 1)
        nxt_c = jnp.where(last, 0, c + 1)
        issue(nxt_b, nxt_c, 1 - slot)

        valid = c > 0
        prev = 1 - slot
        keys_partial(slot, 0)
        s = scores(prev)
        keys_partial(slot, 1)
        p, alpha, m_next, l_next = softmax_step(s, m_prev, l_prev)
        keys_partial(slot, 2)
        acc_next = acc_prev * alpha + _dot(p.astype(BF16), cb_sc[prev])
        for sub in range(3, n_sub):
            keys_partial(slot, sub)
        rope_keys(slot)
        return (jnp.where(valid, m_next, m_prev), jnp.where(valid, l_next, l_prev),
                jnp.where(valid, acc_next, acc_prev))

    def pair_step(i, carry):
        carry = chunk_step(2 * i, 0, carry)
        return chunk_step(2 * i + 1, 1, carry)

    init = (jnp.full((H_A, 1), -jnp.inf, F32), jnp.zeros((H_A, 1), F32), jnp.zeros((H_A, KV_LORA), F32))
    m_run, l_run, acc = lax.fori_loop(0, n_chunks // 2, pair_step, init)
    p, alpha, m_run, l_run = softmax_step(scores(1), m_run, l_run)
    acc_ref[0] = acc * alpha + _dot(p.astype(BF16), cb_sc[1])
    m_ref[0] = jnp.broadcast_to(m_run, (H_A, LANE))
    l_ref[0] = jnp.broadcast_to(l_run, (H_A, LANE))

    @pl.when(b == nb - 1)
    def _():
        wait(0)


def _absorb_kernel(q_ref, gk_ref, wukh_ref, qa_ref):
    gk = gk_ref[...]
    for h in range(H_A):
        qg = q_ref[:, h * HEAD_PAD:h * HEAD_PAD + QK_NOPE] * gk[:, :QK_NOPE]
        hi = qg.astype(BF16)
        lo = (qg - hi.astype(F32)).astype(BF16)
        qa_ref[:, h * KV_LORA:(h + 1) * KV_LORA] = _dot(hi, wukh_ref[h]) + _dot(lo, wukh_ref[h])


def _absorb_queries(q, dw):
    bd = q.shape[0]
    return pl.pallas_call(
        _absorb_kernel,
        in_specs=[_whole()] * 3,
        out_specs=_whole(),
        out_shape=jax.ShapeDtypeStruct((bd, H_A * KV_LORA), F32),
        compiler_params=pltpu.CompilerParams(vmem_limit_bytes=VMEM_LIMIT),
        name="absorb_queries",
    )(q, dw['g_k128'], dw['w_uk_heads'])


def _decode_attention(page_table, q_abs, qr_mat, cache_c, cache_krt, dw):
    bd, n_pages = page_table.shape
    page = cache_c.shape[1]
    rows = DEC_PPS * page
    assert n_pages % (2 * DEC_PPS) == 0 and DEC_PPS // DEC_SUB >= 3
    const = lambda shp: pl.BlockSpec(shp, lambda b, pt: (0,) * len(shp))
    in_specs = [pl.BlockSpec((1, H_A, KV_LORA), lambda b, pt: (b, 0, 0)),
                pl.BlockSpec((1, 2 * H_A, QK_ROPE), lambda b, pt: (b, 0, 0)),
                const((1, QK_ROPE)), const((KV_LORA, H_A * QK_NOPE)),
                const((2 * H_A, LANE)), const((2 * H_A, QK_ROPE)),
                pl.BlockSpec(memory_space=pl.ANY), pl.BlockSpec(memory_space=pl.ANY)]
    out_b = lambda w: pl.BlockSpec((1, H_A, w), lambda b, pt: (b, 0, 0))
    grid_spec = pltpu.PrefetchScalarGridSpec(
        num_scalar_prefetch=1,
        grid=(bd,),
        in_specs=in_specs,
        out_specs=(out_b(KV_LORA), out_b(LANE), out_b(LANE)),
        scratch_shapes=[pltpu.VMEM((2, DEC_PPS, page, KV_LORA), F32),
                        pltpu.VMEM((2, DEC_PPS, QK_ROPE, page), F32),
                        pltpu.VMEM((2, rows, KV_LORA), BF16),
                        pltpu.VMEM((2, rows, LANE), BF16),
                        pltpu.VMEM((2, QK_ROPE, rows), BF16),
                        pltpu.VMEM((2, QK_ROPE, rows), BF16),
                        pltpu.SemaphoreType.DMA((2,)), pltpu.SemaphoreType.DMA((2,))],
    )
    return pl.pallas_call(
        _decode_kernel,
        grid_spec=grid_spec,
        out_shape=(jax.ShapeDtypeStruct((bd, H_A, KV_LORA), F32),
                   jax.ShapeDtypeStruct((bd, H_A, LANE), F32),
                   jax.ShapeDtypeStruct((bd, H_A, LANE), F32)),
        compiler_params=_cparams(("arbitrary",)),
        name="decode_attn",
    )(page_table, q_abs, qr_mat, dw['gk_rope'], dw['w_uk_perm'], dw['seg_t'], dw['ones_rows'], cache_c, cache_krt)


def _decode_final_kernel(q_ref, k_ref, c_ref, m_ref, l_ref, acc_ref, wuv_ref, o_ref):
    c_new = c_ref[...]
    for pair in range(H_A // 2):
        o_pair = None
        for h in (2 * pair, 2 * pair + 1):
            lo, hi = h * HEAD_PAD, (h + 1) * HEAD_PAD
            s_new = jnp.sum(q_ref[:, lo:hi] * k_ref[:, lo:hi].astype(F32), axis=-1, keepdims=True)
            m_old = m_ref[:, h:h + 1]
            m_new = jnp.maximum(m_old, s_new)
            a = jnp.exp2(m_old - m_new)
            pn = jnp.exp2(s_new - m_new)
            l_new = l_ref[:, h:h + 1] * a + pn
            ctx = (acc_ref[:, h * KV_LORA:(h + 1) * KV_LORA] * a + pn * c_new) / l_new
            part = _dot(ctx.astype(BF16), wuv_ref[h])
            o_pair = part if o_pair is None else o_pair + part
        o_ref[:, pair * LANE:(pair + 1) * LANE] = o_pair.astype(o_ref.dtype)


def _decode_final(q, k_new, c_new, m, l, acc, dw):
    bd = q.shape[0]
    return pl.pallas_call(
        _decode_final_kernel,
        in_specs=[_whole()] * 7,
        out_specs=_whole(),
        out_shape=jax.ShapeDtypeStruct((bd, H_A * V_HEAD), BF16),
        compiler_params=pltpu.CompilerParams(vmem_limit_bytes=VMEM_LIMIT),
        name="decode_final",
    )(q, k_new, c_new, m, l, acc, dw['w_uv_pair'])


def _merge_kernel(x_ref, attn_ref, ssm_ref, gates_ref, wa_ref, wb_ref, wo_ref, gffn_ref, wr_ref, br_ref,
                  h_ref, hn_ref, logit_ref):
    g = jax.nn.sigmoid(gates_ref[...])
    a = _dot(attn_ref[...], wa_ref[...])
    b = _dot(ssm_ref[...].astype(BF16), wb_ref[...])
    mixed = g[:, :D_MODEL] * a + g[:, D_MODEL:] * b
    h = x_ref[...] + _dot(mixed.astype(BF16), wo_ref[...])
    h_ref[...] = h
    hn = _rms(h, gffn_ref[...])
    hn_ref[...] = hn
    hi, mid, _ = _split3(hn)
    w_hi = wr_ref[0]
    w_lo = wr_ref[1]
    logit_ref[...] = _dot(hi, w_hi) + _dot(mid, w_hi) + _dot(hi, w_lo) + br_ref[...]


def _merge(x, attn, ssm, gates, mw, *, tm):
    t = x.shape[0]
    row = lambda w: pl.BlockSpec((tm, w), lambda i: (i, 0))
    consts = (mw['w_a_out'], mw['w_b_out'], mw['w_out'], mw['g_ffn'], mw['w_router2'], mw['b_router128'])
    return pl.pallas_call(
        _merge_kernel,
        grid=(t // tm,),
        in_specs=[row(D_MODEL), row(H_A * V_HEAD), row(D_INNER), row(2 * D_MODEL)] + [_whole()] * len(consts),
        out_specs=(row(D_MODEL), row(D_MODEL), row(LANE)),
        out_shape=(jax.ShapeDtypeStruct((t, D_MODEL), F32), jax.ShapeDtypeStruct((t, D_MODEL), F32),
                   jax.ShapeDtypeStruct((t, LANE), F32)),
        compiler_params=_cparams(("parallel",)),
        name="merge",
    )(x, attn, ssm, gates, *consts)


def _moe_kernel(be_ref, cnt_ref, seg_ref, tok_ref, asg_ref, run_ref, nxt_ref, x_hbm, wgu_hbm, bgu_ref, wd_hbm, bd_ref,
                y_hbm, xbuf, obuf, wgu_f, wd_f, wgu_b, wd_b, sem_in, sem_out, sem_w):
    i = pl.program_id(0)
    n = pl.num_programs(0)
    bm = xbuf.shape[1]
    dump0 = y_hbm.shape[0] - bm

    def weight_copies(expert, wslot):
        return (pltpu.make_async_copy(wgu_hbm.at[expert], wgu_f.at[wslot], sem_w.at[wslot]),
                pltpu.make_async_copy(wd_hbm.at[expert], wd_f.at[wslot], sem_w.at[wslot]))

    def in_copy(tok, slot, r):
        return pltpu.make_async_copy(x_hbm.at[pl.ds(tok, 1), :], xbuf.at[slot, pl.ds(r, 1), :], sem_in.at[slot])

    def out_copy(row, slot, r):
        return pltpu.make_async_copy(obuf.at[slot, pl.ds(r, 1), :], y_hbm.at[pl.ds(row, 1), :], sem_out.at[slot])

    def gather_loop(blk, slot):
        base = seg_ref[blk]

        def body(r, carry):
            in_copy(tok_ref[base + r], slot, r).start()
            return carry
        lax.fori_loop(0, bm, body, 0, unroll=8)

    def gather_inline(blk, slot):
        base = seg_ref[blk]
        for r in range(bm):
            in_copy(tok_ref[base + r], slot, r).start()

    def gather_wait(slot):
        pltpu.make_async_copy(xbuf.at[slot], xbuf.at[slot], sem_in.at[slot]).wait()

    def scattered(blk):
        return (blk < 0) | (cnt_ref[jnp.maximum(blk, 0)] > 0)

    def scatter_rows(blk):
        j = jnp.maximum(blk, 0)
        return seg_ref[j], jnp.where(blk >= 0, cnt_ref[j], 0)

    def scatter_loop(blk, slot):
        base, count = scatter_rows(blk)

        def body(r, carry):
            out_copy(jnp.where(r < count, asg_ref[base + r], dump0 + r), slot, r).start(priority=1)
            return carry
        lax.fori_loop(0, bm, body, 0, unroll=8)

    def scatter_inline(blk, slot):
        base, count = scatter_rows(blk)
        for r in range(bm):
            out_copy(jnp.where(count > r, asg_ref[base + r], dump0 + r), slot, r).start(priority=1)

    def scatter_wait(slot):
        pltpu.make_async_copy(obuf.at[slot], obuf.at[slot], sem_out.at[slot]).wait()

    valid = cnt_ref[i] > 0
    nxt_valid = (i + 1 < n) & (cnt_ref[jnp.minimum(i + 1, n - 1)] > 0)

    @pl.when(i == 0)
    def _():
        obuf[1] = jnp.zeros(obuf.shape[1:], F32)

    @pl.when((i == 0) & valid)
    def _():
        gather_loop(0, 0)

    @pl.when((i >= 1) & scattered(i - 2))
    def _():
        scatter_wait(i % 2)

    @pl.when((i == 0) & (run_ref[0] >= 0))
    def _():
        for cp in weight_copies(be_ref[0], 0):
            cp.start()

    for wslot in range(2):
        @pl.when((run_ref[i] >= 0) & (run_ref[i] % 2 == wslot))
        def _():
            for cp in weight_copies(0, wslot):
                cp.wait()
            wgu_b[...] = wgu_f[wslot].astype(BF16)
            wd_b[...] = wd_f[wslot].astype(BF16)

            @pl.when(nxt_ref[i] >= 0)
            def _():
                for cp in weight_copies(nxt_ref[i], 1 - wslot):
                    cp.start()

    def expert_block(prefetch_next, slot):
        gather_wait(slot)
        x = xbuf[slot].astype(BF16)
        scatter_inline(i - 1, 1 - slot)
        if prefetch_next:
            gather_inline(i + 1, 1 - slot)
        gu = _dot(x, wgu_b[...]) + bgu_ref[0]
        gate = jnp.minimum(gu[:, :D_FF], SWIGLU_LIMIT)
        up = jnp.clip(gu[:, D_FF:], -SWIGLU_LIMIT, SWIGLU_LIMIT)
        act = (up + 1.0) * gate * jax.nn.sigmoid(SWIGLU_ALPHA * gate)
        obuf[slot] = _dot(act.astype(BF16), wd_b[...]) + bd_ref[0]

    for parity in range(2):
        on_parity = valid & (i % 2 == parity)

        @pl.when(on_parity & nxt_valid)
        def _():
            expert_block(True, parity)

        @pl.when(on_parity & jnp.logical_not(nxt_valid))
        def _():
            expert_block(False, parity)

    @pl.when(jnp.logical_not(valid) & scattered(i - 1))
    def _():
        scatter_loop(i - 1, (i + 1) % 2)

    @pl.when((i == n - 1) & scattered(i - 1))
    def _():
        scatter_wait((i + 1) % 2)


def _moe_experts(block_e, block_cnt, seg_start, tok_sorted, asg_sorted, run_open, run_next, x, w_gate_up, b_gate_up,
                 w_down, b_down):
    n_blocks = block_e.shape[0]
    bm = MOE_BM
    n_assign = x.shape[0] * TOP_K
    b_spec = lambda shp: pl.BlockSpec(shp, lambda i, be, *_: (be[i], 0, 0))
    hbm = pl.BlockSpec(memory_space=pl.ANY)
    grid_spec = pltpu.PrefetchScalarGridSpec(
        num_scalar_prefetch=7,
        grid=(n_blocks,),
        in_specs=[hbm, hbm, b_spec((1, 1, 2 * D_FF)), hbm, b_spec((1, 1, D_MODEL))],
        out_specs=hbm,
        scratch_shapes=[pltpu.VMEM((2, bm, D_MODEL), F32), pltpu.VMEM((2, bm, D_MODEL), F32),
                        pltpu.VMEM((2, D_MODEL, 2 * D_FF), F32), pltpu.VMEM((2, D_FF, D_MODEL), F32),
                        pltpu.VMEM((D_MODEL, 2 * D_FF), BF16), pltpu.VMEM((D_FF, D_MODEL), BF16),
                        pltpu.SemaphoreType.DMA((2,)), pltpu.SemaphoreType.DMA((2,)),
                        pltpu.SemaphoreType.DMA((2,))],
    )
    return pl.pallas_call(
        _moe_kernel,
        grid_spec=grid_spec,
        out_shape=jax.ShapeDtypeStruct((n_assign + bm, D_MODEL), F32),
        compiler_params=_cparams(("arbitrary",)),
        name="moe_experts",
    )(block_e, block_cnt, seg_start, tok_sorted, asg_sorted, run_open, run_next, x, w_gate_up,
      b_gate_up.reshape(N_EXPERTS, 1, 2 * D_FF), w_down, b_down.reshape(N_EXPERTS, 1, D_MODEL))


def _ple_kernel(y_ref, hp_ref, hs_ref, gw_ref, pp_ref, ps_ref, gin_ref, wgate_ref, wple_ref, gple_ref,
                op_ref, os_ref):
    i = pl.program_id(0)
    n = pl.num_programs(0)
    tm = hp_ref.shape[0]
    is_sample = i == n - 1
    gw = gw_ref[...]
    y = y_ref[0:tm, :] * gw[:, 0:1]
    for k in range(1, TOP_K):
        y = y + y_ref[k * tm:(k + 1) * tm, :] * gw[:, k:k + 1]
    h2 = jnp.where(is_sample, hs_ref[...], hp_ref[...]) + y
    p_emb = jnp.where(is_sample, ps_ref[...], pp_ref[...])
    gate = jax.nn.sigmoid(_dot(_rms(h2, gin_ref[...]).astype(BF16), wgate_ref[...]))
    ple = _rms(_dot(p_emb.astype(BF16), wple_ref[...]), gple_ref[...]) * gate
    out = h2 + ple

    @pl.when(i < n - 1)
    def _():
        op_ref[...] = out

    @pl.when(i == n - 1)
    def _():
        os_ref[...] = out


def _combine_ple(y_rows, h_p, h_s, gate_w, p_p, p_s, pw):
    n_prompt = h_p.shape[0]
    tm = PLE_TM
    assert h_s.shape[0] == tm and n_prompt % tm == 0
    n_tiles = n_prompt // tm + 1
    prompt_row = lambda w: pl.BlockSpec((tm, w), lambda i: (jnp.minimum(i, n_tiles - 2), 0))
    sample_row = lambda w: pl.BlockSpec((tm, w), lambda i: (0, 0))
    return pl.pallas_call(
        _ple_kernel,
        grid=(n_tiles,),
        in_specs=[pl.BlockSpec((tm * TOP_K, D_MODEL), lambda i: (i, 0)), prompt_row(D_MODEL), sample_row(D_MODEL),
                  pl.BlockSpec((tm, LANE), lambda i: (i, 0)), prompt_row(PLE_DIM), sample_row(PLE_DIM)]
                 + [_whole()] * 4,
        out_specs=(prompt_row(D_MODEL), sample_row(D_MODEL)),
        out_shape=(jax.ShapeDtypeStruct((n_prompt, D_MODEL), F32), jax.ShapeDtypeStruct((tm, D_MODEL), F32)),
        compiler_params=_cparams(("arbitrary",)),
        name="combine_ple",
    )(y_rows, h_p, h_s, gate_w, p_p, p_s, pw['g_ple_in'], pw['w_ple_gate'], pw['w_ple'], pw['g_ple'])


def _pad_lanes(x, n):
    return jnp.pad(x, [(0, 0)] * (x.ndim - 1) + [(0, n - x.shape[-1])])


def _rot_cols(w):
    half = QK_ROPE // 2
    return jnp.concatenate([-w[..., half:], w[..., :half]], axis=-1)


def _prep_weights(g_mix_norm, w_in, g_q_a, w_q_b, g_kv_a, w_uk, w_uv, g_q_head, g_k_head, w_a_out, w_conv,
                  b_conv, dt_bias, a_log, d_skip, g_ssm_norm, w_b_out, w_out, g_ffn_norm, w_router, b_router,
                  g_ple_in, w_ple_gate, w_ple, g_ple):
    sizes = (Q_LORA, KV_LORA, QK_ROPE, D_INNER, CONV_DIM, H_B, 2 * D_MODEL)
    offs = np.concatenate([[0], np.cumsum(sizes)])
    wq, wkv, wkr, wz, wxbc, wdt, wg = [w_in[:, int(offs[i]):int(offs[i + 1])] for i in range(7)]
    zc = lambda n: jnp.zeros((D_MODEL, n), F32)
    kr128 = jnp.concatenate([zc(QK_NOPE), wkr, zc(LANE - QK_HEAD)], axis=1)
    krrot = jnp.concatenate([zc(QK_NOPE), _rot_cols(wkr), zc(LANE - QK_HEAD)], axis=1)
    w_lat = jnp.concatenate([wq, wkv, kr128, krrot, _pad_lanes(wdt, LANE)], axis=1).astype(BF16)

    wqb = w_q_b.reshape(Q_LORA, H_A, QK_HEAD)
    nope, rope_w = wqb[..., :QK_NOPE], wqb[..., QK_NOPE:]
    z_nope = jnp.zeros_like(nope)
    q128 = _pad_lanes(jnp.concatenate([nope, rope_w], axis=-1), LANE).reshape(Q_LORA, H_A * LANE)
    qrot = _pad_lanes(jnp.concatenate([z_nope, _rot_cols(rope_w)], axis=-1), LANE).reshape(Q_LORA, H_A * LANE)
    w_qb = jnp.concatenate([q128, qrot], axis=1).astype(BF16)

    wuk3 = w_uk.reshape(KV_LORA, H_A, QK_NOPE)
    w_uk_pad = _pad_lanes(wuk3, LANE).reshape(KV_LORA, H_A * LANE).astype(BF16)
    sub = LANE // H_A
    w_uk_perm = wuk3.reshape(KV_LORA, H_A, QK_NOPE // sub, sub).transpose(0, 2, 1, 3)
    w_uk_perm = w_uk_perm.reshape(KV_LORA, H_A * QK_NOPE).astype(BF16)

    wuv3 = w_uv.reshape(KV_LORA, H_A, V_HEAD)
    even = (jnp.arange(H_A) % 2 == 0)[None, :, None]
    zv = jnp.zeros_like(wuv3)
    w_uv_ext = jnp.where(even, jnp.concatenate([wuv3, zv], axis=-1), jnp.concatenate([zv, wuv3], axis=-1))
    ones_lane = np.zeros((H_A, LANE), np.float32)
    ones_lane[0::2, V_HEAD] = 1.0
    ones_lane[1::2, 0] = 1.0
    front = dict(
        g_mix=g_mix_norm.reshape(1, -1), w_lat=w_lat, w_z=wz.astype(BF16), w_xbc=wxbc.astype(BF16),
        w_g=wg.astype(BF16), g_q_a=g_q_a.reshape(1, -1), w_qb=w_qb, g_kv_a=g_kv_a.reshape(1, -1),
        w_uk_pad=w_uk_pad, w_uv_ext=w_uv_ext.reshape(KV_LORA, H_A * LANE).astype(BF16),
        v_ones=jnp.asarray(ones_lane.reshape(1, H_A * LANE)),
        g_q128=_pad_lanes(g_q_head.reshape(1, -1), LANE) * Q_SCALE,
        g_k128=_pad_lanes(g_k_head.reshape(1, -1), LANE),
    )

    lane_head = np.arange(LANE) // sub
    seg = (lane_head[None, :] == np.arange(H_A)[:, None]).astype(np.float32)
    seg_t = np.zeros((2 * H_A, LANE), np.float32)
    seg_t[:H_A] = seg
    ones_rows = np.zeros((2 * H_A, QK_ROPE), np.float32)
    ones_rows[:H_A] = 1.0
    w_uv_pair = w_uv_ext
    decode = dict(
        g_k128=front['g_k128'], gk_rope=g_k_head[QK_NOPE:].reshape(1, QK_ROPE), w_uk_perm=w_uk_perm,
        w_uk_heads=wuk3.transpose(1, 2, 0).astype(BF16),
        seg_t=jnp.asarray(seg_t, BF16), ones_rows=jnp.asarray(ones_rows, BF16),
        w_uv_pair=w_uv_pair.transpose(1, 0, 2).astype(BF16),
    )

    expand = (np.arange(D_INNER)[None, :] // SSM_HEAD == np.arange(LANE)[:, None]).astype(np.float32)
    tri = (np.arange(SSD_CHUNK)[None, :] <= np.arange(SSD_CHUNK)[:, None]).astype(np.float32)
    ssm = dict(
        w_conv=w_conv, b_conv=b_conv.reshape(1, -1), dt_bias128=_pad_lanes(dt_bias.reshape(1, -1), LANE),
        a_head128=_pad_lanes(-jnp.exp(a_log).reshape(1, -1), LANE),
        d_skip_x=jnp.repeat(d_skip, SSM_HEAD).reshape(1, -1), g_norm=g_ssm_norm.reshape(1, -1),
        tri=jnp.asarray(tri, BF16), expand=jnp.asarray(expand, BF16),
    )

    wr = _pad_lanes(w_router, LANE)
    wr_hi = wr.astype(BF16)
    wr_lo = (wr - wr_hi.astype(F32)).astype(BF16)
    merge = dict(
        w_a_out=w_a_out.astype(BF16), w_b_out=w_b_out.astype(BF16), w_out=w_out.astype(BF16),
        g_ffn=g_ffn_norm.reshape(1, -1), w_router2=jnp.stack([wr_hi, wr_lo]),
        b_router128=_pad_lanes(b_router.reshape(1, -1), LANE),
    )
    ple = dict(g_ple_in=g_ple_in.reshape(1, -1), w_ple_gate=w_ple_gate.astype(BF16), w_ple=w_ple.astype(BF16),
               g_ple=g_ple.reshape(1, -1))
    return front, decode, ssm, merge, ple


def _rope_tables(pos):
    half = QK_ROPE // 2
    inv = ROPE_THETA ** (-jnp.arange(half, dtype=F32) / half)
    ang = pos.astype(F32)[:, None] * inv[None, :]
    cos, sin = jnp.cos(ang), jnp.sin(ang)
    n = pos.shape[0]
    cos128 = jnp.concatenate([jnp.ones((n, QK_NOPE), F32), cos, cos, jnp.ones((n, LANE - QK_HEAD), F32)], axis=1)
    sin128 = jnp.concatenate([jnp.zeros((n, QK_NOPE), F32), sin, sin, jnp.zeros((n, LANE - QK_HEAD), F32)], axis=1)
    return cos128, sin128


def _route(logits, n_tok):
    top_val, top_idx = lax.top_k(logits[:, :N_EXPERTS], TOP_K)
    gate_w = jax.nn.softmax(top_val, axis=-1)
    n_assign = n_tok * TOP_K
    flat_e = top_idx.reshape(-1).astype(jnp.int32)
    counts = jnp.sum((flat_e[:, None] == jnp.arange(N_EXPERTS, dtype=jnp.int32)[None, :]).astype(jnp.int32), axis=0)
    bm = MOE_BM
    padded = (counts + bm - 1) // bm * bm
    pend = jnp.cumsum(padded)
    pstarts = pend - padded
    starts = jnp.cumsum(counts) - counts
    idx_bits = max(1, (n_assign - 1).bit_length())
    key = jnp.sort(flat_e * (1 << idx_bits) + jnp.arange(n_assign, dtype=jnp.int32))
    asg = jnp.pad(key & ((1 << idx_bits) - 1), (0, bm)).astype(jnp.int32)
    tok_sorted = asg // TOP_K
    tile_rows = PLE_TM * TOP_K
    asg_sorted = (asg // tile_rows) * tile_rows + (asg % TOP_K) * PLE_TM + tok_sorted % PLE_TM
    n_blocks = -(-n_assign // bm) + N_EXPERTS
    blk_start = jnp.arange(n_blocks, dtype=jnp.int32) * bm
    block_e = jnp.minimum(jnp.sum((pend[None, :] <= blk_start[:, None]).astype(jnp.int32), axis=1), N_EXPERTS - 1)
    offset = blk_start - pstarts[block_e]
    block_cnt = jnp.where(blk_start < pend[-1], jnp.clip(counts[block_e] - offset, 0, bm), 0).astype(jnp.int32)
    seg_start = jnp.clip(starts[block_e] + offset, 0, n_assign).astype(jnp.int32)
    block_e = block_e.astype(jnp.int32)
    opens = (block_cnt > 0) & (offset == 0)
    run_open = jnp.where(opens, jnp.cumsum(opens.astype(jnp.int32)) - 1, -1).astype(jnp.int32)
    e_ids = jnp.arange(N_EXPERTS, dtype=jnp.int32)
    later = (e_ids[None, :] > e_ids[:, None]) & (counts[None, :] > 0)
    next_expert = jnp.min(jnp.where(later, e_ids[None, :], N_EXPERTS), axis=1)
    run_next = jnp.where(opens & (next_expert[block_e] < N_EXPERTS), next_expert[block_e], -1).astype(jnp.int32)
    return gate_w, tok_sorted, asg_sorted, block_e, block_cnt, seg_start, run_open, run_next


def kernel(x_prompt, x_sample, p_prompt, p_sample, cache_kv_latent, cache_k_rope, page_table, state_conv, state_ssm, g_mix_norm, w_in, g_q_a, w_q_b, g_kv_a, w_uk, w_uv, g_q_head, g_k_head, w_a_out, w_conv, b_conv, dt_bias, a_log, d_skip, g_ssm_norm, w_b_out, w_out, g_ffn_norm, w_router, b_router, w_gate_up, b_gate_up, w_down, b_down, g_ple_in, w_ple_gate, w_ple, g_ple):
    depth = g_mix_norm.shape[0]
    assert depth == 1, "one layer"
    b, s, _ = x_prompt.shape
    bd, sd, _ = x_sample.shape
    assert sd == 1, "one new token per sample sequence"
    n_pages = page_table.shape[1]
    page_size = cache_kv_latent.shape[2]
    assert n_pages % DEC_PPS == 0 and s % SSD_CHUNK == 0
    tp = b * s

    fw, dw, sw, mw, pw = _prep_weights(
        g_mix_norm[0], w_in[0], g_q_a[0], w_q_b[0], g_kv_a[0], w_uk[0], w_uv[0], g_q_head[0], g_k_head[0],
        w_a_out[0], w_conv[0], b_conv[0], dt_bias[0], a_log[0], d_skip[0], g_ssm_norm[0], w_b_out[0], w_out[0],
        g_ffn_norm[0], w_router[0], b_router[0], g_ple_in[0], w_ple_gate[0], w_ple[0], g_ple[0])

    xp = x_prompt.reshape(tp, D_MODEL)
    cos_p, sin_p = _rope_tables(jnp.arange(s))
    tm_p = min(FRONT_TM, s)
    bound = QK_HEAD * jnp.max(jnp.abs(fw['g_q128'])) * jnp.max(jnp.abs(fw['g_k128']))
    bound = (1.02 * bound + 1.0).astype(BF16).astype(F32)
    bias_lane = (jnp.arange(LANE) == QK_HEAD).astype(F32).reshape(1, LANE)
    q_p, k_p, v_p, c_p, kr_p, z_p, xbc_p, dt_p, gates_p = _front(
        xp, cos_p, sin_p, fw, -bound * bias_lane, bias_lane, tm=tm_p, q_dtype=BF16, pos_blocks=s // tm_p)
    qkv = (q_p.reshape(b, s, -1), k_p.reshape(b, s, -1), v_p.reshape(b, s, -1))
    attn_p = lax.cond(bound <= ATTN_MAX_BOUND,
                      lambda q, k, v: _prompt_attention(q, k, v, b, s, bounded=True),
                      lambda q, k, v: _prompt_attention(q, k, v, b, s, bounded=False), *qkv)
    ssm_p, hfin_p = _ssd_prompt(xbc_p, z_p, dt_p, sw, b, s)
    h_p, hn_p, logit_p = _merge(xp, attn_p.reshape(tp, -1), ssm_p, gates_p, mw, tm=min(MERGE_TM, tp))

    xs = x_sample.reshape(bd, D_MODEL)
    cos_s, sin_s = _rope_tables(jnp.full((bd,), n_pages * page_size, jnp.int32))
    no_pad = jnp.zeros((1, LANE), F32)
    q_s, k_s, _, c_s, kr_s, z_s, xbc_s, dt_s, gates_s = _front(
        xs, cos_s, sin_s, fw, no_pad, no_pad, tm=bd, q_dtype=F32, pos_blocks=1)
    q3 = q_s.reshape(bd, H_A, LANE)
    q_abs = _absorb_queries(q_s, dw).reshape(bd, H_A, KV_LORA)
    qr = q3[:, :, QK_NOPE:QK_HEAD]
    qr_mat = jnp.concatenate([jnp.zeros_like(qr), qr], axis=1)
    acc, m_run, l_run = _decode_attention(page_table, q_abs, qr_mat, cache_kv_latent[0],
                                          jnp.swapaxes(cache_k_rope[0], 1, 2), dw)
    attn_s = _decode_final(q_s, k_s, c_s, m_run[:, :, 0], l_run[:, :, 0], acc.reshape(bd, -1), dw)
    ssm_s, hnew_s = _ssm_sample(xbc_s, state_conv[0], z_s, dt_s,
                                state_ssm[0].reshape(bd, D_INNER, D_STATE), sw)
    h_s, hn_s, logit_s = _merge(xs, attn_s, ssm_s, gates_s, mw, tm=bd)

    hn_all = jnp.concatenate([hn_p, hn_s], axis=0)
    logits = jnp.concatenate([logit_p, logit_s], axis=0)
    n_tok = tp + bd
    gate_w, tok_sorted, asg_sorted, block_e, block_cnt, seg_start, run_open, run_next = _route(logits, n_tok)
    y_rows = _moe_experts(block_e, block_cnt, seg_start, tok_sorted, asg_sorted, run_open, run_next, hn_all,
                          w_gate_up[0], b_gate_up[0], w_down[0], b_down[0])
    out_p, out_s = _combine_ple(y_rows, h_p, h_s, _pad_lanes(gate_w, LANE), p_prompt[0].reshape(tp, PLE_DIM),
                                p_sample[0].reshape(bd, PLE_DIM), pw)

    y_prompt = out_p.reshape(b, s, D_MODEL)
    y_sample = out_s.reshape(bd, sd, D_MODEL)
    new_c_p = c_p.reshape(1, b, s, KV_LORA)
    new_kr_p = kr_p[:, QK_NOPE:QK_HEAD].reshape(1, b, s, QK_ROPE)
    conv_p = xbc_p.reshape(b, s, CONV_DIM)[:, s - (CONV_W - 1):].reshape(1, b, CONV_W - 1, CONV_DIM)
    ssm_state_p = hfin_p.reshape(1, b, H_B, SSM_HEAD, D_STATE)
    new_c_s = c_s.reshape(1, bd, sd, KV_LORA)
    new_kr_s = kr_s[:, QK_NOPE:QK_HEAD].reshape(1, bd, sd, QK_ROPE)
    conv_s = jnp.concatenate([state_conv[0][:, 1:], xbc_s[:, None, :]], axis=1).reshape(1, bd, CONV_W - 1, CONV_DIM)
    ssm_state_s = hnew_s.reshape(1, bd, H_B, SSM_HEAD, D_STATE)
    return (y_prompt, y_sample, new_c_p, new_kr_p, conv_p, ssm_state_p, new_c_s, new_kr_s, conv_s, ssm_state_s)
```

```python
import functools
import math

import jax
import jax.numpy as jnp
import numpy as np
from jax import lax
from jax.experimental import pallas as pl
from jax.experimental.pallas import tpu as pltpu

F32 = jnp.float32
BF16 = jnp.bfloat16

D_MODEL = 1024
H_A = 16
Q_LORA = 384
KV_LORA = 256
QK_NOPE = 64
QK_ROPE = 32
QK_HEAD = QK_NOPE + QK_ROPE
V_HEAD = 64
ROPE_THETA = 10000.0
D_INNER = 2 * D_MODEL
SSM_HEAD = 64
H_B = D_INNER // SSM_HEAD
N_GROUPS = 4
D_STATE = 128
CONV_W = 4
CONV_DIM = D_INNER + 2 * N_GROUPS * D_STATE
SSD_CHUNK = 128
N_EXPERTS = 32
TOP_K = 4
D_FF = D_MODEL
SWIGLU_LIMIT = 7.0
SWIGLU_ALPHA = 1.702
PLE_DIM = 256
EPS = 1e-6

LANE = 128
HEAD_PAD = LANE
GROUP_W = D_INNER // N_GROUPS
HEADS_PER_GROUP = H_B // N_GROUPS
Q_SCALE = QK_HEAD ** -0.5 * math.log2(math.e)
VMEM_LIMIT = 56 * 1024 * 1024

FRONT_TM = 256
ATTN_TQ = 1024
ATTN_MAX_BOUND = 40.0
MERGE_TM = 256
MOE_BM = 256
PLE_TM = 128
DEC_PPS = 16
DEC_SUB = 4


def _dot(a, b):
    return jnp.dot(a, b, preferred_element_type=F32)


def _dot_nt(a, b):
    return lax.dot_general(a, b, (((1,), (1,)), ((), ())), preferred_element_type=F32)


def _dot_tn(a, b):
    return lax.dot_general(a, b, (((0,), (0,)), ((), ())), preferred_element_type=F32)


def _split3(x):
    hi = x.astype(BF16)
    r1 = x - hi.astype(F32)
    mid = r1.astype(BF16)
    lo = (r1 - mid.astype(F32)).astype(BF16)
    return hi, mid, lo


def _dot_f32_lhs(x, e, terms=3):
    parts = _split3(x)[:terms]
    out = _dot(parts[0], e)
    for part in parts[1:]:
        out = out + _dot(part, e)
    return out


def _dot_f32_rhs(e, x):
    hi, mid, lo = _split3(x)
    return _dot(e, hi) + _dot(e, mid) + _dot(e, lo)


def _rms(x, g):
    return x * lax.rsqrt(jnp.mean(x * x, axis=-1, keepdims=True) + EPS) * g


def _silu(x):
    return x * jax.nn.sigmoid(x)


def _cparams(sem, vmem=VMEM_LIMIT):
    return pltpu.CompilerParams(dimension_semantics=sem, vmem_limit_bytes=vmem)


def _whole():
    return pl.BlockSpec(memory_space=pltpu.VMEM)


def _head_norm(xh, g):
    ss = jnp.sum(xh * xh, axis=-1, keepdims=True)
    return xh * lax.rsqrt(ss * (1.0 / QK_HEAD) + EPS) * g


def _front_kernel(x_ref, cos_ref, sin_ref, gmix_ref, wlat_ref, wz_ref, wxbc_ref, wg_ref,
                  gqa_ref, wqb_ref, gkva_ref, wuk_ref, wuv_ref, gq_ref, gk_ref, qpad_ref, kpad_ref, vpad_ref,
                  q_ref, k_ref, v_ref, c_ref, kr_ref, z_ref, xbc_ref, dt_ref, gates_ref):
    x = x_ref[...]
    ub = _rms(x, gmix_ref[...]).astype(BF16)
    z_ref[...] = _dot(ub, wz_ref[...])
    xbc_ref[...] = _dot(ub, wxbc_ref[...])
    gates_ref[...] = _dot(ub, wg_ref[...])
    lat = _dot(ub, wlat_ref[...])
    q_lat = lat[:, :Q_LORA]
    kv_lat = lat[:, Q_LORA:Q_LORA + KV_LORA]
    o = Q_LORA + KV_LORA
    kr_raw = lat[:, o:o + LANE]
    kr_rot = lat[:, o + LANE:o + 2 * LANE]
    dt_ref[...] = lat[:, o + 2 * LANE:o + 3 * LANE]
    cos = cos_ref[...]
    sin = sin_ref[...]
    c = _rms(kv_lat, gkva_ref[...])
    c_ref[...] = c
    kr = kr_raw * cos + kr_rot * sin
    kr_ref[...] = kr
    qn = _rms(q_lat, gqa_ref[...]).astype(BF16)
    q2 = _dot(qn, wqb_ref[...])
    nq = H_A * HEAD_PAD
    gq = gq_ref[...]
    qpad = qpad_ref[...]
    kpad = kpad_ref[...]
    for h in range(H_A):
        lo, hi = h * HEAD_PAD, (h + 1) * HEAD_PAD
        qh = q2[:, lo:hi] * cos + q2[:, nq + lo:nq + hi] * sin
        q_ref[:, lo:hi] = (_head_norm(qh, gq) + qpad).astype(q_ref.dtype)
    cb = c.astype(BF16)
    kn = _dot(cb, wuk_ref[...])
    gk = gk_ref[...]
    for h in range(H_A):
        lo, hi = h * HEAD_PAD, (h + 1) * HEAD_PAD
        k_ref[:, lo:hi] = (_head_norm(kn[:, lo:hi] + kr, gk) + kpad).astype(k_ref.dtype)
    v_ref[...] = (_dot(cb, wuv_ref[...]) + vpad_ref[...]).astype(v_ref.dtype)


def _front(x, cos, sin, fw, qpad, kpad, *, tm, q_dtype, pos_blocks):
    t = x.shape[0]
    nt = t // tm
    row = lambda w: pl.BlockSpec((tm, w), lambda i: (i, 0))
    pos_spec = pl.BlockSpec((tm, LANE), lambda i: (i % pos_blocks, 0))
    nq = H_A * HEAD_PAD
    out_shape = (
        jax.ShapeDtypeStruct((t, nq), q_dtype),
        jax.ShapeDtypeStruct((t, nq), BF16),
        jax.ShapeDtypeStruct((t, nq), BF16),
        jax.ShapeDtypeStruct((t, KV_LORA), F32),
        jax.ShapeDtypeStruct((t, LANE), F32),
        jax.ShapeDtypeStruct((t, D_INNER), F32),
        jax.ShapeDtypeStruct((t, CONV_DIM), F32),
        jax.ShapeDtypeStruct((t, LANE), F32),
        jax.ShapeDtypeStruct((t, 2 * D_MODEL), F32),
    )
    out_specs = (row(nq), row(nq), row(nq), row(KV_LORA), row(LANE), row(D_INNER),
                 row(CONV_DIM), row(LANE), row(2 * D_MODEL))
    weights = (fw['g_mix'], fw['w_lat'], fw['w_z'], fw['w_xbc'], fw['w_g'], fw['g_q_a'], fw['w_qb'],
               fw['g_kv_a'], fw['w_uk_pad'], fw['w_uv_ext'], fw['g_q128'], fw['g_k128'], qpad, kpad, fw['v_ones'])
    return pl.pallas_call(
        _front_kernel,
        grid=(nt,),
        in_specs=[row(D_MODEL), pos_spec, pos_spec] + [_whole()] * len(weights),
        out_specs=out_specs,
        out_shape=out_shape,
        compiler_params=_cparams(("parallel",)),
        name="front",
    )(x, cos, sin, *weights)


def _attn_kernel(qi_ref, ki_ref, q_ref, k_ref, v_ref, o_ref, acc_sc, *m_scratch, bounded):
    step = pl.program_id(2)
    qi = qi_ref[step]
    ki = ki_ref[step]
    tq = q_ref.shape[0]
    half = tq // 2

    @pl.when(ki == 0)
    def _():
        acc_sc[...] = jnp.zeros(acc_sc.shape, F32)
        if not bounded:
            m_scratch[0][...] = jnp.full(m_scratch[0].shape, -jnp.inf, F32)

    def update(q0, qn, k0, kn, masked):
        for h in range(2):
            cols = slice(h * HEAD_PAD, (h + 1) * HEAD_PAD)
            s = _dot_nt(q_ref[q0:q0 + qn, cols], k_ref[k0:k0 + kn, cols])
            v = v_ref[k0:k0 + kn, cols]
            if masked:
                keep = (lax.broadcasted_iota(jnp.int32, (qn, kn), 1) <= lax.broadcasted_iota(jnp.int32, (qn, kn), 0))
            if bounded:
                p = jnp.exp2(s)
                if masked:
                    p = jnp.where(keep, p, 0.0)
                acc_sc[h, q0:q0 + qn, :] += _dot(p.astype(BF16), v)
            else:
                m_sc = m_scratch[0]
                if masked:
                    s = jnp.where(keep, s, -jnp.inf)
                m_prev = m_sc[h, q0:q0 + qn, :]
                m_next = jnp.maximum(m_prev, jnp.max(s, axis=-1, keepdims=True))
                alpha = jnp.exp2(m_prev - m_next)
                p = jnp.exp2(s - m_next[:, :1])
                acc_sc[h, q0:q0 + qn, :] = alpha * acc_sc[h, q0:q0 + qn, :] + _dot(p.astype(BF16), v)
                m_sc[h, q0:q0 + qn, :] = m_next

    @pl.when(ki < qi)
    def _():
        update(0, tq, 0, tq, False)

    @pl.when(ki == qi)
    def _():
        update(0, half, 0, half, True)
        update(half, half, 0, half, False)
        update(half, half, half, half, True)
        lane = lax.broadcasted_iota(jnp.int32, (tq, LANE), 1)
        a0 = acc_sc[0]
        a1 = acc_sc[1]
        o0 = a0 / a0[:, V_HEAD:V_HEAD + 1]
        o1 = a1 / a1[:, 0:1]
        o_ref[...] = jnp.where(lane < V_HEAD, o0, o1).astype(o_ref.dtype)


def _prompt_attention(q, k, v, b, s, *, bounded):
    tq = min(ATTN_TQ, s)
    nq = s // tq
    pairs = [(i, j) for i in range(nq) for j in range(i + 1)]
    qi_tab = jnp.asarray([p[0] for p in pairs], jnp.int32)
    ki_tab = jnp.asarray([p[1] for p in pairs], jnp.int32)
    blk = lambda sel: pl.BlockSpec((None, tq, 2 * HEAD_PAD), lambda bi, h, t, qt, kt: (bi, sel(qt, kt)[t], h))
    scratch = [pltpu.VMEM((2, tq, LANE), F32)]
    if not bounded:
        scratch.append(pltpu.VMEM((2, tq, LANE), F32))
    grid_spec = pltpu.PrefetchScalarGridSpec(
        num_scalar_prefetch=2,
        grid=(b, H_A // 2, len(pairs)),
        in_specs=[blk(lambda qt, kt: qt), blk(lambda qt, kt: kt), blk(lambda qt, kt: kt)],
        out_specs=pl.BlockSpec((None, tq, 2 * V_HEAD), lambda bi, h, t, qt, kt: (bi, qt[t], h)),
        scratch_shapes=scratch,
    )
    return pl.pallas_call(
        functools.partial(_attn_kernel, bounded=bounded),
        grid_spec=grid_spec,
        out_shape=jax.ShapeDtypeStruct((b, s, H_A * V_HEAD), BF16),
        compiler_params=_cparams(("parallel", "parallel", "arbitrary")),
        name="prompt_attn_bounded" if bounded else "prompt_attn",
    )(qi_tab, ki_tab, q, k, v)


def _ssd_kernel(xbc_ref, z_ref, dt_ref, wconv_ref, bconv_ref, dtb_ref, ah_ref, dskip_ref, gn_ref,
                tri_ref, exp_ref, y_ref, hfin_ref, xbuf, state):
    ci = pl.program_id(1)
    nc = pl.num_programs(1)
    L = SSD_CHUNK

    @pl.when(ci == 0)
    def _():
        xbuf[0:8, :] = jnp.zeros((8, CONV_DIM), F32)
        state[...] = jnp.zeros(state.shape, F32)

    xbuf[8:8 + L, :] = xbc_ref[...]
    conv = bconv_ref[...] + xbuf[8:8 + L, :] * wconv_ref[3:4, :]
    for w in range(CONV_W - 1):
        sh = CONV_W - 1 - w
        conv = conv + xbuf[8 - sh:8 - sh + L, :] * wconv_ref[w:w + 1, :]
    xbuf[0:8, :] = xbuf[L:L + 8, :]
    xc = _silu(conv)
    xs = xc[:, :D_INNER]

    dt = jax.nn.softplus(dt_ref[...] + dtb_ref[...])
    a = dt * ah_ref[...]
    a_cum = _dot_f32_rhs(tri_ref[...], a)
    a_cum_t = a_cum.T
    a_last = a_cum[L - 1:L, :]
    ex = exp_ref[...]
    dt_x = _dot_f32_lhs(dt, ex, terms=2)
    dfs_x = _dot_f32_lhs(jnp.exp(a_cum), ex, terms=2)
    dte_x = _dot_f32_lhs(jnp.exp(a_last - a_cum), ex, terms=2)
    xdt = xs * dt_x
    xdt_b = xdt.astype(BF16)
    xw_b = (xdt * dte_x).astype(BF16)
    chunk_decay = jnp.exp(a_cum_t[:, L - 1:L])

    row = lax.broadcasted_iota(jnp.int32, (L, L), 0)
    col = lax.broadcasted_iota(jnp.int32, (L, L), 1)
    causal = col <= row
    lane = lax.broadcasted_iota(jnp.int32, (L, LANE), 1)
    for g in range(N_GROUPS):
        bm = xc[:, D_INNER + g * D_STATE:D_INNER + (g + 1) * D_STATE].astype(BF16)
        cm = xc[:, D_INNER + (N_GROUPS + g) * D_STATE:D_INNER + (N_GROUPS + g + 1) * D_STATE].astype(BF16)
        cb = _dot_nt(cm, bm)
        c0, c1 = g * GROUP_W, (g + 1) * GROUP_W
        st_prev = state[c0:c1, :]
        y_off = _dot_nt(cm, st_prev.astype(BF16)) * dfs_x[:, c0:c1]
        for j in range(HEADS_PER_GROUP // 2):
            h0 = g * HEADS_PER_GROUP + 2 * j
            x2 = xdt_b[:, h0 * SSM_HEAD:(h0 + 2) * SSM_HEAD]
            ys = []
            for hh in (h0, h0 + 1):
                seg = a_cum[:, hh:hh + 1] - a_cum_t[hh:hh + 1, :]
                sc = jnp.where(causal, cb * jnp.exp(seg), 0.0)
                ys.append(_dot(sc.astype(BF16), x2))
            y2 = jnp.where(lane < SSM_HEAD, ys[0], ys[1])
            lo = h0 * SSM_HEAD
            y_ref[:, lo:lo + LANE] = y2 + y_off[:, lo - c0:lo - c0 + LANE]
        st_new = _dot_tn(xw_b[:, c0:c1], bm)
        carry = jnp.concatenate(
            [jnp.broadcast_to(chunk_decay[hh:hh + 1, :], (SSM_HEAD, D_STATE))
             for hh in range(g * HEADS_PER_GROUP, (g + 1) * HEADS_PER_GROUP)], axis=0)
        state[c0:c1, :] = st_prev * carry + st_new

    y = y_ref[...] + dskip_ref[...] * xs
    yg = y * _silu(z_ref[...])
    gn = gn_ref[...]
    for g in range(N_GROUPS):
        c0, c1 = g * GROUP_W, (g + 1) * GROUP_W
        blk = yg[:, c0:c1]
        y_ref[:, c0:c1] = blk * lax.rsqrt(jnp.mean(blk * blk, axis=-1, keepdims=True) + EPS) * gn[:, c0:c1]

    @pl.when(ci == nc - 1)
    def _():
        hfin_ref[...] = state[...]


def _ssd_prompt(xbc, z, dt, sw, b, s):
    nc = s // SSD_CHUNK
    row = lambda w: pl.BlockSpec((SSD_CHUNK, w), lambda bi, ci: (bi * nc + ci, 0))
    consts = (sw['w_conv'], sw['b_conv'], sw['dt_bias128'], sw['a_head128'], sw['d_skip_x'], sw['g_norm'],
              sw['tri'], sw['expand'])
    return pl.pallas_call(
        _ssd_kernel,
        grid=(b, nc),
        in_specs=[row(CONV_DIM), row(D_INNER), row(LANE)] + [_whole()] * len(consts),
        out_specs=(row(D_INNER), pl.BlockSpec((None, D_INNER, D_STATE), lambda bi, ci: (bi, 0, 0))),
        out_shape=(jax.ShapeDtypeStruct((b * s, D_INNER), F32),
                   jax.ShapeDtypeStruct((b, D_INNER, D_STATE), F32)),
        scratch_shapes=[pltpu.VMEM((SSD_CHUNK + 8, CONV_DIM), F32), pltpu.VMEM((D_INNER, D_STATE), F32)],
        compiler_params=_cparams(("parallel", "arbitrary")),
        name="ssd_prompt",
    )(xbc, z, dt, *consts)


def _ssm_step_kernel(xbc_ref, cbuf_ref, z_ref, dt_ref, h_ref, wconv_ref, bconv_ref, dtb_ref, ah_ref,
                     dskip_ref, gn_ref, exp_ref, y_ref, hnew_ref):
    conv = bconv_ref[...] + xbc_ref[0] * wconv_ref[3:4, :]
    for w in range(CONV_W - 1):
        conv = conv + cbuf_ref[0, w:w + 1, :] * wconv_ref[w:w + 1, :]
    xc = _silu(conv)
    xs = xc[:, :D_INNER]
    dt = jax.nn.softplus(dt_ref[0] + dtb_ref[...])
    da = jnp.exp(dt * ah_ref[...])
    ex = exp_ref[...]
    xdt = xs * _dot_f32_lhs(jnp.broadcast_to(dt, (8, LANE)), ex)[0:1, :]

    eye = lax.broadcasted_iota(jnp.int32, (LANE, LANE), 0) == lax.broadcasted_iota(jnp.int32, (LANE, LANE), 1)

    def to_col(rowvec):
        return jnp.sum(jnp.where(eye, jnp.broadcast_to(rowvec, (LANE, LANE)), 0.0), axis=-1, keepdims=True)

    ys = []
    for g in range(N_GROUPS):
        bt = xc[:, D_INNER + g * D_STATE:D_INNER + (g + 1) * D_STATE]
        ct = xc[:, D_INNER + (N_GROUPS + g) * D_STATE:D_INNER + (N_GROUPS + g + 1) * D_STATE]
        ct8 = jnp.broadcast_to(ct, (8, D_STATE)).astype(BF16)
        for j in range(GROUP_W // LANE):
            r0 = g * GROUP_W + j * LANE
            h0 = r0 // SSM_HEAD
            h_old = h_ref[0, r0:r0 + LANE, :]
            decay = jnp.concatenate([jnp.broadcast_to(da[:, hh:hh + 1], (SSM_HEAD, D_STATE)) for hh in (h0, h0 + 1)],
                                    axis=0)
            h_new = h_old * decay + to_col(xdt[:, r0:r0 + LANE]) * bt
            hnew_ref[0, r0:r0 + LANE, :] = h_new
            ys.append(_dot_nt(ct8, h_new.astype(BF16))[0:1, :])
    y = jnp.concatenate(ys, axis=-1) + dskip_ref[...] * xs
    yg = y * _silu(z_ref[0])
    gn = gn_ref[...]
    for g in range(N_GROUPS):
        c0, c1 = g * GROUP_W, (g + 1) * GROUP_W
        blk = yg[:, c0:c1]
        y_ref[0, :, c0:c1] = blk * lax.rsqrt(jnp.mean(blk * blk, axis=-1, keepdims=True) + EPS) * gn[:, c0:c1]


def _ssm_sample(xbc, conv_buf, z, dt, h, sw):
    bd = xbc.shape[0]
    vec = lambda w: pl.BlockSpec((1, 1, w), lambda i: (i, 0, 0))
    consts = (sw['w_conv'], sw['b_conv'], sw['dt_bias128'], sw['a_head128'], sw['d_skip_x'], sw['g_norm'],
              sw['expand'])
    y, h_new = pl.pallas_call(
        _ssm_step_kernel,
        grid=(bd,),
        in_specs=[vec(CONV_DIM), pl.BlockSpec((1, CONV_W - 1, CONV_DIM), lambda i: (i, 0, 0)), vec(D_INNER),
                  vec(LANE), pl.BlockSpec((1, D_INNER, D_STATE), lambda i: (i, 0, 0))] + [_whole()] * len(consts),
        out_specs=(vec(D_INNER), pl.BlockSpec((1, D_INNER, D_STATE), lambda i: (i, 0, 0))),
        out_shape=(jax.ShapeDtypeStruct((bd, 1, D_INNER), F32),
                   jax.ShapeDtypeStruct((bd, D_INNER, D_STATE), F32)),
        compiler_params=_cparams(("parallel",)),
        name="ssm_sample",
    )(xbc.reshape(bd, 1, CONV_DIM), conv_buf, z.reshape(bd, 1, D_INNER), dt.reshape(bd, 1, LANE), h, *consts)
    return y.reshape(bd, D_INNER), h_new


def _decode_kernel(pt_ref, qa_ref, qr_ref, gkr_ref, wuk_ref, segt_ref, ones_ref, cache_c, cache_kr,
                   acc_ref, m_ref, l_ref, cpage, krpage, cb_sc, pcat_sc, krb_sc, kr2b_sc, sem_c, sem_kr):
    b = pl.program_id(0)
    nb = pl.num_programs(0)
    n_chunks = pt_ref.shape[1] // DEC_PPS
    page = cpage.shape[2]
    sub_rows = DEC_SUB * page
    n_sub = DEC_PPS // DEC_SUB
    nj = (H_A * QK_NOPE) // LANE

    def page_copies(pid, slot, i):
        return (pltpu.make_async_copy(cache_c.at[pid], cpage.at[slot, i], sem_c.at[slot]),
                pltpu.make_async_copy(cache_kr.at[pid], krpage.at[slot, i], sem_kr.at[slot]))

    def issue(bb, c, slot):
        for i in range(DEC_PPS):
            for cp in page_copies(pt_ref[bb, c * DEC_PPS + i], slot, i):
                cp.start()

    def wait(slot):
        for i in range(DEC_PPS):
            for cp in page_copies(0, slot, i):
                cp.wait()

    @pl.when(b == 0)
    def _():
        cb_sc[1] = jnp.zeros(cb_sc.shape[1:], BF16)
        pcat_sc[1] = jnp.zeros(pcat_sc.shape[1:], BF16)
        krb_sc[1] = jnp.zeros(krb_sc.shape[1:], BF16)
        kr2b_sc[1] = jnp.zeros(kr2b_sc.shape[1:], BF16)
        issue(0, 0, 0)

    qr = (qr_ref[0] * gkr_ref[...]).astype(BF16)
    q_abs = qa_ref[0]
    qa_hi = q_abs.astype(BF16)
    qa_lo = (q_abs - qa_hi.astype(F32)).astype(BF16)
    qa2 = jnp.concatenate([qa_hi, qa_lo], axis=0)

    def keys_partial(slot, sub):
        cb = jnp.concatenate([cpage[slot, sub * DEC_SUB + t].astype(BF16) for t in range(DEC_SUB)], axis=0)
        r0 = sub * sub_rows
        cb_sc[slot, r0:r0 + sub_rows, :] = cb
        kk = _dot(cb, wuk_ref[...])
        p_sq = None
        for jj in range(nj):
            blk = kk[:, jj * LANE:(jj + 1) * LANE]
            sq = blk * blk
            p_sq = sq if p_sq is None else p_sq + sq
        pcat_sc[slot, r0:r0 + sub_rows, :] = p_sq.astype(BF16)

    def rope_keys(slot):
        krt = jnp.concatenate([krpage[slot, i] for i in range(DEC_PPS)], axis=1)
        krb_sc[slot] = krt.astype(BF16)
        kr2b_sc[slot] = (krt * krt).astype(BF16)

    def scores(slot):
        s2 = _dot_nt(qa2, cb_sc[slot])
        ns = _dot_nt(segt_ref[...], pcat_sc[slot])
        ns = ns + _dot(qr, krb_sc[slot]) + _dot(ones_ref[...], kr2b_sc[slot])
        s = s2[0:H_A, :] + s2[H_A:2 * H_A, :] + ns[H_A:2 * H_A, :]
        return s * lax.rsqrt(ns[0:H_A, :] * (1.0 / QK_HEAD) + EPS)

    def softmax_step(s, m_prev, l_prev):
        m_next = jnp.maximum(m_prev, jnp.max(s, axis=-1, keepdims=True))
        alpha = jnp.exp2(m_prev - m_next)
        p = jnp.exp2(s - m_next)
        return p, alpha, m_next, alpha * l_prev + jnp.sum(p, axis=-1, keepdims=True)

    def chunk_step(c, slot, carry):
        m_prev, l_prev, acc_prev = carry
        wait(slot)
        last = c + 1 == n_chunks
        nxt_b = jnp.minimum(jnp.where(last, b + 1, b), nb - 1)
        nxt_c = jnp.where(last, 0, c + 1)
        issue(nxt_b, nxt_c, 1 - slot)

        valid = c > 0
        prev = 1 - slot
        keys_partial(slot, 0)
        s = scores(prev)
        keys_partial(slot, 1)
        p, alpha, m_next, l_next = softmax_step(s, m_prev, l_prev)
        keys_partial(slot, 2)
        acc_next = acc_prev * alpha + _dot(p.astype(BF16), cb_sc[prev])
        for sub in range(3, n_sub):
            keys_partial(slot, sub)
        rope_keys(slot)
        return (jnp.where(valid, m_next, m_prev), jnp.where(valid, l_next, l_prev),
                jnp.where(valid, acc_next, acc_prev))

    def pair_step(i, carry):
        carry = chunk_step(2 * i, 0, carry)
        return chunk_step(2 * i + 1, 1, carry)

    init = (jnp.full((H_A, 1), -jnp.inf, F32), jnp.zeros((H_A, 1), F32), jnp.zeros((H_A, KV_LORA), F32))
    m_run, l_run, acc = lax.fori_loop(0, n_chunks // 2, pair_step, init)
    p, alpha, m_run, l_run = softmax_step(scores(1), m_run, l_run)
    acc_ref[0] = acc * alpha + _dot(p.astype(BF16), cb_sc[1])
    m_ref[0] = jnp.broadcast_to(m_run, (H_A, LANE))
    l_ref[0] = jnp.broadcast_to(l_run, (H_A, LANE))

    @pl.when(b == nb - 1)
    def _():
        wait(0)


def _absorb_kernel(q_ref, gk_ref, wukh_ref, qa_ref):
    gk = gk_ref[...]
    for h in range(H_A):
        qg = q_ref[:, h * HEAD_PAD:h * HEAD_PAD + QK_NOPE] * gk[:, :QK_NOPE]
        hi = qg.astype(BF16)
        lo = (qg - hi.astype(F32)).astype(BF16)
        qa_ref[:, h * KV_LORA:(h + 1) * KV_LORA] = _dot(hi, wukh_ref[h]) + _dot(lo, wukh_ref[h])


def _absorb_queries(q, dw):
    bd = q.shape[0]
    return pl.pallas_call(
        _absorb_kernel,
        in_specs=[_whole()] * 3,
        out_specs=_whole(),
        out_shape=jax.ShapeDtypeStruct((bd, H_A * KV_LORA), F32),
        compiler_params=pltpu.CompilerParams(vmem_limit_bytes=VMEM_LIMIT),
        name="absorb_queries",
    )(q, dw['g_k128'], dw['w_uk_heads'])


def _decode_attention(page_table, q_abs, qr_mat, cache_c, cache_krt, dw):
    bd, n_pages = page_table.shape
    page = cache_c.shape[1]
    rows = DEC_PPS * page
    assert n_pages % (2 * DEC_PPS) == 0 and DEC_PPS // DEC_SUB >= 3
    const = lambda shp: pl.BlockSpec(shp, lambda b, pt: (0,) * len(shp))
    in_specs = [pl.BlockSpec((1, H_A, KV_LORA), lambda b, pt: (b, 0, 0)),
                pl.BlockSpec((1, 2 * H_A, QK_ROPE), lambda b, pt: (b, 0, 0)),
                const((1, QK_ROPE)), const((KV_LORA, H_A * QK_NOPE)),
                const((2 * H_A, LANE)), const((2 * H_A, QK_ROPE)),
                pl.BlockSpec(memory_space=pl.ANY), pl.BlockSpec(memory_space=pl.ANY)]
    out_b = lambda w: pl.BlockSpec((1, H_A, w), lambda b, pt: (b, 0, 0))
    grid_spec = pltpu.PrefetchScalarGridSpec(
        num_scalar_prefetch=1,
        grid=(bd,),
        in_specs=in_specs,
        out_specs=(out_b(KV_LORA), out_b(LANE), out_b(LANE)),
        scratch_shapes=[pltpu.VMEM((2, DEC_PPS, page, KV_LORA), F32),
                        pltpu.VMEM((2, DEC_PPS, QK_ROPE, page), F32),
                        pltpu.VMEM((2, rows, KV_LORA), BF16),
                        pltpu.VMEM((2, rows, LANE), BF16),
                        pltpu.VMEM((2, QK_ROPE, rows), BF16),
                        pltpu.VMEM((2, QK_ROPE, rows), BF16),
                        pltpu.SemaphoreType.DMA((2,)), pltpu.SemaphoreType.DMA((2,))],
    )
    return pl.pallas_call(
        _decode_kernel,
        grid_spec=grid_spec,
        out_shape=(jax.ShapeDtypeStruct((bd, H_A, KV_LORA), F32),
                   jax.ShapeDtypeStruct((bd, H_A, LANE), F32),
                   jax.ShapeDtypeStruct((bd, H_A, LANE), F32)),
        compiler_params=_cparams(("arbitrary",)),
        name="decode_attn",
    )(page_table, q_abs, qr_mat, dw['gk_rope'], dw['w_uk_perm'], dw['seg_t'], dw['ones_rows'], cache_c, cache_krt)


def _decode_final_kernel(q_ref, k_ref, c_ref, m_ref, l_ref, acc_ref, wuv_ref, o_ref):
    c_new = c_ref[...]
    for pair in range(H_A // 2):
        o_pair = None
        for h in (2 * pair, 2 * pair + 1):
            lo, hi = h * HEAD_PAD, (h + 1) * HEAD_PAD
            s_new = jnp.sum(q_ref[:, lo:hi] * k_ref[:, lo:hi].astype(F32), axis=-1, keepdims=True)
            m_old = m_ref[:, h:h + 1]
            m_new = jnp.maximum(m_old, s_new)
            a = jnp.exp2(m_old - m_new)
            pn = jnp.exp2(s_new - m_new)
            l_new = l_ref[:, h:h + 1] * a + pn
            ctx = (acc_ref[:, h * KV_LORA:(h + 1) * KV_LORA] * a + pn * c_new) / l_new
            part = _dot(ctx.astype(BF16), wuv_ref[h])
            o_pair = part if o_pair is None else o_pair + part
        o_ref[:, pair * LANE:(pair + 1) * LANE] = o_pair.astype(o_ref.dtype)


def _decode_final(q, k_new, c_new, m, l, acc, dw):
    bd = q.shape[0]
    return pl.pallas_call(
        _decode_final_kernel,
        in_specs=[_whole()] * 7,
        out_specs=_whole(),
        out_shape=jax.ShapeDtypeStruct((bd, H_A * V_HEAD), BF16),
        compiler_params=pltpu.CompilerParams(vmem_limit_bytes=VMEM_LIMIT),
        name="decode_final",
    )(q, k_new, c_new, m, l, acc, dw['w_uv_pair'])


def _merge_kernel(x_ref, attn_ref, ssm_ref, gates_ref, wa_ref, wb_ref, wo_ref, gffn_ref, wr_ref, br_ref,
                  h_ref, hn_ref, logit_ref):
    g = jax.nn.sigmoid(gates_ref[...])
    a = _dot(attn_ref[...], wa_ref[...])
    b = _dot(ssm_ref[...].astype(BF16), wb_ref[...])
    mixed = g[:, :D_MODEL] * a + g[:, D_MODEL:] * b
    h = x_ref[...] + _dot(mixed.astype(BF16), wo_ref[...])
    h_ref[...] = h
    hn = _rms(h, gffn_ref[...])
    hn_ref[...] = hn
    hi, mid, _ = _split3(hn)
    w_hi = wr_ref[0]
    w_lo = wr_ref[1]
    logit_ref[...] = _dot(hi, w_hi) + _dot(mid, w_hi) + _dot(hi, w_lo) + br_ref[...]


def _merge(x, attn, ssm, gates, mw, *, tm):
    t = x.shape[0]
    row = lambda w: pl.BlockSpec((tm, w), lambda i: (i, 0))
    consts = (mw['w_a_out'], mw['w_b_out'], mw['w_out'], mw['g_ffn'], mw['w_router2'], mw['b_router128'])
    return pl.pallas_call(
        _merge_kernel,
        grid=(t // tm,),
        in_specs=[row(D_MODEL), row(H_A * V_HEAD), row(D_INNER), row(2 * D_MODEL)] + [_whole()] * len(consts),
        out_specs=(row(D_MODEL), row(D_MODEL), row(LANE)),
        out_shape=(jax.ShapeDtypeStruct((t, D_MODEL), F32), jax.ShapeDtypeStruct((t, D_MODEL), F32),
                   jax.ShapeDtypeStruct((t, LANE), F32)),
        compiler_params=_cparams(("parallel",)),
        name="merge",
    )(x, attn, ssm, gates, *consts)


def _moe_kernel(be_ref, cnt_ref, seg_ref, tok_ref, asg_ref, run_ref, nxt_ref, x_hbm, wgu_hbm, bgu_ref, wd_hbm, bd_ref,
                y_hbm, xbuf, obuf, wgu_f, wd_f, wgu_b, wd_b, sem_in, sem_out, sem_w):
    i = pl.program_id(0)
    n = pl.num_programs(0)
    bm = xbuf.shape[1]
    dump0 = y_hbm.shape[0] - bm

    def weight_copies(expert, wslot):
        return (pltpu.make_async_copy(wgu_hbm.at[expert], wgu_f.at[wslot], sem_w.at[wslot]),
                pltpu.make_async_copy(wd_hbm.at[expert], wd_f.at[wslot], sem_w.at[wslot]))

    def in_copy(tok, slot, r):
        return pltpu.make_async_copy(x_hbm.at[pl.ds(tok, 1), :], xbuf.at[slot, pl.ds(r, 1), :], sem_in.at[slot])

    def out_copy(row, slot, r):
        return pltpu.make_async_copy(obuf.at[slot, pl.ds(r, 1), :], y_hbm.at[pl.ds(row, 1), :], sem_out.at[slot])

    def gather_loop(blk, slot):
        base = seg_ref[blk]

        def body(r, carry):
            in_copy(tok_ref[base + r], slot, r).start()
            return carry
        lax.fori_loop(0, bm, body, 0, unroll=8)

    def gather_inline(blk, slot):
        base = seg_ref[blk]
        for r in range(bm):
            in_copy(tok_ref[base + r], slot, r).start()

    def gather_wait(slot):
        pltpu.make_async_copy(xbuf.at[slot], xbuf.at[slot], sem_in.at[slot]).wait()

    def scattered(blk):
        return (blk < 0) | (cnt_ref[jnp.maximum(blk, 0)] > 0)

    def scatter_rows(blk):
        j = jnp.maximum(blk, 0)
        return seg_ref[j], jnp.where(blk >= 0, cnt_ref[j], 0)

    def scatter_loop(blk, slot):
        base, count = scatter_rows(blk)

        def body(r, carry):
            out_copy(jnp.where(r < count, asg_ref[base + r], dump0 + r), slot, r).start(priority=1)
            return carry
        lax.fori_loop(0, bm, body, 0, unroll=8)

    def scatter_inline(blk, slot):
        base, count = scatter_rows(blk)
        for r in range(bm):
            out_copy(jnp.where(count > r, asg_ref[base + r], dump0 + r), slot, r).start(priority=1)

    def scatter_wait(slot):
        pltpu.make_async_copy(obuf.at[slot], obuf.at[slot], sem_out.at[slot]).wait()

    valid = cnt_ref[i] > 0
    nxt_valid = (i + 1 < n) & (cnt_ref[jnp.minimum(i + 1, n - 1)] > 0)

    @pl.when(i == 0)
    def _():
        obuf[1] = jnp.zeros(obuf.shape[1:], F32)

    @pl.when((i == 0) & valid)
    def _():
        gather_loop(0, 0)

    @pl.when((i >= 1) & scattered(i - 2))
    def _():
        scatter_wait(i % 2)

    @pl.when((i == 0) & (run_ref[0] >= 0))
    def _():
        for cp in weight_copies(be_ref[0], 0):
            cp.start()

    for wslot in range(2):
        @pl.when((run_ref[i] >= 0) & (run_ref[i] % 2 == wslot))
        def _():
            for cp in weight_copies(0, wslot):
                cp.wait()
            wgu_b[...] = wgu_f[wslot].astype(BF16)
            wd_b[...] = wd_f[wslot].astype(BF16)

            @pl.when(nxt_ref[i] >= 0)
            def _():
                for cp in weight_copies(nxt_ref[i], 1 - wslot):
                    cp.start()

    def expert_block(prefetch_next, slot):
        gather_wait(slot)
        x = xbuf[slot].astype(BF16)
        scatter_inline(i - 1, 1 - slot)
        if prefetch_next:
            gather_inline(i + 1, 1 - slot)
        gu = _dot(x, wgu_b[...]) + bgu_ref[0]
        gate = jnp.minimum(gu[:, :D_FF], SWIGLU_LIMIT)
        up = jnp.clip(gu[:, D_FF:], -SWIGLU_LIMIT, SWIGLU_LIMIT)
        act = (up + 1.0) * gate * jax.nn.sigmoid(SWIGLU_ALPHA * gate)
        obuf[slot] = _dot(act.astype(BF16), wd_b[...]) + bd_ref[0]

    for parity in range(2):
        on_parity = valid & (i % 2 == parity)

        @pl.when(on_parity & nxt_valid)
        def _():
            expert_block(True, parity)

        @pl.when(on_parity & jnp.logical_not(nxt_valid))
        def _():
            expert_block(False, parity)

    @pl.when(jnp.logical_not(valid) & scattered(i - 1))
    def _():
        scatter_loop(i - 1, (i + 1) % 2)

    @pl.when((i == n - 1) & scattered(i - 1))
    def _():
        scatter_wait((i + 1) % 2)


def _moe_experts(block_e, block_cnt, seg_start, tok_sorted, asg_sorted, run_open, run_next, x, w_gate_up, b_gate_up,
                 w_down, b_down):
    n_blocks = block_e.shape[0]
    bm = MOE_BM
    n_assign = x.shape[0] * TOP_K
    b_spec = lambda shp: pl.BlockSpec(shp, lambda i, be, *_: (be[i], 0, 0))
    hbm = pl.BlockSpec(memory_space=pl.ANY)
    grid_spec = pltpu.PrefetchScalarGridSpec(
        num_scalar_prefetch=7,
        grid=(n_blocks,),
        in_specs=[hbm, hbm, b_spec((1, 1, 2 * D_FF)), hbm, b_spec((1, 1, D_MODEL))],
        out_specs=hbm,
        scratch_shapes=[pltpu.VMEM((2, bm, D_MODEL), F32), pltpu.VMEM((2, bm, D_MODEL), F32),
                        pltpu.VMEM((2, D_MODEL, 2 * D_FF), F32), pltpu.VMEM((2, D_FF, D_MODEL), F32),
                        pltpu.VMEM((D_MODEL, 2 * D_FF), BF16), pltpu.VMEM((D_FF, D_MODEL), BF16),
                        pltpu.SemaphoreType.DMA((2,)), pltpu.SemaphoreType.DMA((2,)),
                        pltpu.SemaphoreType.DMA((2,))],
    )
    return pl.pallas_call(
        _moe_kernel,
        grid_spec=grid_spec,
        out_shape=jax.ShapeDtypeStruct((n_assign + bm, D_MODEL), F32),
        compiler_params=_cparams(("arbitrary",)),
        name="moe_experts",
    )(block_e, block_cnt, seg_start, tok_sorted, asg_sorted, run_open, run_next, x, w_gate_up,
      b_gate_up.reshape(N_EXPERTS, 1, 2 * D_FF), w_down, b_down.reshape(N_EXPERTS, 1, D_MODEL))


def _ple_kernel(y_ref, hp_ref, hs_ref, gw_ref, pp_ref, ps_ref, gin_ref, wgate_ref, wple_ref, gple_ref,
                op_ref, os_ref):
    i = pl.program_id(0)
    n = pl.num_programs(0)
    tm = hp_ref.shape[0]
    is_sample = i == n - 1
    gw = gw_ref[...]
    y = y_ref[0:tm, :] * gw[:, 0:1]
    for k in range(1, TOP_K):
        y = y + y_ref[k * tm:(k + 1) * tm, :] * gw[:, k:k + 1]
    h2 = jnp.where(is_sample, hs_ref[...], hp_ref[...]) + y
    p_emb = jnp.where(is_sample, ps_ref[...], pp_ref[...])
    gate = jax.nn.sigmoid(_dot(_rms(h2, gin_ref[...]).astype(BF16), wgate_ref[...]))
    ple = _rms(_dot(p_emb.astype(BF16), wple_ref[...]), gple_ref[...]) * gate
    out = h2 + ple

    @pl.when(i < n - 1)
    def _():
        op_ref[...] = out

    @pl.when(i == n - 1)
    def _():
        os_ref[...] = out


def _combine_ple(y_rows, h_p, h_s, gate_w, p_p, p_s, pw):
    n_prompt = h_p.shape[0]
    tm = PLE_TM
    assert h_s.shape[0] == tm and n_prompt % tm == 0
    n_tiles = n_prompt // tm + 1
    prompt_row = lambda w: pl.BlockSpec((tm, w), lambda i: (jnp.minimum(i, n_tiles - 2), 0))
    sample_row = lambda w: pl.BlockSpec((tm, w), lambda i: (0, 0))
    return pl.pallas_call(
        _ple_kernel,
        grid=(n_tiles,),
        in_specs=[pl.BlockSpec((tm * TOP_K, D_MODEL), lambda i: (i, 0)), prompt_row(D_MODEL), sample_row(D_MODEL),
                  pl.BlockSpec((tm, LANE), lambda i: (i, 0)), prompt_row(PLE_DIM), sample_row(PLE_DIM)]
                 + [_whole()] * 4,
        out_specs=(prompt_row(D_MODEL), sample_row(D_MODEL)),
        out_shape=(jax.ShapeDtypeStruct((n_prompt, D_MODEL), F32), jax.ShapeDtypeStruct((tm, D_MODEL), F32)),
        compiler_params=_cparams(("arbitrary",)),
        name="combine_ple",
    )(y_rows, h_p, h_s, gate_w, p_p, p_s, pw['g_ple_in'], pw['w_ple_gate'], pw['w_ple'], pw['g_ple'])


def _pad_lanes(x, n):
    return jnp.pad(x, [(0, 0)] * (x.ndim - 1) + [(0, n - x.shape[-1])])


def _rot_cols(w):
    half = QK_ROPE // 2
    return jnp.concatenate([-w[..., half:], w[..., :half]], axis=-1)


def _prep_weights(g_mix_norm, w_in, g_q_a, w_q_b, g_kv_a, w_uk, w_uv, g_q_head, g_k_head, w_a_out, w_conv,
                  b_conv, dt_bias, a_log, d_skip, g_ssm_norm, w_b_out, w_out, g_ffn_norm, w_router, b_router,
                  g_ple_in, w_ple_gate, w_ple, g_ple):
    sizes = (Q_LORA, KV_LORA, QK_ROPE, D_INNER, CONV_DIM, H_B, 2 * D_MODEL)
    offs = np.concatenate([[0], np.cumsum(sizes)])
    wq, wkv, wkr, wz, wxbc, wdt, wg = [w_in[:, int(offs[i]):int(offs[i + 1])] for i in range(7)]
    zc = lambda n: jnp.zeros((D_MODEL, n), F32)
    kr128 = jnp.concatenate([zc(QK_NOPE), wkr, zc(LANE - QK_HEAD)], axis=1)
    krrot = jnp.concatenate([zc(QK_NOPE), _rot_cols(wkr), zc(LANE - QK_HEAD)], axis=1)
    w_lat = jnp.concatenate([wq, wkv, kr128, krrot, _pad_lanes(wdt, LANE)], axis=1).astype(BF16)

    wqb = w_q_b.reshape(Q_LORA, H_A, QK_HEAD)
    nope, rope_w = wqb[..., :QK_NOPE], wqb[..., QK_NOPE:]
    z_nope = jnp.zeros_like(nope)
    q128 = _pad_lanes(jnp.concatenate([nope, rope_w], axis=-1), LANE).reshape(Q_LORA, H_A * LANE)
    qrot = _pad_lanes(jnp.concatenate([z_nope, _rot_cols(rope_w)], axis=-1), LANE).reshape(Q_LORA, H_A * LANE)
    w_qb = jnp.concatenate([q128, qrot], axis=1).astype(BF16)

    wuk3 = w_uk.reshape(KV_LORA, H_A, QK_NOPE)
    w_uk_pad = _pad_lanes(wuk3, LANE).reshape(KV_LORA, H_A * LANE).astype(BF16)
    sub = LANE // H_A
    w_uk_perm = wuk3.reshape(KV_LORA, H_A, QK_NOPE // sub, sub).transpose(0, 2, 1, 3)
    w_uk_perm = w_uk_perm.reshape(KV_LORA, H_A * QK_NOPE).astype(BF16)

    wuv3 = w_uv.reshape(KV_LORA, H_A, V_HEAD)
    even = (jnp.arange(H_A) % 2 == 0)[None, :, None]
    zv = jnp.zeros_like(wuv3)
    w_uv_ext = jnp.where(even, jnp.concatenate([wuv3, zv], axis=-1), jnp.concatenate([zv, wuv3], axis=-1))
    ones_lane = np.zeros((H_A, LANE), np.float32)
    ones_lane[0::2, V_HEAD] = 1.0
    ones_lane[1::2, 0] = 1.0
    front = dict(
        g_mix=g_mix_norm.reshape(1, -1), w_lat=w_lat, w_z=wz.astype(BF16), w_xbc=wxbc.astype(BF16),
        w_g=wg.astype(BF16), g_q_a=g_q_a.reshape(1, -1), w_qb=w_qb, g_kv_a=g_kv_a.reshape(1, -1),
        w_uk_pad=w_uk_pad, w_uv_ext=w_uv_ext.reshape(KV_LORA, H_A * LANE).astype(BF16),
        v_ones=jnp.asarray(ones_lane.reshape(1, H_A * LANE)),
        g_q128=_pad_lanes(g_q_head.reshape(1, -1), LANE) * Q_SCALE,
        g_k128=_pad_lanes(g_k_head.reshape(1, -1), LANE),
    )

    lane_head = np.arange(LANE) // sub
    seg = (lane_head[None, :] == np.arange(H_A)[:, None]).astype(np.float32)
    seg_t = np.zeros((2 * H_A, LANE), np.float32)
    seg_t[:H_A] = seg
    ones_rows = np.zeros((2 * H_A, QK_ROPE), np.float32)
    ones_rows[:H_A] = 1.0
    w_uv_pair = w_uv_ext
    decode = dict(
        g_k128=front['g_k128'], gk_rope=g_k_head[QK_NOPE:].reshape(1, QK_ROPE), w_uk_perm=w_uk_perm,
        w_uk_heads=wuk3.transpose(1, 2, 0).astype(BF16),
        seg_t=jnp.asarray(seg_t, BF16), ones_rows=jnp.asarray(ones_rows, BF16),
        w_uv_pair=w_uv_pair.transpose(1, 0, 2).astype(BF16),
    )

    expand = (np.arange(D_INNER)[None, :] // SSM_HEAD == np.arange(LANE)[:, None]).astype(np.float32)
    tri = (np.arange(SSD_CHUNK)[None, :] <= np.arange(SSD_CHUNK)[:, None]).astype(np.float32)
    ssm = dict(
        w_conv=w_conv, b_conv=b_conv.reshape(1, -1), dt_bias128=_pad_lanes(dt_bias.reshape(1, -1), LANE),
        a_head128=_pad_lanes(-jnp.exp(a_log).reshape(1, -1), LANE),
        d_skip_x=jnp.repeat(d_skip, SSM_HEAD).reshape(1, -1), g_norm=g_ssm_norm.reshape(1, -1),
        tri=jnp.asarray(tri, BF16), expand=jnp.asarray(expand, BF16),
    )

    wr = _pad_lanes(w_router, LANE)
    wr_hi = wr.astype(BF16)
    wr_lo = (wr - wr_hi.astype(F32)).astype(BF16)
    merge = dict(
        w_a_out=w_a_out.astype(BF16), w_b_out=w_b_out.astype(BF16), w_out=w_out.astype(BF16),
        g_ffn=g_ffn_norm.reshape(1, -1), w_router2=jnp.stack([wr_hi, wr_lo]),
        b_router128=_pad_lanes(b_router.reshape(1, -1), LANE),
    )
    ple = dict(g_ple_in=g_ple_in.reshape(1, -1), w_ple_gate=w_ple_gate.astype(BF16), w_ple=w_ple.astype(BF16),
               g_ple=g_ple.reshape(1, -1))
    return front, decode, ssm, merge, ple


def _rope_tables(pos):
    half = QK_ROPE // 2
    inv = ROPE_THETA ** (-jnp.arange(half, dtype=F32) / half)
    ang = pos.astype(F32)[:, None] * inv[None, :]
    cos, sin = jnp.cos(ang), jnp.sin(ang)
    n = pos.shape[0]
    cos128 = jnp.concatenate([jnp.ones((n, QK_NOPE), F32), cos, cos, jnp.ones((n, LANE - QK_HEAD), F32)], axis=1)
    sin128 = jnp.concatenate([jnp.zeros((n, QK_NOPE), F32), sin, sin, jnp.zeros((n, LANE - QK_HEAD), F32)], axis=1)
    return cos128, sin128


def _route(logits, n_tok):
    top_val, top_idx = lax.top_k(logits[:, :N_EXPERTS], TOP_K)
    gate_w = jax.nn.softmax(top_val, axis=-1)
    n_assign = n_tok * TOP_K
    flat_e = top_idx.reshape(-1).astype(jnp.int32)
    counts = jnp.sum((flat_e[:, None] == jnp.arange(N_EXPERTS, dtype=jnp.int32)[None, :]).astype(jnp.int32), axis=0)
    bm = MOE_BM
    padded = (counts + bm - 1) // bm * bm
    pend = jnp.cumsum(padded)
    pstarts = pend - padded
    starts = jnp.cumsum(counts) - counts
    idx_bits = max(1, (n_assign - 1).bit_length())
    key = jnp.sort(flat_e * (1 << idx_bits) + jnp.arange(n_assign, dtype=jnp.int32))
    asg = jnp.pad(key & ((1 << idx_bits) - 1), (0, bm)).astype(jnp.int32)
    k_bits, tm_bits = TOP_K.bit_length() - 1, PLE_TM.bit_length() - 1
    assert TOP_K == 1 << k_bits and PLE_TM == 1 << tm_bits
    tok_sorted = asg >> k_bits
    asg_sorted = (((asg >> (k_bits + tm_bits)) << (k_bits + tm_bits)) + ((asg & (TOP_K - 1)) << tm_bits)
                  + (tok_sorted & (PLE_TM - 1)))
    n_blocks = -(-n_assign // bm) + N_EXPERTS
    blk_start = jnp.arange(n_blocks, dtype=jnp.int32) * bm
    e_ids = jnp.arange(N_EXPERTS, dtype=jnp.int32)
    block_e = jnp.minimum(jnp.sum((pend[None, :] <= blk_start[:, None]).astype(jnp.int32), axis=1), N_EXPERTS - 1)
    block_e = block_e.astype(jnp.int32)
    is_e = (block_e[:, None] == e_ids[None, :]).astype(jnp.int32)
    of_block = lambda per_expert: jnp.sum(is_e * per_expert[None, :], axis=1)
    offset = blk_start - of_block(pstarts)
    block_cnt = jnp.where(blk_start < pend[-1], jnp.clip(of_block(counts) - offset, 0, bm), 0).astype(jnp.int32)
    seg_start = jnp.clip(of_block(starts) + offset, 0, n_assign).astype(jnp.int32)
    opens = (block_cnt > 0) & (offset == 0)
    run_open = jnp.where(opens, jnp.cumsum(opens.astype(jnp.int32)) - 1, -1).astype(jnp.int32)
    later = (e_ids[None, :] > e_ids[:, None]) & (counts[None, :] > 0)
    next_expert = of_block(jnp.min(jnp.where(later, e_ids[None, :], N_EXPERTS), axis=1))
    run_next = jnp.where(opens & (next_expert < N_EXPERTS), next_expert, -1).astype(jnp.int32)
    return gate_w, tok_sorted, asg_sorted, block_e, block_cnt, seg_start, run_open, run_next


def kernel(x_prompt, x_sample, p_prompt, p_sample, cache_kv_latent, cache_k_rope, page_table, state_conv, state_ssm, g_mix_norm, w_in, g_q_a, w_q_b, g_kv_a, w_uk, w_uv, g_q_head, g_k_head, w_a_out, w_conv, b_conv, dt_bias, a_log, d_skip, g_ssm_norm, w_b_out, w_out, g_ffn_norm, w_router, b_router, w_gate_up, b_gate_up, w_down, b_down, g_ple_in, w_ple_gate, w_ple, g_ple):
    depth = g_mix_norm.shape[0]
    assert depth == 1, "one layer"
    b, s, _ = x_prompt.shape
    bd, sd, _ = x_sample.shape
    assert sd == 1, "one new token per sample sequence"
    n_pages = page_table.shape[1]
    page_size = cache_kv_latent.shape[2]
    assert n_pages % DEC_PPS == 0 and s % SSD_CHUNK == 0
    tp = b * s

    fw, dw, sw, mw, pw = _prep_weights(
        g_mix_norm[0], w_in[0], g_q_a[0], w_q_b[0], g_kv_a[0], w_uk[0], w_uv[0], g_q_head[0], g_k_head[0],
        w_a_out[0], w_conv[0], b_conv[0], dt_bias[0], a_log[0], d_skip[0], g_ssm_norm[0], w_b_out[0], w_out[0],
        g_ffn_norm[0], w_router[0], b_router[0], g_ple_in[0], w_ple_gate[0], w_ple[0], g_ple[0])

    xp = x_prompt.reshape(tp, D_MODEL)
    cos_p, sin_p = _rope_tables(jnp.arange(s))
    tm_p = min(FRONT_TM, s)
    bound = QK_HEAD * jnp.max(jnp.abs(fw['g_q128'])) * jnp.max(jnp.abs(fw['g_k128']))
    bound = (1.02 * bound + 1.0).astype(BF16).astype(F32)
    bias_lane = (jnp.arange(LANE) == QK_HEAD).astype(F32).reshape(1, LANE)
    q_p, k_p, v_p, c_p, kr_p, z_p, xbc_p, dt_p, gates_p = _front(
        xp, cos_p, sin_p, fw, -bound * bias_lane, bias_lane, tm=tm_p, q_dtype=BF16, pos_blocks=s // tm_p)
    qkv = (q_p.reshape(b, s, -1), k_p.reshape(b, s, -1), v_p.reshape(b, s, -1))
    attn_p = lax.cond(bound <= ATTN_MAX_BOUND,
                      lambda q, k, v: _prompt_attention(q, k, v, b, s, bounded=True),
                      lambda q, k, v: _prompt_attention(q, k, v, b, s, bounded=False), *qkv)
    ssm_p, hfin_p = _ssd_prompt(xbc_p, z_p, dt_p, sw, b, s)
    h_p, hn_p, logit_p = _merge(xp, attn_p.reshape(tp, -1), ssm_p, gates_p, mw, tm=min(MERGE_TM, tp))

    xs = x_sample.reshape(bd, D_MODEL)
    cos_s, sin_s = _rope_tables(jnp.full((bd,), n_pages * page_size, jnp.int32))
    no_pad = jnp.zeros((1, LANE), F32)
    q_s, k_s, _, c_s, kr_s, z_s, xbc_s, dt_s, gates_s = _front(
        xs, cos_s, sin_s, fw, no_pad, no_pad, tm=bd, q_dtype=F32, pos_blocks=1)
    q3 = q_s.reshape(bd, H_A, LANE)
    q_abs = _absorb_queries(q_s, dw).reshape(bd, H_A, KV_LORA)
    qr = q3[:, :, QK_NOPE:QK_HEAD]
    qr_mat = jnp.concatenate([jnp.zeros_like(qr), qr], axis=1)
    acc, m_run, l_run = _decode_attention(page_table, q_abs, qr_mat, cache_kv_latent[0],
                                          jnp.swapaxes(cache_k_rope[0], 1, 2), dw)
    attn_s = _decode_final(q_s, k_s, c_s, m_run[:, :, 0], l_run[:, :, 0], acc.reshape(bd, -1), dw)
    ssm_s, hnew_s = _ssm_sample(xbc_s, state_conv[0], z_s, dt_s,
                                state_ssm[0].reshape(bd, D_INNER, D_STATE), sw)
    h_s, hn_s, logit_s = _merge(xs, attn_s, ssm_s, gates_s, mw, tm=bd)

    hn_all = jnp.concatenate([hn_p, hn_s], axis=0)
    logits = jnp.concatenate([logit_p, logit_s], axis=0)
    n_tok = tp + bd
    gate_w, tok_sorted, asg_sorted, block_e, block_cnt, seg_start, run_open, run_next = _route(logits, n_tok)
    y_rows = _moe_experts(block_e, block_cnt, seg_start, tok_sorted, asg_sorted, run_open, run_next, hn_all,
                          w_gate_up[0], b_gate_up[0], w_down[0], b_down[0])
    out_p, out_s = _combine_ple(y_rows, h_p, h_s, _pad_lanes(gate_w, LANE), p_prompt[0].reshape(tp, PLE_DIM),
                                p_sample[0].reshape(bd, PLE_DIM), pw)

    y_prompt = out_p.reshape(b, s, D_MODEL)
    y_sample = out_s.reshape(bd, sd, D_MODEL)
    new_c_p = c_p.reshape(1, b, s, KV_LORA)
    new_kr_p = kr_p[:, QK_NOPE:QK_HEAD].reshape(1, b, s, QK_ROPE)
    conv_p = xbc_p.reshape(b, s, CONV_DIM)[:, s - (CONV_W - 1):].reshape(1, b, CONV_W - 1, CONV_DIM)
    ssm_state_p = hfin_p.reshape(1, b, H_B, SSM_HEAD, D_STATE)
    new_c_s = c_s.reshape(1, bd, sd, KV_LORA)
    new_kr_s = kr_s[:, QK_NOPE:QK_HEAD].reshape(1, bd, sd, QK_ROPE)
    conv_s = jnp.concatenate([state_conv[0][:, 1:], xbc_s[:, None, :]], axis=1).reshape(1, bd, CONV_W - 1, CONV_DIM)
    ssm_state_s = hnew_s.reshape(1, bd, H_B, SSM_HEAD, D_STATE)
    return (y_prompt, y_sample, new_c_p, new_kr_p, conv_p, ssm_state_p, new_c_s, new_kr_s, conv_s, ssm_state_s)
```

```python
import functools
import math

import jax
import jax.numpy as jnp
import numpy as np
from jax import lax
from jax.experimental import pallas as pl
from jax.experimental.pallas import tpu as pltpu

F32 = jnp.float32
BF16 = jnp.bfloat16

D_MODEL = 1024
H_A = 16
Q_LORA = 384
KV_LORA = 256
QK_NOPE = 64
QK_ROPE = 32
QK_HEAD = QK_NOPE + QK_ROPE
V_HEAD = 64
ROPE_THETA = 10000.0
D_INNER = 2 * D_MODEL
SSM_HEAD = 64
H_B = D_INNER // SSM_HEAD
N_GROUPS = 4
D_STATE = 128
CONV_W = 4
CONV_DIM = D_INNER + 2 * N_GROUPS * D_STATE
SSD_CHUNK = 128
N_EXPERTS = 32
TOP_K = 4
D_FF = D_MODEL
SWIGLU_LIMIT = 7.0
SWIGLU_ALPHA = 1.702
PLE_DIM = 256
EPS = 1e-6

LANE = 128
HEAD_PAD = LANE
GROUP_W = D_INNER // N_GROUPS
HEADS_PER_GROUP = H_B // N_GROUPS
Q_SCALE = QK_HEAD ** -0.5 * math.log2(math.e)
VMEM_LIMIT = 56 * 1024 * 1024

FRONT_TM = 256
ATTN_TQ = 1024
ATTN_MAX_BOUND = 40.0
MERGE_TM = 256
MOE_BM = 256
PLE_TM = 128
DEC_PPS = 16
DEC_SUB = 4


def _dot(a, b):
    return jnp.dot(a, b, preferred_element_type=F32)


def _dot_nt(a, b):
    return lax.dot_general(a, b, (((1,), (1,)), ((), ())), preferred_element_type=F32)


def _dot_tn(a, b):
    return lax.dot_general(a, b, (((0,), (0,)), ((), ())), preferred_element_type=F32)


def _split3(x):
    hi = x.astype(BF16)
    r1 = x - hi.astype(F32)
    mid = r1.astype(BF16)
    lo = (r1 - mid.astype(F32)).astype(BF16)
    return hi, mid, lo


def _dot_f32_lhs(x, e, terms=3):
    parts = _split3(x)[:terms]
    out = _dot(parts[0], e)
    for part in parts[1:]:
        out = out + _dot(part, e)
    return out


def _dot_f32_rhs(e, x):
    hi, mid, lo = _split3(x)
    return _dot(e, hi) + _dot(e, mid) + _dot(e, lo)


def _rms(x, g):
    return x * lax.rsqrt(jnp.mean(x * x, axis=-1, keepdims=True) + EPS) * g


def _silu(x):
    return x * jax.nn.sigmoid(x)


def _cparams(sem, vmem=VMEM_LIMIT):
    return pltpu.CompilerParams(dimension_semantics=sem, vmem_limit_bytes=vmem)


def _whole():
    return pl.BlockSpec(memory_space=pltpu.VMEM)


def _head_norm(xh, g):
    ss = jnp.sum(xh * xh, axis=-1, keepdims=True)
    return xh * lax.rsqrt(ss * (1.0 / QK_HEAD) + EPS) * g


def _front_kernel(x_ref, cos_ref, sin_ref, gmix_ref, wlat_ref, wz_ref, wxbc_ref, wg_ref,
                  gqa_ref, wqb_ref, gkva_ref, wuk_ref, wuv_ref, gq_ref, gk_ref, qpad_ref, kpad_ref, vpad_ref,
                  q_ref, k_ref, v_ref, c_ref, kr_ref, z_ref, xbc_ref, dt_ref, gates_ref):
    x = x_ref[...]
    ub = _rms(x, gmix_ref[...]).astype(BF16)
    z_ref[...] = _dot(ub, wz_ref[...])
    xbc_ref[...] = _dot(ub, wxbc_ref[...])
    gates_ref[...] = _dot(ub, wg_ref[...])
    lat = _dot(ub, wlat_ref[...])
    q_lat = lat[:, :Q_LORA]
    kv_lat = lat[:, Q_LORA:Q_LORA + KV_LORA]
    o = Q_LORA + KV_LORA
    kr_raw = lat[:, o:o + LANE]
    kr_rot = lat[:, o + LANE:o + 2 * LANE]
    dt_ref[...] = lat[:, o + 2 * LANE:o + 3 * LANE]
    cos = cos_ref[...]
    sin = sin_ref[...]
    c = _rms(kv_lat, gkva_ref[...])
    c_ref[...] = c
    kr = kr_raw * cos + kr_rot * sin
    kr_ref[...] = kr
    qn = _rms(q_lat, gqa_ref[...]).astype(BF16)
    q2 = _dot(qn, wqb_ref[...])
    nq = H_A * HEAD_PAD
    gq = gq_ref[...]
    qpad = qpad_ref[...]
    kpad = kpad_ref[...]
    for h in range(H_A):
        lo, hi = h * HEAD_PAD, (h + 1) * HEAD_PAD
        qh = q2[:, lo:hi] * cos + q2[:, nq + lo:nq + hi] * sin
        q_ref[:, lo:hi] = (_head_norm(qh, gq) + qpad).astype(q_ref.dtype)
    cb = c.astype(BF16)
    kn = _dot(cb, wuk_ref[...])
    gk = gk_ref[...]
    for h in range(H_A):
        lo, hi = h * HEAD_PAD, (h + 1) * HEAD_PAD
        k_ref[:, lo:hi] = (_head_norm(kn[:, lo:hi] + kr, gk) + kpad).astype(k_ref.dtype)
    v_ref[...] = (_dot(cb, wuv_ref[...]) + vpad_ref[...]).astype(v_ref.dtype)


def _front(x, cos, sin, fw, qpad, kpad, *, tm, q_dtype, pos_blocks):
    t = x.shape[0]
    nt = t // tm
    row = lambda w: pl.BlockSpec((tm, w), lambda i: (i, 0))
    pos_spec = pl.BlockSpec((tm, LANE), lambda i: (i % pos_blocks, 0))
    nq = H_A * HEAD_PAD
    out_shape = (
        jax.ShapeDtypeStruct((t, nq), q_dtype),
        jax.ShapeDtypeStruct((t, nq), BF16),
        jax.ShapeDtypeStruct((t, nq), BF16),
        jax.ShapeDtypeStruct((t, KV_LORA), F32),
        jax.ShapeDtypeStruct((t, LANE), F32),
        jax.ShapeDtypeStruct((t, D_INNER), F32),
        jax.ShapeDtypeStruct((t, CONV_DIM), F32),
        jax.ShapeDtypeStruct((t, LANE), F32),
        jax.ShapeDtypeStruct((t, 2 * D_MODEL), F32),
    )
    out_specs = (row(nq), row(nq), row(nq), row(KV_LORA), row(LANE), row(D_INNER),
                 row(CONV_DIM), row(LANE), row(2 * D_MODEL))
    weights = (fw['g_mix'], fw['w_lat'], fw['w_z'], fw['w_xbc'], fw['w_g'], fw['g_q_a'], fw['w_qb'],
               fw['g_kv_a'], fw['w_uk_pad'], fw['w_uv_ext'], fw['g_q128'], fw['g_k128'], qpad, kpad, fw['v_ones'])
    return pl.pallas_call(
        _front_kernel,
        grid=(nt,),
        in_specs=[row(D_MODEL), pos_spec, pos_spec] + [_whole()] * len(weights),
        out_specs=out_specs,
        out_shape=out_shape,
        compiler_params=_cparams(("parallel",)),
        name="front",
    )(x, cos, sin, *weights)


def _attn_kernel(qi_ref, ki_ref, q_ref, k_ref, v_ref, o_ref, acc_sc, *m_scratch, bounded):
    step = pl.program_id(2)
    qi = qi_ref[step]
    ki = ki_ref[step]
    tq = q_ref.shape[0]
    half = tq // 2

    @pl.when(ki == 0)
    def _():
        acc_sc[...] = jnp.zeros(acc_sc.shape, F32)
        if not bounded:
            m_scratch[0][...] = jnp.full(m_scratch[0].shape, -jnp.inf, F32)

    def update(q0, qn, k0, kn, masked):
        for h in range(2):
            cols = slice(h * HEAD_PAD, (h + 1) * HEAD_PAD)
            s = _dot_nt(q_ref[q0:q0 + qn, cols], k_ref[k0:k0 + kn, cols])
            v = v_ref[k0:k0 + kn, cols]
            if masked:
                keep = (lax.broadcasted_iota(jnp.int32, (qn, kn), 1) <= lax.broadcasted_iota(jnp.int32, (qn, kn), 0))
            if bounded:
                p = jnp.exp2(s)
                if masked:
                    p = jnp.where(keep, p, 0.0)
                acc_sc[h, q0:q0 + qn, :] += _dot(p.astype(BF16), v)
            else:
                m_sc = m_scratch[0]
                if masked:
                    s = jnp.where(keep, s, -jnp.inf)
                m_prev = m_sc[h, q0:q0 + qn, :]
                m_next = jnp.maximum(m_prev, jnp.max(s, axis=-1, keepdims=True))
                alpha = jnp.exp2(m_prev - m_next)
                p = jnp.exp2(s - m_next[:, :1])
                acc_sc[h, q0:q0 + qn, :] = alpha * acc_sc[h, q0:q0 + qn, :] + _dot(p.astype(BF16), v)
                m_sc[h, q0:q0 + qn, :] = m_next

    @pl.when(ki < qi)
    def _():
        update(0, tq, 0, tq, False)

    @pl.when(ki == qi)
    def _():
        qt = tq // 4
        for qs in range(4):
            for ks in range(qs + 1):
                update(qs * qt, qt, ks * qt, qt, ks == qs)
        lane = lax.broadcasted_iota(jnp.int32, (tq, LANE), 1)
        a0 = acc_sc[0]
        a1 = acc_sc[1]
        o0 = a0 / a0[:, V_HEAD:V_HEAD + 1]
        o1 = a1 / a1[:, 0:1]
        o_ref[...] = jnp.where(lane < V_HEAD, o0, o1).astype(o_ref.dtype)


def _prompt_attention(q, k, v, b, s, *, bounded):
    tq = min(ATTN_TQ, s)
    nq = s // tq
    pairs = [(i, j) for i in range(nq) for j in range(i + 1)]
    qi_tab = jnp.asarray([p[0] for p in pairs], jnp.int32)
    ki_tab = jnp.asarray([p[1] for p in pairs], jnp.int32)
    blk = lambda sel: pl.BlockSpec((None, tq, 2 * HEAD_PAD), lambda bi, h, t, qt, kt: (bi, sel(qt, kt)[t], h))
    scratch = [pltpu.VMEM((2, tq, LANE), F32)]
    if not bounded:
        scratch.append(pltpu.VMEM((2, tq, LANE), F32))
    grid_spec = pltpu.PrefetchScalarGridSpec(
        num_scalar_prefetch=2,
        grid=(b, H_A // 2, len(pairs)),
        in_specs=[blk(lambda qt, kt: qt), blk(lambda qt, kt: kt), blk(lambda qt, kt: kt)],
        out_specs=pl.BlockSpec((None, tq, 2 * V_HEAD), lambda bi, h, t, qt, kt: (bi, qt[t], h)),
        scratch_shapes=scratch,
    )
    return pl.pallas_call(
        functools.partial(_attn_kernel, bounded=bounded),
        grid_spec=grid_spec,
        out_shape=jax.ShapeDtypeStruct((b, s, H_A * V_HEAD), BF16),
        compiler_params=_cparams(("parallel", "parallel", "arbitrary")),
        name="prompt_attn_bounded" if bounded else "prompt_attn",
    )(qi_tab, ki_tab, q, k, v)


def _ssd_kernel(xbc_ref, z_ref, dt_ref, wconv_ref, bconv_ref, dtb_ref, ah_ref, dskip_ref, gn_ref,
                tri_ref, exp_ref, y_ref, hfin_ref, xbuf, state):
    ci = pl.program_id(1)
    nc = pl.num_programs(1)
    L = SSD_CHUNK

    @pl.when(ci == 0)
    def _():
        xbuf[0:8, :] = jnp.zeros((8, CONV_DIM), F32)
        state[...] = jnp.zeros(state.shape, F32)

    xbuf[8:8 + L, :] = xbc_ref[...]
    conv = bconv_ref[...] + xbuf[8:8 + L, :] * wconv_ref[3:4, :]
    for w in range(CONV_W - 1):
        sh = CONV_W - 1 - w
        conv = conv + xbuf[8 - sh:8 - sh + L, :] * wconv_ref[w:w + 1, :]
    xbuf[0:8, :] = xbuf[L:L + 8, :]
    xc = _silu(conv)
    xs = xc[:, :D_INNER]

    dt = jax.nn.softplus(dt_ref[...] + dtb_ref[...])
    a = dt * ah_ref[...]
    a_cum = _dot_f32_rhs(tri_ref[...], a)
    a_cum_t = a_cum.T
    a_last = a_cum[L - 1:L, :]
    ex = exp_ref[...]
    dt_x = _dot_f32_lhs(dt, ex, terms=2)
    dfs_x = _dot_f32_lhs(jnp.exp(a_cum), ex, terms=2)
    dte_x = _dot_f32_lhs(jnp.exp(a_last - a_cum), ex, terms=2)
    xdt = xs * dt_x
    xdt_b = xdt.astype(BF16)
    xw_b = (xdt * dte_x).astype(BF16)
    chunk_decay = jnp.exp(a_cum_t[:, L - 1:L])

    row = lax.broadcasted_iota(jnp.int32, (L, L), 0)
    col = lax.broadcasted_iota(jnp.int32, (L, L), 1)
    causal = col <= row
    lane = lax.broadcasted_iota(jnp.int32, (L, LANE), 1)
    for g in range(N_GROUPS):
        bm = xc[:, D_INNER + g * D_STATE:D_INNER + (g + 1) * D_STATE].astype(BF16)
        cm = xc[:, D_INNER + (N_GROUPS + g) * D_STATE:D_INNER + (N_GROUPS + g + 1) * D_STATE].astype(BF16)
        cb = _dot_nt(cm, bm)
        c0, c1 = g * GROUP_W, (g + 1) * GROUP_W
        st_prev = state[c0:c1, :]
        y_off = _dot_nt(cm, st_prev.astype(BF16)) * dfs_x[:, c0:c1]
        for j in range(HEADS_PER_GROUP // 2):
            h0 = g * HEADS_PER_GROUP + 2 * j
            x2 = xdt_b[:, h0 * SSM_HEAD:(h0 + 2) * SSM_HEAD]
            ys = []
            for hh in (h0, h0 + 1):
                seg = a_cum[:, hh:hh + 1] - a_cum_t[hh:hh + 1, :]
                sc = jnp.where(causal, cb * jnp.exp(seg), 0.0)
                ys.append(_dot(sc.astype(BF16), x2))
            y2 = jnp.where(lane < SSM_HEAD, ys[0], ys[1])
            lo = h0 * SSM_HEAD
            y_ref[:, lo:lo + LANE] = y2 + y_off[:, lo - c0:lo - c0 + LANE]
        st_new = _dot_tn(xw_b[:, c0:c1], bm)
        carry = jnp.concatenate(
            [jnp.broadcast_to(chunk_decay[hh:hh + 1, :], (SSM_HEAD, D_STATE))
             for hh in range(g * HEADS_PER_GROUP, (g + 1) * HEADS_PER_GROUP)], axis=0)
        state[c0:c1, :] = st_prev * carry + st_new

    y = y_ref[...] + dskip_ref[...] * xs
    yg = y * _silu(z_ref[...])
    gn = gn_ref[...]
    for g in range(N_GROUPS):
        c0, c1 = g * GROUP_W, (g + 1) * GROUP_W
        blk = yg[:, c0:c1]
        y_ref[:, c0:c1] = blk * lax.rsqrt(jnp.mean(blk * blk, axis=-1, keepdims=True) + EPS) * gn[:, c0:c1]

    @pl.when(ci == nc - 1)
    def _():
        hfin_ref[...] = state[...]


def _ssd_prompt(xbc, z, dt, sw, b, s):
    nc = s // SSD_CHUNK
    row = lambda w: pl.BlockSpec((SSD_CHUNK, w), lambda bi, ci: (bi * nc + ci, 0))
    consts = (sw['w_conv'], sw['b_conv'], sw['dt_bias128'], sw['a_head128'], sw['d_skip_x'], sw['g_norm'],
              sw['tri'], sw['expand'])
    return pl.pallas_call(
        _ssd_kernel,
        grid=(b, nc),
        in_specs=[row(CONV_DIM), row(D_INNER), row(LANE)] + [_whole()] * len(consts),
        out_specs=(row(D_INNER), pl.BlockSpec((None, D_INNER, D_STATE), lambda bi, ci: (bi, 0, 0))),
        out_shape=(jax.ShapeDtypeStruct((b * s, D_INNER), F32),
                   jax.ShapeDtypeStruct((b, D_INNER, D_STATE), F32)),
        scratch_shapes=[pltpu.VMEM((SSD_CHUNK + 8, CONV_DIM), F32), pltpu.VMEM((D_INNER, D_STATE), F32)],
        compiler_params=_cparams(("parallel", "arbitrary")),
        name="ssd_prompt",
    )(xbc, z, dt, *consts)


def _ssm_step_kernel(xbc_ref, cbuf_ref, z_ref, dt_ref, h_ref, wconv_ref, bconv_ref, dtb_ref, ah_ref,
                     dskip_ref, gn_ref, exp_ref, y_ref, hnew_ref):
    conv = bconv_ref[...] + xbc_ref[0] * wconv_ref[3:4, :]
    for w in range(CONV_W - 1):
        conv = conv + cbuf_ref[0, w:w + 1, :] * wconv_ref[w:w + 1, :]
    xc = _silu(conv)
    xs = xc[:, :D_INNER]
    dt = jax.nn.softplus(dt_ref[0] + dtb_ref[...])
    da = jnp.exp(dt * ah_ref[...])
    ex = exp_ref[...]
    xdt = xs * _dot_f32_lhs(jnp.broadcast_to(dt, (8, LANE)), ex)[0:1, :]

    eye = lax.broadcasted_iota(jnp.int32, (LANE, LANE), 0) == lax.broadcasted_iota(jnp.int32, (LANE, LANE), 1)

    def to_col(rowvec):
        return jnp.sum(jnp.where(eye, jnp.broadcast_to(rowvec, (LANE, LANE)), 0.0), axis=-1, keepdims=True)

    ys = []
    for g in range(N_GROUPS):
        bt = xc[:, D_INNER + g * D_STATE:D_INNER + (g + 1) * D_STATE]
        ct = xc[:, D_INNER + (N_GROUPS + g) * D_STATE:D_INNER + (N_GROUPS + g + 1) * D_STATE]
        ct8 = jnp.broadcast_to(ct, (8, D_STATE)).astype(BF16)
        for j in range(GROUP_W // LANE):
            r0 = g * GROUP_W + j * LANE
            h0 = r0 // SSM_HEAD
            h_old = h_ref[0, r0:r0 + LANE, :]
            decay = jnp.concatenate([jnp.broadcast_to(da[:, hh:hh + 1], (SSM_HEAD, D_STATE)) for hh in (h0, h0 + 1)],
                                    axis=0)
            h_new = h_old * decay + to_col(xdt[:, r0:r0 + LANE]) * bt
            hnew_ref[0, r0:r0 + LANE, :] = h_new
            ys.append(_dot_nt(ct8, h_new.astype(BF16))[0:1, :])
    y = jnp.concatenate(ys, axis=-1) + dskip_ref[...] * xs
    yg = y * _silu(z_ref[0])
    gn = gn_ref[...]
    for g in range(N_GROUPS):
        c0, c1 = g * GROUP_W, (g + 1) * GROUP_W
        blk = yg[:, c0:c1]
        y_ref[0, :, c0:c1] = blk * lax.rsqrt(jnp.mean(blk * blk, axis=-1, keepdims=True) + EPS) * gn[:, c0:c1]


def _ssm_sample(xbc, conv_buf, z, dt, h, sw):
    bd = xbc.shape[0]
    vec = lambda w: pl.BlockSpec((1, 1, w), lambda i: (i, 0, 0))
    consts = (sw['w_conv'], sw['b_conv'], sw['dt_bias128'], sw['a_head128'], sw['d_skip_x'], sw['g_norm'],
              sw['expand'])
    y, h_new = pl.pallas_call(
        _ssm_step_kernel,
        grid=(bd,),
        in_specs=[vec(CONV_DIM), pl.BlockSpec((1, CONV_W - 1, CONV_DIM), lambda i: (i, 0, 0)), vec(D_INNER),
                  vec(LANE), pl.BlockSpec((1, D_INNER, D_STATE), lambda i: (i, 0, 0))] + [_whole()] * len(consts),
        out_specs=(vec(D_INNER), pl.BlockSpec((1, D_INNER, D_STATE), lambda i: (i, 0, 0))),
        out_shape=(jax.ShapeDtypeStruct((bd, 1, D_INNER), F32),
                   jax.ShapeDtypeStruct((bd, D_INNER, D_STATE), F32)),
        compiler_params=_cparams(("parallel",)),
        name="ssm_sample",
    )(xbc.reshape(bd, 1, CONV_DIM), conv_buf, z.reshape(bd, 1, D_INNER), dt.reshape(bd, 1, LANE), h, *consts)
    return y.reshape(bd, D_INNER), h_new


def _decode_kernel(pt_ref, qa_ref, qr_ref, gkr_ref, wuk_ref, segt_ref, ones_ref, cache_c, cache_kr,
                   acc_ref, m_ref, l_ref, cpage, krpage, cb_sc, pcat_sc, krb_sc, kr2b_sc, sem_c, sem_kr):
    b = pl.program_id(0)
    nb = pl.num_programs(0)
    n_chunks = pt_ref.shape[1] // DEC_PPS
    page = cpage.shape[2]
    sub_rows = DEC_SUB * page
    n_sub = DEC_PPS // DEC_SUB
    nj = (H_A * QK_NOPE) // LANE

    def page_copies(pid, slot, i):
        return (pltpu.make_async_copy(cache_c.at[pid], cpage.at[slot, i], sem_c.at[slot]),
                pltpu.make_async_copy(cache_kr.at[pid], krpage.at[slot, i], sem_kr.at[slot]))

    def issue(bb, c, slot):
        for i in range(DEC_PPS):
            for cp in page_copies(pt_ref[bb, c * DEC_PPS + i], slot, i):
                cp.start()

    def wait(slot):
        for i in range(DEC_PPS):
            for cp in page_copies(0, slot, i):
                cp.wait()

    @pl.when(b == 0)
    def _():
        cb_sc[1] = jnp.zeros(cb_sc.shape[1:], BF16)
        pcat_sc[1] = jnp.zeros(pcat_sc.shape[1:], BF16)
        krb_sc[1] = jnp.zeros(krb_sc.shape[1:], BF16)
        kr2b_sc[1] = jnp.zeros(kr2b_sc.shape[1:], BF16)
        issue(0, 0, 0)

    qr = (qr_ref[0] * gkr_ref[...]).astype(BF16)
    q_abs = qa_ref[0]
    qa_hi = q_abs.astype(BF16)
    qa_lo = (q_abs - qa_hi.astype(F32)).astype(BF16)
    qa2 = jnp.concatenate([qa_hi, qa_lo], axis=0)

    def keys_partial(slot, sub):
        cb = jnp.concatenate([cpage[slot, sub * DEC_SUB + t].astype(BF16) for t in range(DEC_SUB)], axis=0)
        r0 = sub * sub_rows
        cb_sc[slot, r0:r0 + sub_rows, :] = cb
        kk = _dot(cb, wuk_ref[...])
        p_sq = None
        for jj in range(nj):
            blk = kk[:, jj * LANE:(jj + 1) * LANE]
            sq = blk * blk
            p_sq = sq if p_sq is None else p_sq + sq
        pcat_sc[slot, r0:r0 + sub_rows, :] = p_sq.astype(BF16)

    def rope_keys(slot):
        krt = jnp.concatenate([krpage[slot, i] for i in range(DEC_PPS)], axis=1)
        krb_sc[slot] = krt.astype(BF16)
        kr2b_sc[slot] = (krt * krt).astype(BF16)

    def scores(slot):
        s2 = _dot_nt(qa2, cb_sc[slot])
        ns = _dot_nt(segt_ref[...], pcat_sc[slot])
        ns = ns + _dot(qr, krb_sc[slot]) + _dot(ones_ref[...], kr2b_sc[slot])
        s = s2[0:H_A, :] + s2[H_A:2 * H_A, :] + ns[H_A:2 * H_A, :]
        return s * lax.rsqrt(ns[0:H_A, :] * (1.0 / QK_HEAD) + EPS)

    def softmax_step(s, m_prev, l_prev):
        m_next = jnp.maximum(m_prev, jnp.max(s, axis=-1, keepdims=True))
        alpha = jnp.exp2(m_prev - m_next)
        p = jnp.exp2(s - m_next)
        return p, alpha, m_next, alpha * l_prev + jnp.sum(p, axis=-1, keepdims=True)

    def chunk_step(c, slot, carry):
        m_prev, l_prev, acc_prev = carry
        wait(slot)
        last = c + 1 == n_chunks
        nxt_b = jnp.minimum(jnp.where(last, b + 1, b), nb - 1)
        nxt_c = jnp.where(last, 0, c + 1)
        issue(nxt_b, nxt_c, 1 - slot)

        valid = c > 0
        prev = 1 - slot
        keys_partial(slot, 0)
        s = scores(prev)
        keys_partial(slot, 1)
        p, alpha, m_next, l_next = softmax_step(s, m_prev, l_prev)
        keys_partial(slot, 2)
        acc_next = acc_prev * alpha + _dot(p.astype(BF16), cb_sc[prev])
        for sub in range(3, n_sub):
            keys_partial(slot, sub)
        rope_keys(slot)
        return (jnp.where(valid, m_next, m_prev), jnp.where(valid, l_next, l_prev),
                jnp.where(valid, acc_next, acc_prev))

    def pair_step(i, carry):
        carry = chunk_step(2 * i, 0, carry)
        return chunk_step(2 * i + 1, 1, carry)

    init = (jnp.full((H_A, 1), -jnp.inf, F32), jnp.zeros((H_A, 1), F32), jnp.zeros((H_A, KV_LORA), F32))
    m_run, l_run, acc = lax.fori_loop(0, n_chunks // 2, pair_step, init)
    p, alpha, m_run, l_run = softmax_step(scores(1), m_run, l_run)
    acc_ref[0] = acc * alpha + _dot(p.astype(BF16), cb_sc[1])
    m_ref[0] = jnp.broadcast_to(m_run, (H_A, LANE))
    l_ref[0] = jnp.broadcast_to(l_run, (H_A, LANE))

    @pl.when(b == nb - 1)
    def _():
        wait(0)


def _absorb_kernel(q_ref, gk_ref, wukh_ref, qa_ref):
    gk = gk_ref[...]
    for h in range(H_A):
        qg = q_ref[:, h * HEAD_PAD:h * HEAD_PAD + QK_NOPE] * gk[:, :QK_NOPE]
        hi = qg.astype(BF16)
        lo = (qg - hi.astype(F32)).astype(BF16)
        qa_ref[:, h * KV_LORA:(h + 1) * KV_LORA] = _dot(hi, wukh_ref[h]) + _dot(lo, wukh_ref[h])


def _absorb_queries(q, dw):
    bd = q.shape[0]
    return pl.pallas_call(
        _absorb_kernel,
        in_specs=[_whole()] * 3,
        out_specs=_whole(),
        out_shape=jax.ShapeDtypeStruct((bd, H_A * KV_LORA), F32),
        compiler_params=pltpu.CompilerParams(vmem_limit_bytes=VMEM_LIMIT),
        name="absorb_queries",
    )(q, dw['g_k128'], dw['w_uk_heads'])


def _decode_attention(page_table, q_abs, qr_mat, cache_c, cache_krt, dw):
    bd, n_pages = page_table.shape
    page = cache_c.shape[1]
    rows = DEC_PPS * page
    assert n_pages % (2 * DEC_PPS) == 0 and DEC_PPS // DEC_SUB >= 3
    const = lambda shp: pl.BlockSpec(shp, lambda b, pt: (0,) * len(shp))
    in_specs = [pl.BlockSpec((1, H_A, KV_LORA), lambda b, pt: (b, 0, 0)),
                pl.BlockSpec((1, 2 * H_A, QK_ROPE), lambda b, pt: (b, 0, 0)),
                const((1, QK_ROPE)), const((KV_LORA, H_A * QK_NOPE)),
                const((2 * H_A, LANE)), const((2 * H_A, QK_ROPE)),
                pl.BlockSpec(memory_space=pl.ANY), pl.BlockSpec(memory_space=pl.ANY)]
    out_b = lambda w: pl.BlockSpec((1, H_A, w), lambda b, pt: (b, 0, 0))
    grid_spec = pltpu.PrefetchScalarGridSpec(
        num_scalar_prefetch=1,
        grid=(bd,),
        in_specs=in_specs,
        out_specs=(out_b(KV_LORA), out_b(LANE), out_b(LANE)),
        scratch_shapes=[pltpu.VMEM((2, DEC_PPS, page, KV_LORA), F32),
                        pltpu.VMEM((2, DEC_PPS, QK_ROPE, page), F32),
                        pltpu.VMEM((2, rows, KV_LORA), BF16),
                        pltpu.VMEM((2, rows, LANE), BF16),
                        pltpu.VMEM((2, QK_ROPE, rows), BF16),
                        pltpu.VMEM((2, QK_ROPE, rows), BF16),
                        pltpu.SemaphoreType.DMA((2,)), pltpu.SemaphoreType.DMA((2,))],
    )
    return pl.pallas_call(
        _decode_kernel,
        grid_spec=grid_spec,
        out_shape=(jax.ShapeDtypeStruct((bd, H_A, KV_LORA), F32),
                   jax.ShapeDtypeStruct((bd, H_A, LANE), F32),
                   jax.ShapeDtypeStruct((bd, H_A, LANE), F32)),
        compiler_params=_cparams(("arbitrary",)),
        name="decode_attn",
    )(page_table, q_abs, qr_mat, dw['gk_rope'], dw['w_uk_perm'], dw['seg_t'], dw['ones_rows'], cache_c, cache_krt)


def _decode_final_kernel(q_ref, k_ref, c_ref, m_ref, l_ref, acc_ref, wuv_ref, o_ref):
    c_new = c_ref[...]
    for pair in range(H_A // 2):
        o_pair = None
        for h in (2 * pair, 2 * pair + 1):
            lo, hi = h * HEAD_PAD, (h + 1) * HEAD_PAD
            s_new = jnp.sum(q_ref[:, lo:hi] * k_ref[:, lo:hi].astype(F32), axis=-1, keepdims=True)
            m_old = m_ref[:, h:h + 1]
            m_new = jnp.maximum(m_old, s_new)
            a = jnp.exp2(m_old - m_new)
            pn = jnp.exp2(s_new - m_new)
            l_new = l_ref[:, h:h + 1] * a + pn
            ctx = (acc_ref[:, h * KV_LORA:(h + 1) * KV_LORA] * a + pn * c_new) / l_new
            part = _dot(ctx.astype(BF16), wuv_ref[h])
            o_pair = part if o_pair is None else o_pair + part
        o_ref[:, pair * LANE:(pair + 1) * LANE] = o_pair.astype(o_ref.dtype)


def _decode_final(q, k_new, c_new, m, l, acc, dw):
    bd = q.shape[0]
    return pl.pallas_call(
        _decode_final_kernel,
        in_specs=[_whole()] * 7,
        out_specs=_whole(),
        out_shape=jax.ShapeDtypeStruct((bd, H_A * V_HEAD), BF16),
        compiler_params=pltpu.CompilerParams(vmem_limit_bytes=VMEM_LIMIT),
        name="decode_final",
    )(q, k_new, c_new, m, l, acc, dw['w_uv_pair'])


def _merge_kernel(x_ref, attn_ref, ssm_ref, gates_ref, wa_ref, wb_ref, wo_ref, gffn_ref, wr_ref, br_ref,
                  h_ref, hn_ref, logit_ref):
    g = jax.nn.sigmoid(gates_ref[...])
    a = _dot(attn_ref[...], wa_ref[...])
    b = _dot(ssm_ref[...].astype(BF16), wb_ref[...])
    mixed = g[:, :D_MODEL] * a + g[:, D_MODEL:] * b
    h = x_ref[...] + _dot(mixed.astype(BF16), wo_ref[...])
    h_ref[...] = h
    hn = _rms(h, gffn_ref[...])
    hn_ref[...] = hn
    hi, mid, _ = _split3(hn)
    w_hi = wr_ref[0]
    w_lo = wr_ref[1]
    logit_ref[...] = _dot(hi, w_hi) + _dot(mid, w_hi) + _dot(hi, w_lo) + br_ref[...]


def _merge(x, attn, ssm, gates, mw, *, tm):
    t = x.shape[0]
    row = lambda w: pl.BlockSpec((tm, w), lambda i: (i, 0))
    consts = (mw['w_a_out'], mw['w_b_out'], mw['w_out'], mw['g_ffn'], mw['w_router2'], mw['b_router128'])
    return pl.pallas_call(
        _merge_kernel,
        grid=(t // tm,),
        in_specs=[row(D_MODEL), row(H_A * V_HEAD), row(D_INNER), row(2 * D_MODEL)] + [_whole()] * len(consts),
        out_specs=(row(D_MODEL), row(D_MODEL), row(LANE)),
        out_shape=(jax.ShapeDtypeStruct((t, D_MODEL), F32), jax.ShapeDtypeStruct((t, D_MODEL), F32),
                   jax.ShapeDtypeStruct((t, LANE), F32)),
        compiler_params=_cparams(("parallel",)),
        name="merge",
    )(x, attn, ssm, gates, *consts)


def _moe_kernel(be_ref, cnt_ref, seg_ref, tok_ref, asg_ref, run_ref, nxt_ref, x_hbm, wgu_hbm, bgu_ref, wd_hbm, bd_ref,
                y_hbm, xbuf, obuf, wgu_f, wd_f, wgu_b, wd_b, sem_in, sem_out, sem_w):
    i = pl.program_id(0)
    n = pl.num_programs(0)
    bm = xbuf.shape[1]
    dump0 = y_hbm.shape[0] - bm

    def weight_copies(expert, wslot):
        return (pltpu.make_async_copy(wgu_hbm.at[expert], wgu_f.at[wslot], sem_w.at[wslot]),
                pltpu.make_async_copy(wd_hbm.at[expert], wd_f.at[wslot], sem_w.at[wslot]))

    def in_copy(tok, slot, r):
        return pltpu.make_async_copy(x_hbm.at[pl.ds(tok, 1), :], xbuf.at[slot, pl.ds(r, 1), :], sem_in.at[slot])

    def out_copy(row, slot, r):
        return pltpu.make_async_copy(obuf.at[slot, pl.ds(r, 1), :], y_hbm.at[pl.ds(row, 1), :], sem_out.at[slot])

    def gather_loop(blk, slot):
        base = seg_ref[blk]

        def body(r, carry):
            in_copy(tok_ref[base + r], slot, r).start()
            return carry
        lax.fori_loop(0, bm, body, 0, unroll=8)

    def gather_inline(blk, slot):
        base = seg_ref[blk]
        for r in range(bm):
            in_copy(tok_ref[base + r], slot, r).start()

    def gather_wait(slot):
        pltpu.make_async_copy(xbuf.at[slot], xbuf.at[slot], sem_in.at[slot]).wait()

    def scattered(blk):
        return (blk < 0) | (cnt_ref[jnp.maximum(blk, 0)] > 0)

    def scatter_rows(blk):
        j = jnp.maximum(blk, 0)
        return seg_ref[j], jnp.where(blk >= 0, cnt_ref[j], 0)

    def scatter_loop(blk, slot):
        base, count = scatter_rows(blk)

        def body(r, carry):
            out_copy(jnp.where(r < count, asg_ref[base + r], dump0 + r), slot, r).start(priority=1)
            return carry
        lax.fori_loop(0, bm, body, 0, unroll=8)

    def scatter_inline(blk, slot):
        base, count = scatter_rows(blk)
        for r in range(bm):
            out_copy(jnp.where(count > r, asg_ref[base + r], dump0 + r), slot, r).start(priority=1)

    def scatter_wait(slot):
        pltpu.make_async_copy(obuf.at[slot], obuf.at[slot], sem_out.at[slot]).wait()

    valid = cnt_ref[i] > 0
    nxt_valid = (i + 1 < n) & (cnt_ref[jnp.minimum(i + 1, n - 1)] > 0)

    @pl.when(i == 0)
    def _():
        obuf[1] = jnp.zeros(obuf.shape[1:], F32)

    @pl.when((i == 0) & valid)
    def _():
        gather_loop(0, 0)

    @pl.when((i >= 1) & scattered(i - 2))
    def _():
        scatter_wait(i % 2)

    @pl.when((i == 0) & (run_ref[0] >= 0))
    def _():
        for cp in weight_copies(be_ref[0], 0):
            cp.start()

    for wslot in range(2):
        @pl.when((run_ref[i] >= 0) & (run_ref[i] % 2 == wslot))
        def _():
            for cp in weight_copies(0, wslot):
                cp.wait()
            wgu_b[...] = wgu_f[wslot].astype(BF16)
            wd_b[...] = wd_f[wslot].astype(BF16)

            @pl.when(nxt_ref[i] >= 0)
            def _():
                for cp in weight_copies(nxt_ref[i], 1 - wslot):
                    cp.start()

    def expert_block(prefetch_next, slot):
        gather_wait(slot)
        x = xbuf[slot].astype(BF16)
        scatter_inline(i - 1, 1 - slot)
        if prefetch_next:
            gather_inline(i + 1, 1 - slot)
        gu = _dot(x, wgu_b[...]) + bgu_ref[0]
        gate = jnp.minimum(gu[:, :D_FF], SWIGLU_LIMIT)
        up = jnp.clip(gu[:, D_FF:], -SWIGLU_LIMIT, SWIGLU_LIMIT)
        act = (up + 1.0) * gate * jax.nn.sigmoid(SWIGLU_ALPHA * gate)
        obuf[slot] = _dot(act.astype(BF16), wd_b[...]) + bd_ref[0]

    for parity in range(2):
        on_parity = valid & (i % 2 == parity)

        @pl.when(on_parity & nxt_valid)
        def _():
            expert_block(True, parity)

        @pl.when(on_parity & jnp.logical_not(nxt_valid))
        def _():
            expert_block(False, parity)

    @pl.when(jnp.logical_not(valid) & scattered(i - 1))
    def _():
        scatter_loop(i - 1, (i + 1) % 2)

    @pl.when((i == n - 1) & scattered(i - 1))
    def _():
        scatter_wait((i + 1) % 2)


def _moe_experts(block_e, block_cnt, seg_start, tok_sorted, asg_sorted, run_open, run_next, x, w_gate_up, b_gate_up,
                 w_down, b_down):
    n_blocks = block_e.shape[0]
    bm = MOE_BM
    n_assign = x.shape[0] * TOP_K
    b_spec = lambda shp: pl.BlockSpec(shp, lambda i, be, *_: (be[i], 0, 0))
    hbm = pl.BlockSpec(memory_space=pl.ANY)
    grid_spec = pltpu.PrefetchScalarGridSpec(
        num_scalar_prefetch=7,
        grid=(n_blocks,),
        in_specs=[hbm, hbm, b_spec((1, 1, 2 * D_FF)), hbm, b_spec((1, 1, D_MODEL))],
        out_specs=hbm,
        scratch_shapes=[pltpu.VMEM((2, bm, D_MODEL), F32), pltpu.VMEM((2, bm, D_MODEL), F32),
                        pltpu.VMEM((2, D_MODEL, 2 * D_FF), F32), pltpu.VMEM((2, D_FF, D_MODEL), F32),
                        pltpu.VMEM((D_MODEL, 2 * D_FF), BF16), pltpu.VMEM((D_FF, D_MODEL), BF16),
                        pltpu.SemaphoreType.DMA((2,)), pltpu.SemaphoreType.DMA((2,)),
                        pltpu.SemaphoreType.DMA((2,))],
    )
    return pl.pallas_call(
        _moe_kernel,
        grid_spec=grid_spec,
        out_shape=jax.ShapeDtypeStruct((n_assign + bm, D_MODEL), F32),
        compiler_params=_cparams(("arbitrary",)),
        name="moe_experts",
    )(block_e, block_cnt, seg_start, tok_sorted, asg_sorted, run_open, run_next, x, w_gate_up,
      b_gate_up.reshape(N_EXPERTS, 1, 2 * D_FF), w_down, b_down.reshape(N_EXPERTS, 1, D_MODEL))


def _ple_kernel(y_ref, hp_ref, hs_ref, gw_ref, pp_ref, ps_ref, gin_ref, wgate_ref, wple_ref, gple_ref,
                op_ref, os_ref):
    i = pl.program_id(0)
    n = pl.num_programs(0)
    tm = hp_ref.shape[0]
    is_sample = i == n - 1
    gw = gw_ref[...]
    y = y_ref[0:tm, :] * gw[:, 0:1]
    for k in range(1, TOP_K):
        y = y + y_ref[k * tm:(k + 1) * tm, :] * gw[:, k:k + 1]
    h2 = jnp.where(is_sample, hs_ref[...], hp_ref[...]) + y
    p_emb = jnp.where(is_sample, ps_ref[...], pp_ref[...])
    gate = jax.nn.sigmoid(_dot(_rms(h2, gin_ref[...]).astype(BF16), wgate_ref[...]))
    ple = _rms(_dot(p_emb.astype(BF16), wple_ref[...]), gple_ref[...]) * gate
    out = h2 + ple

    @pl.when(i < n - 1)
    def _():
        op_ref[...] = out

    @pl.when(i == n - 1)
    def _():
        os_ref[...] = out


def _combine_ple(y_rows, h_p, h_s, gate_w, p_p, p_s, pw):
    n_prompt = h_p.shape[0]
    tm = PLE_TM
    assert h_s.shape[0] == tm and n_prompt % tm == 0
    n_tiles = n_prompt // tm + 1
    prompt_row = lambda w: pl.BlockSpec((tm, w), lambda i: (jnp.minimum(i, n_tiles - 2), 0))
    sample_row = lambda w: pl.BlockSpec((tm, w), lambda i: (0, 0))
    return pl.pallas_call(
        _ple_kernel,
        grid=(n_tiles,),
        in_specs=[pl.BlockSpec((tm * TOP_K, D_MODEL), lambda i: (i, 0)), prompt_row(D_MODEL), sample_row(D_MODEL),
                  pl.BlockSpec((tm, LANE), lambda i: (i, 0)), prompt_row(PLE_DIM), sample_row(PLE_DIM)]
                 + [_whole()] * 4,
        out_specs=(prompt_row(D_MODEL), sample_row(D_MODEL)),
        out_shape=(jax.ShapeDtypeStruct((n_prompt, D_MODEL), F32), jax.ShapeDtypeStruct((tm, D_MODEL), F32)),
        compiler_params=_cparams(("arbitrary",)),
        name="combine_ple",
    )(y_rows, h_p, h_s, gate_w, p_p, p_s, pw['g_ple_in'], pw['w_ple_gate'], pw['w_ple'], pw['g_ple'])


def _pad_lanes(x, n):
    return jnp.pad(x, [(0, 0)] * (x.ndim - 1) + [(0, n - x.shape[-1])])


def _rot_cols(w):
    half = QK_ROPE // 2
    return jnp.concatenate([-w[..., half:], w[..., :half]], axis=-1)


def _prep_weights(g_mix_norm, w_in, g_q_a, w_q_b, g_kv_a, w_uk, w_uv, g_q_head, g_k_head, w_a_out, w_conv,
                  b_conv, dt_bias, a_log, d_skip, g_ssm_norm, w_b_out, w_out, g_ffn_norm, w_router, b_router,
                  g_ple_in, w_ple_gate, w_ple, g_ple):
    sizes = (Q_LORA, KV_LORA, QK_ROPE, D_INNER, CONV_DIM, H_B, 2 * D_MODEL)
    offs = np.concatenate([[0], np.cumsum(sizes)])
    wq, wkv, wkr, wz, wxbc, wdt, wg = [w_in[:, int(offs[i]):int(offs[i + 1])] for i in range(7)]
    zc = lambda n: jnp.zeros((D_MODEL, n), F32)
    kr128 = jnp.concatenate([zc(QK_NOPE), wkr, zc(LANE - QK_HEAD)], axis=1)
    krrot = jnp.concatenate([zc(QK_NOPE), _rot_cols(wkr), zc(LANE - QK_HEAD)], axis=1)
    w_lat = jnp.concatenate([wq, wkv, kr128, krrot, _pad_lanes(wdt, LANE)], axis=1).astype(BF16)

    wqb = w_q_b.reshape(Q_LORA, H_A, QK_HEAD)
    nope, rope_w = wqb[..., :QK_NOPE], wqb[..., QK_NOPE:]
    z_nope = jnp.zeros_like(nope)
    q128 = _pad_lanes(jnp.concatenate([nope, rope_w], axis=-1), LANE).reshape(Q_LORA, H_A * LANE)
    qrot = _pad_lanes(jnp.concatenate([z_nope, _rot_cols(rope_w)], axis=-1), LANE).reshape(Q_LORA, H_A * LANE)
    w_qb = jnp.concatenate([q128, qrot], axis=1).astype(BF16)

    wuk3 = w_uk.reshape(KV_LORA, H_A, QK_NOPE)
    w_uk_pad = _pad_lanes(wuk3, LANE).reshape(KV_LORA, H_A * LANE).astype(BF16)
    sub = LANE // H_A
    w_uk_perm = wuk3.reshape(KV_LORA, H_A, QK_NOPE // sub, sub).transpose(0, 2, 1, 3)
    w_uk_perm = w_uk_perm.reshape(KV_LORA, H_A * QK_NOPE).astype(BF16)

    wuv3 = w_uv.reshape(KV_LORA, H_A, V_HEAD)
    even = (jnp.arange(H_A) % 2 == 0)[None, :, None]
    zv = jnp.zeros_like(wuv3)
    w_uv_ext = jnp.where(even, jnp.concatenate([wuv3, zv], axis=-1), jnp.concatenate([zv, wuv3], axis=-1))
    ones_lane = np.zeros((H_A, LANE), np.float32)
    ones_lane[0::2, V_HEAD] = 1.0
    ones_lane[1::2, 0] = 1.0
    front = dict(
        g_mix=g_mix_norm.reshape(1, -1), w_lat=w_lat, w_z=wz.astype(BF16), w_xbc=wxbc.astype(BF16),
        w_g=wg.astype(BF16), g_q_a=g_q_a.reshape(1, -1), w_qb=w_qb, g_kv_a=g_kv_a.reshape(1, -1),
        w_uk_pad=w_uk_pad, w_uv_ext=w_uv_ext.reshape(KV_LORA, H_A * LANE).astype(BF16),
        v_ones=jnp.asarray(ones_lane.reshape(1, H_A * LANE)),
        g_q128=_pad_lanes(g_q_head.reshape(1, -1), LANE) * Q_SCALE,
        g_k128=_pad_lanes(g_k_head.reshape(1, -1), LANE),
    )

    lane_head = np.arange(LANE) // sub
    seg = (lane_head[None, :] == np.arange(H_A)[:, None]).astype(np.float32)
    seg_t = np.zeros((2 * H_A, LANE), np.float32)
    seg_t[:H_A] = seg
    ones_rows = np.zeros((2 * H_A, QK_ROPE), np.float32)
    ones_rows[:H_A] = 1.0
    w_uv_pair = w_uv_ext
    decode = dict(
        g_k128=front['g_k128'], gk_rope=g_k_head[QK_NOPE:].reshape(1, QK_ROPE), w_uk_perm=w_uk_perm,
        w_uk_heads=wuk3.transpose(1, 2, 0).astype(BF16),
        seg_t=jnp.asarray(seg_t, BF16), ones_rows=jnp.asarray(ones_rows, BF16),
        w_uv_pair=w_uv_pair.transpose(1, 0, 2).astype(BF16),
    )

    expand = (np.arange(D_INNER)[None, :] // SSM_HEAD == np.arange(LANE)[:, None]).astype(np.float32)
    tri = (np.arange(SSD_CHUNK)[None, :] <= np.arange(SSD_CHUNK)[:, None]).astype(np.float32)
    ssm = dict(
        w_conv=w_conv, b_conv=b_conv.reshape(1, -1), dt_bias128=_pad_lanes(dt_bias.reshape(1, -1), LANE),
        a_head128=_pad_lanes(-jnp.exp(a_log).reshape(1, -1), LANE),
        d_skip_x=jnp.repeat(d_skip, SSM_HEAD).reshape(1, -1), g_norm=g_ssm_norm.reshape(1, -1),
        tri=jnp.asarray(tri, BF16), expand=jnp.asarray(expand, BF16),
    )

    wr = _pad_lanes(w_router, LANE)
    wr_hi = wr.astype(BF16)
    wr_lo = (wr - wr_hi.astype(F32)).astype(BF16)
    merge = dict(
        w_a_out=w_a_out.astype(BF16), w_b_out=w_b_out.astype(BF16), w_out=w_out.astype(BF16),
        g_ffn=g_ffn_norm.reshape(1, -1), w_router2=jnp.stack([wr_hi, wr_lo]),
        b_router128=_pad_lanes(b_router.reshape(1, -1), LANE),
    )
    ple = dict(g_ple_in=g_ple_in.reshape(1, -1), w_ple_gate=w_ple_gate.astype(BF16), w_ple=w_ple.astype(BF16),
               g_ple=g_ple.reshape(1, -1))
    return front, decode, ssm, merge, ple


def _rope_tables(pos):
    half = QK_ROPE // 2
    inv = ROPE_THETA ** (-jnp.arange(half, dtype=F32) / half)
    ang = pos.astype(F32)[:, None] * inv[None, :]
    cos, sin = jnp.cos(ang), jnp.sin(ang)
    n = pos.shape[0]
    cos128 = jnp.concatenate([jnp.ones((n, QK_NOPE), F32), cos, cos, jnp.ones((n, LANE - QK_HEAD), F32)], axis=1)
    sin128 = jnp.concatenate([jnp.zeros((n, QK_NOPE), F32), sin, sin, jnp.zeros((n, LANE - QK_HEAD), F32)], axis=1)
    return cos128, sin128


def _route(logits, n_tok):
    top_val, top_idx = lax.top_k(logits[:, :N_EXPERTS], TOP_K)
    gate_w = jax.nn.softmax(top_val, axis=-1)
    n_assign = n_tok * TOP_K
    flat_e = top_idx.reshape(-1).astype(jnp.int32)
    counts = jnp.sum((flat_e[:, None] == jnp.arange(N_EXPERTS, dtype=jnp.int32)[None, :]).astype(jnp.int32), axis=0)
    bm = MOE_BM
    padded = (counts + bm - 1) // bm * bm
    pend = jnp.cumsum(padded)
    pstarts = pend - padded
    starts = jnp.cumsum(counts) - counts
    idx_bits = max(1, (n_assign - 1).bit_length())
    key = jnp.sort(flat_e * (1 << idx_bits) + jnp.arange(n_assign, dtype=jnp.int32))
    asg = jnp.pad(key & ((1 << idx_bits) - 1), (0, bm)).astype(jnp.int32)
    k_bits, tm_bits = TOP_K.bit_length() - 1, PLE_TM.bit_length() - 1
    assert TOP_K == 1 << k_bits and PLE_TM == 1 << tm_bits
    tok_sorted = asg >> k_bits
    asg_sorted = (((asg >> (k_bits + tm_bits)) << (k_bits + tm_bits)) + ((asg & (TOP_K - 1)) << tm_bits)
                  + (tok_sorted & (PLE_TM - 1)))
    n_blocks = -(-n_assign // bm) + N_EXPERTS
    blk_start = jnp.arange(n_blocks, dtype=jnp.int32) * bm
    e_ids = jnp.arange(N_EXPERTS, dtype=jnp.int32)
    block_e = jnp.minimum(jnp.sum((pend[None, :] <= blk_start[:, None]).astype(jnp.int32), axis=1), N_EXPERTS - 1)
    block_e = block_e.astype(jnp.int32)
    is_e = (block_e[:, None] == e_ids[None, :]).astype(jnp.int32)
    of_block = lambda per_expert: jnp.sum(is_e * per_expert[None, :], axis=1)
    offset = blk_start - of_block(pstarts)
    block_cnt = jnp.where(blk_start < pend[-1], jnp.clip(of_block(counts) - offset, 0, bm), 0).astype(jnp.int32)
    seg_start = jnp.clip(of_block(starts) + offset, 0, n_assign).astype(jnp.int32)
    opens = (block_cnt > 0) & (offset == 0)
    run_open = jnp.where(opens, jnp.cumsum(opens.astype(jnp.int32)) - 1, -1).astype(jnp.int32)
    later = (e_ids[None, :] > e_ids[:, None]) & (counts[None, :] > 0)
    next_expert = of_block(jnp.min(jnp.where(later, e_ids[None, :], N_EXPERTS), axis=1))
    run_next = jnp.where(opens & (next_expert < N_EXPERTS), next_expert, -1).astype(jnp.int32)
    return gate_w, tok_sorted, asg_sorted, block_e, block_cnt, seg_start, run_open, run_next


def kernel(x_prompt, x_sample, p_prompt, p_sample, cache_kv_latent, cache_k_rope, page_table, state_conv, state_ssm, g_mix_norm, w_in, g_q_a, w_q_b, g_kv_a, w_uk, w_uv, g_q_head, g_k_head, w_a_out, w_conv, b_conv, dt_bias, a_log, d_skip, g_ssm_norm, w_b_out, w_out, g_ffn_norm, w_router, b_router, w_gate_up, b_gate_up, w_down, b_down, g_ple_in, w_ple_gate, w_ple, g_ple):
    depth = g_mix_norm.shape[0]
    assert depth == 1, "one layer"
    b, s, _ = x_prompt.shape
    bd, sd, _ = x_sample.shape
    assert sd == 1, "one new token per sample sequence"
    n_pages = page_table.shape[1]
    page_size = cache_kv_latent.shape[2]
    assert n_pages % DEC_PPS == 0 and s % SSD_CHUNK == 0
    tp = b * s

    fw, dw, sw, mw, pw = _prep_weights(
        g_mix_norm[0], w_in[0], g_q_a[0], w_q_b[0], g_kv_a[0], w_uk[0], w_uv[0], g_q_head[0], g_k_head[0],
        w_a_out[0], w_conv[0], b_conv[0], dt_bias[0], a_log[0], d_skip[0], g_ssm_norm[0], w_b_out[0], w_out[0],
        g_ffn_norm[0], w_router[0], b_router[0], g_ple_in[0], w_ple_gate[0], w_ple[0], g_ple[0])

    xp = x_prompt.reshape(tp, D_MODEL)
    cos_p, sin_p = _rope_tables(jnp.arange(s))
    tm_p = min(FRONT_TM, s)
    bound = QK_HEAD * jnp.max(jnp.abs(fw['g_q128'])) * jnp.max(jnp.abs(fw['g_k128']))
    bound = (1.02 * bound + 1.0).astype(BF16).astype(F32)
    bias_lane = (jnp.arange(LANE) == QK_HEAD).astype(F32).reshape(1, LANE)
    q_p, k_p, v_p, c_p, kr_p, z_p, xbc_p, dt_p, gates_p = _front(
        xp, cos_p, sin_p, fw, -bound * bias_lane, bias_lane, tm=tm_p, q_dtype=BF16, pos_blocks=s // tm_p)
    qkv = (q_p.reshape(b, s, -1), k_p.reshape(b, s, -1), v_p.reshape(b, s, -1))
    attn_p = lax.cond(bound <= ATTN_MAX_BOUND,
                      lambda q, k, v: _prompt_attention(q, k, v, b, s, bounded=True),
                      lambda q, k, v: _prompt_attention(q, k, v, b, s, bounded=False), *qkv)
    ssm_p, hfin_p = _ssd_prompt(xbc_p, z_p, dt_p, sw, b, s)
    h_p, hn_p, logit_p = _merge(xp, attn_p.reshape(tp, -1), ssm_p, gates_p, mw, tm=min(MERGE_TM, tp))

    xs = x_sample.reshape(bd, D_MODEL)
    cos_s, sin_s = _rope_tables(jnp.full((bd,), n_pages * page_size, jnp.int32))
    no_pad = jnp.zeros((1, LANE), F32)
    q_s, k_s, _, c_s, kr_s, z_s, xbc_s, dt_s, gates_s = _front(
        xs, cos_s, sin_s, fw, no_pad, no_pad, tm=bd, q_dtype=F32, pos_blocks=1)
    q3 = q_s.reshape(bd, H_A, LANE)
    q_abs = _absorb_queries(q_s, dw).reshape(bd, H_A, KV_LORA)
    qr = q3[:, :, QK_NOPE:QK_HEAD]
    qr_mat = jnp.concatenate([jnp.zeros_like(qr), qr], axis=1)
    acc, m_run, l_run = _decode_attention(page_table, q_abs, qr_mat, cache_kv_latent[0],
                                          jnp.swapaxes(cache_k_rope[0], 1, 2), dw)
    attn_s = _decode_final(q_s, k_s, c_s, m_run[:, :, 0], l_run[:, :, 0], acc.reshape(bd, -1), dw)
    ssm_s, hnew_s = _ssm_sample(xbc_s, state_conv[0], z_s, dt_s,
                                state_ssm[0].reshape(bd, D_INNER, D_STATE), sw)
    h_s, hn_s, logit_s = _merge(xs, attn_s, ssm_s, gates_s, mw, tm=bd)

    hn_all = jnp.concatenate([hn_p, hn_s], axis=0)
    logits = jnp.concatenate([logit_p, logit_s], axis=0)
    n_tok = tp + bd
    gate_w, tok_sorted, asg_sorted, block_e, block_cnt, seg_start, run_open, run_next = _route(logits, n_tok)
    y_rows = _moe_experts(block_e, block_cnt, seg_start, tok_sorted, asg_sorted, run_open, run_next, hn_all,
                          w_gate_up[0], b_gate_up[0], w_down[0], b_down[0])
    out_p, out_s = _combine_ple(y_rows, h_p, h_s, _pad_lanes(gate_w, LANE), p_prompt[0].reshape(tp, PLE_DIM),
                                p_sample[0].reshape(bd, PLE_DIM), pw)

    y_prompt = out_p.reshape(b, s, D_MODEL)
    y_sample = out_s.reshape(bd, sd, D_MODEL)
    new_c_p = c_p.reshape(1, b, s, KV_LORA)
    new_kr_p = kr_p[:, QK_NOPE:QK_HEAD].reshape(1, b, s, QK_ROPE)
    conv_p = xbc_p.reshape(b, s, CONV_DIM)[:, s - (CONV_W - 1):].reshape(1, b, CONV_W - 1, CONV_DIM)
    ssm_state_p = hfin_p.reshape(1, b, H_B, SSM_HEAD, D_STATE)
    new_c_s = c_s.reshape(1, bd, sd, KV_LORA)
    new_kr_s = kr_s[:, QK_NOPE:QK_HEAD].reshape(1, bd, sd, QK_ROPE)
    conv_s = jnp.concatenate([state_conv[0][:, 1:], xbc_s[:, None, :]], axis=1).reshape(1, bd, CONV_W - 1, CONV_DIM)
    ssm_state_s = hnew_s.reshape(1, bd, H_B, SSM_HEAD, D_STATE)
    return (y_prompt, y_sample, new_c_p, new_kr_p, conv_p, ssm_state_p, new_c_s, new_kr_s, conv_s, ssm_state_s)
```
